```python
import math
import jax, jax.numpy as jnp
from jax import lax
import numpy as np

D_MODEL = 1024
BATCH = 16
SEQ = 256
DEPTH = 2
DEC_BATCH = 8
DEC_SEQ = 2048
PAST_LEN = 256

GRID_W = 64
HEAD_DIM = 64
BLOCK = 128
WINDOW = 128
ROPE_BASE = 10000.0
EPS = 1e-6
NEG = -1e30
ATTN_SCALE = HEAD_DIM ** -0.5
CONV_CH = D_MODEL // 2
CONV_W = 31
WIN_HEADS = (D_MODEL // 2) // HEAD_DIM
WIN_KV = WIN_HEADS // 4
WIN_G = WIN_HEADS // WIN_KV
IN0_W = 2 * CONV_CH + (WIN_HEADS + 2 * WIN_KV) * HEAD_DIM
DIFF_HEADS = D_MODEL // (2 * HEAD_DIM)
DIFF_V = 2 * HEAD_DIM
DIFF_QK_W = DIFF_HEADS * 2 * HEAD_DIM
IN1_W = 2 * DIFF_QK_W + DIFF_HEADS * DIFF_V
MIX_W = D_MODEL
N_EXPERTS = 32
TOP_K = 4
D_FF = D_MODEL
SWIGLU_LIMIT = 7.0
SWIGLU_ALPHA = 1.702
MOE_BLOCK = 128

kernel_name = 'hybrid_dit_conv_window_diffattn_moe_step'


def rmsnorm(x, g):
    xf = x.astype(jnp.float32)
    y = xf * lax.rsqrt(jnp.mean(xf * xf, axis=-1, keepdims=True) + EPS)
    return (y * g.astype(jnp.float32)).astype(x.dtype)


def layernorm(x, g, b):
    xf = x.astype(jnp.float32)
    mu = jnp.mean(xf, axis=-1, keepdims=True)
    var = jnp.mean(jnp.square(xf - mu), axis=-1, keepdims=True)
    y = (xf - mu) * lax.rsqrt(var + EPS)
    return (y * g.astype(jnp.float32) + b.astype(jnp.float32)).astype(x.dtype)


def modulation(cond, w, b):
    m = jax.nn.silu(cond) @ w + b
    return jnp.split(m[:, None, :], 6, axis=-1)


def ada_norm(x, g, shift, scale):
    return rmsnorm(x, g) * (1 + scale) + shift


def axial_rope(t):
    rows = t // GRID_W
    pos = jnp.arange(rows * GRID_W)
    r = (pos // GRID_W).astype(jnp.float32)
    col = (pos % GRID_W).astype(jnp.float32)
    quarter = HEAD_DIM // 4
    inv = ROPE_BASE ** (-jnp.arange(quarter, dtype=jnp.float32) / quarter)
    ar = r[:, None] * inv
    ac = col[:, None] * inv
    ang = jnp.concatenate([ar, ar, ac, ac], axis=-1)
    return jnp.cos(ang), jnp.sin(ang)


def apply_rope(x, cos, sin):
    x1, x2, x3, x4 = jnp.split(x, 4, axis=-1)
    rot = jnp.concatenate([-x2, x1, -x4, x3], axis=-1)
    c = cos[None, :, None, :].astype(x.dtype)
    s = sin[None, :, None, :].astype(x.dtype)
    return x * c + rot * s


def sink_softmax(s, sink):
    m = jnp.maximum(jnp.max(s, axis=-1, keepdims=True), sink)
    p = jnp.exp(s - m)
    return p / (jnp.sum(p, axis=-1, keepdims=True) + jnp.exp(sink - m))


def conformer_conv(a_val, a_gate, lp):
    u = a_val * jax.nn.sigmoid(a_gate)
    u = lax.conv_general_dilated(
        u, lp['conv_w'][:, None, :], window_strides=(1,),
        padding=[(CONV_W // 2, CONV_W // 2)],
        dimension_numbers=('NWC', 'WIO', 'NWC'),
        feature_group_count=CONV_CH) + lp['conv_b']
    return jax.nn.silu(layernorm(u, lp['cnorm_g'], lp['cnorm_b']))


def dense_sink_gqa(q, k, v, sink):
    bsz, tq = q.shape[:2]
    nq = tq // BLOCK
    qb = q.reshape(bsz, nq, BLOCK, WIN_KV, WIN_G, HEAD_DIM).transpose(1, 0, 2, 3, 4, 5)
    snk = sink.astype(jnp.float32).reshape(1, WIN_KV, WIN_G, 1, 1)

    def one(qblk):
        s = jnp.einsum('bqkgd,bskd->bkgqs', qblk, k).astype(jnp.float32) * ATTN_SCALE
        p = sink_softmax(s, snk)
        return jnp.einsum('bkgqs,bskd->bqkgd', p.astype(v.dtype), v)

    o = lax.map(one, qb)
    return o.transpose(1, 0, 2, 3, 4, 5).reshape(bsz, tq, WIN_HEADS * HEAD_DIM)


def windowed_sink_gqa(q, k, v, kc, vc, sink):
    bsz, t = q.shape[:2]
    nb = t // BLOCK
    qb = q.reshape(bsz, nb, BLOCK, WIN_KV, WIN_G, HEAD_DIM)

    def bands(x):
        xp = jnp.pad(x, ((0, 0), (BLOCK, BLOCK), (0, 0), (0, 0)))
        xp = xp.reshape(bsz, nb + 2, BLOCK, WIN_KV, HEAD_DIM)
        return jnp.concatenate([xp[:, :-2], xp[:, 1:-1], xp[:, 2:]], axis=2)

    kb, vb = bands(k), bands(v)
    s_loc = jnp.einsum('bnqkgd,bnskd->bnkgqs', qb, kb).astype(jnp.float32) * ATTN_SCALE
    s_ctx = jnp.einsum('bnqkgd,bckd->bnkgqc', qb, kc).astype(jnp.float32) * ATTN_SCALE
    qpos = jnp.arange(nb)[:, None, None] * BLOCK + jnp.arange(BLOCK)[None, :, None]
    kpos = (jnp.arange(nb)[:, None, None] - 1) * BLOCK + jnp.arange(3 * BLOCK)[None, None, :]
    valid = (jnp.abs(kpos - qpos) <= WINDOW) & (kpos >= 0) & (kpos < t)
    s_loc = jnp.where(valid[None, :, None, None], s_loc, NEG)
    s = jnp.concatenate([s_loc, s_ctx], axis=-1)
    p = sink_softmax(s, sink.astype(jnp.float32).reshape(1, 1, WIN_KV, WIN_G, 1, 1))
    p_loc = p[..., :3 * BLOCK].astype(v.dtype)
    p_ctx = p[..., 3 * BLOCK:].astype(v.dtype)
    o = (jnp.einsum('bnkgqs,bnskd->bnqkgd', p_loc, vb)
         + jnp.einsum('bnkgqc,bckd->bnqkgd', p_ctx, vc))
    return o.reshape(bsz, t, WIN_HEADS * HEAD_DIM)


def conv_window_mixer(h, lp, rope, ctx_kv):
    bsz, t, _ = h.shape
    splits = (CONV_CH, 2 * CONV_CH, 2 * CONV_CH + WIN_HEADS * HEAD_DIM,
              2 * CONV_CH + (WIN_HEADS + WIN_KV) * HEAD_DIM)
    a_val, a_gate, q, k, v = jnp.split(h @ lp['w_in'], splits, axis=-1)
    a = conformer_conv(a_val, a_gate, lp)
    q = q.reshape(bsz, t, WIN_HEADS, HEAD_DIM)
    k = k.reshape(bsz, t, WIN_KV, HEAD_DIM)
    v = v.reshape(bsz, t, WIN_KV, HEAD_DIM)
    if ctx_kv is None:
        o = dense_sink_gqa(q.reshape(bsz, t, WIN_KV, WIN_G, HEAD_DIM), k, v, lp['sink'])
        new = (k, v)
    else:
        cos, sin = rope
        q = apply_rope(q, cos, sin)
        k = apply_rope(k, cos, sin)
        o = windowed_sink_gqa(q.reshape(bsz, t, WIN_KV, WIN_G, HEAD_DIM), k, v,
                              ctx_kv[0], ctx_kv[1], lp['sink'])
        new = None
    return jnp.concatenate([a, o], axis=-1) @ lp['w_out'], new


def diff_attention(q, k, v, lam):
    bsz, tq = q.shape[:2]
    nq = tq // BLOCK
    qb = q.reshape(bsz, nq, BLOCK, DIFF_HEADS, 2, HEAD_DIM).transpose(1, 0, 2, 3, 4, 5)

    def one(qblk):
        s = jnp.einsum('bqhcd,bkhcd->bhcqk', qblk, k).astype(jnp.float32) * ATTN_SCALE
        p = jax.nn.softmax(s, axis=-1)
        a = p[:, :, 0] - lam * p[:, :, 1]
        return jnp.einsum('bhqk,bkhe->bqhe', a.astype(v.dtype), v)

    o = lax.map(one, qb)
    return o.transpose(1, 0, 2, 3, 4).reshape(bsz, tq, DIFF_HEADS, DIFF_V)


def diff_mixer(h, lp, lam_init, rope, ctx_kv):
    bsz, t, _ = h.shape
    q, k, v = jnp.split(h @ lp['w_in'], (DIFF_QK_W, 2 * DIFF_QK_W), axis=-1)
    q = q.reshape(bsz, t, DIFF_HEADS, 2, HEAD_DIM)
    k = k.reshape(bsz, t, DIFF_HEADS, 2, HEAD_DIM)
    v = v.reshape(bsz, t, DIFF_HEADS, DIFF_V)
    f32 = jnp.float32
    lam = (jnp.exp(jnp.sum(lp['lam_q1'].astype(f32) * lp['lam_k1'].astype(f32)))
           - jnp.exp(jnp.sum(lp['lam_q2'].astype(f32) * lp['lam_k2'].astype(f32))) + lam_init)
    if ctx_kv is None:
        o = diff_attention(q, k, v, lam)
        new = (k, v)
    else:
        cos, sin = rope
        q = apply_rope(q.reshape(bsz, t, 2 * DIFF_HEADS, HEAD_DIM), cos, sin).reshape(q.shape)
        k = apply_rope(k.reshape(bsz, t, 2 * DIFF_HEADS, HEAD_DIM), cos, sin).reshape(k.shape)
        kc, vc = ctx_kv
        o = diff_attention(q, jnp.concatenate([kc, k], axis=1), jnp.concatenate([vc, v], axis=1), lam)
        new = None
    o = rmsnorm(o, lp['subln_g']) * (1.0 - lam_init)
    return o.reshape(bsz, t, DIFF_HEADS * DIFF_V) @ lp['w_out'], new


def moe(x, lp):
    shp = x.shape
    xf = x.reshape(-1, D_MODEL)
    n = xf.shape[0]
    logits = (xf @ lp['router_w'] + lp['router_b']).astype(jnp.float32)
    top_v, top_i = lax.top_k(logits, TOP_K)
    gates = jax.nn.softmax(top_v, axis=-1).astype(x.dtype)
    n_asg = n * TOP_K
    eid = top_i.reshape(-1)
    tok = jnp.repeat(jnp.arange(n, dtype=jnp.int32), TOP_K)
    gw = gates.reshape(-1)
    order = jnp.argsort(eid)
    e_s, tok_s, gw_s = eid[order], tok[order], gw[order]
    counts = jnp.bincount(eid, length=N_EXPERTS)
    padded = (counts + MOE_BLOCK - 1) // MOE_BLOCK * MOE_BLOCK
    start = jnp.cumsum(counts) - counts
    pend = jnp.cumsum(padded)
    pstart = pend - padded
    dest = pstart[e_s] + jnp.arange(n_asg, dtype=jnp.int32) - start[e_s]
    n_blocks = -(-(n_asg + N_EXPERTS * (MOE_BLOCK - 1)) // MOE_BLOCK)
    cap = n_blocks * MOE_BLOCK
    slot_tok = jnp.full((cap,), n, jnp.int32).at[dest].set(tok_s)
    slot_gw = jnp.zeros((cap,), x.dtype).at[dest].set(gw_s)
    block_e = jnp.minimum(
        jnp.searchsorted(pend, jnp.arange(n_blocks, dtype=pend.dtype) * MOE_BLOCK, side='right'),
        N_EXPERTS - 1)
    x_pad = jnp.concatenate([xf, jnp.zeros((1, D_MODEL), xf.dtype)], axis=0)

    def expert_block(args):
        tb, e = args
        xb = x_pad[tb]
        gu = xb @ lp['up_w'][e] + lp['up_b'][e]
        g, lin = jnp.split(gu, 2, axis=-1)
        g = jnp.minimum(g, SWIGLU_LIMIT)
        lin = jnp.clip(lin, -SWIGLU_LIMIT, SWIGLU_LIMIT)
        hdn = g * jax.nn.sigmoid(SWIGLU_ALPHA * g) * (lin + 1)
        return hdn @ lp['down_w'][e] + lp['down_b'][e]

    y = lax.map(expert_block, (slot_tok.reshape(n_blocks, MOE_BLOCK), block_e))
    out = jnp.zeros((n + 1, D_MODEL), x.dtype).at[slot_tok].add(
        y.reshape(cap, D_MODEL) * slot_gw[:, None])
    return out[:n].reshape(shp)


def setup_inputs(seed: int = 0) -> dict:
    key = jax.random.key(seed)
    keys = iter(jax.random.split(key, 96))

    def rnd(shape, scale):
        return jax.random.normal(next(keys), shape, jnp.float32) * scale

    def gain(n):
        return 1.0 + rnd((n,), 0.05)

    D = D_MODEL
    inp = {}
    inp['x_prompt'] = rnd((BATCH, SEQ, D), 1.0)
    inp['x_sample'] = rnd((DEC_BATCH, DEC_SEQ, D), 1.0)
    inp['cache_l0_k'] = rnd((DEC_BATCH, PAST_LEN, WIN_KV, HEAD_DIM), 1.0)
    inp['cache_l0_v'] = rnd((DEC_BATCH, PAST_LEN, WIN_KV, HEAD_DIM), 1.0)
    inp['cache_l1_k'] = rnd((DEC_BATCH, PAST_LEN, DIFF_HEADS, 2, HEAD_DIM), 1.0)
    inp['cache_l1_v'] = rnd((DEC_BATCH, PAST_LEN, DIFF_HEADS, DIFF_V), 1.0)
    inp['c'] = rnd((DEC_BATCH, D), 1.0)
    inp['c_ctx'] = rnd((D,), 1.0)
    inp['final_g'] = gain(D)
    for l in range(DEPTH):
        p = 'l%d_' % l
        inp[p + 'mod_w'] = rnd((D, 6 * D), D ** -0.5)
        inp[p + 'mod_b'] = rnd((6 * D,), 0.02)
        inp[p + 'norm1_g'] = gain(D)
        if l % 2 == 0:
            inp[p + 'w_in'] = rnd((D, IN0_W), D ** -0.5)
            inp[p + 'conv_w'] = rnd((CONV_W, CONV_CH), CONV_W ** -0.5)
            inp[p + 'conv_b'] = rnd((CONV_CH,), 0.02)
            inp[p + 'cnorm_g'] = gain(CONV_CH)
            inp[p + 'cnorm_b'] = rnd((CONV_CH,), 0.02)
            inp[p + 'sink'] = rnd((WIN_HEADS,), 0.5)
        else:
            inp[p + 'w_in'] = rnd((D, IN1_W), D ** -0.5)
            inp[p + 'lam_q1'] = rnd((HEAD_DIM,), 0.1)
            inp[p + 'lam_k1'] = rnd((HEAD_DIM,), 0.1)
            inp[p + 'lam_q2'] = rnd((HEAD_DIM,), 0.1)
            inp[p + 'lam_k2'] = rnd((HEAD_DIM,), 0.1)
            inp[p + 'subln_g'] = gain(DIFF_V)
        inp[p + 'w_out'] = rnd((MIX_W, D), MIX_W ** -0.5)
        inp[p + 'norm2_g'] = gain(D)
        inp[p + 'router_w'] = rnd((D, N_EXPERTS), D ** -0.5)
        inp[p + 'router_b'] = rnd((N_EXPERTS,), 0.01)
        inp[p + 'up_w'] = rnd((N_EXPERTS, D, 2 * D_FF), D ** -0.5)
        inp[p + 'up_b'] = rnd((N_EXPERTS, 2 * D_FF), 0.02)
        inp[p + 'down_w'] = rnd((N_EXPERTS, D_FF, D), D_FF ** -0.5)
        inp[p + 'down_b'] = rnd((N_EXPERTS, D), 0.02)
    return inp


def reference(x_prompt, x_sample, cache_l0_k, cache_l0_v, cache_l1_k, cache_l1_v, c, c_ctx, final_g,
              l0_mod_w, l0_mod_b, l0_norm1_g, l0_w_in, l0_conv_w, l0_conv_b, l0_cnorm_g, l0_cnorm_b,
              l0_sink, l0_w_out, l0_norm2_g, l0_router_w, l0_router_b, l0_up_w, l0_up_b, l0_down_w,
              l0_down_b,
              l1_mod_w, l1_mod_b, l1_norm1_g, l1_w_in, l1_lam_q1, l1_lam_k1, l1_lam_q2, l1_lam_k2,
              l1_subln_g, l1_w_out, l1_norm2_g, l1_router_w, l1_router_b, l1_up_w, l1_up_b, l1_down_w,
              l1_down_b):
    layers = [
        dict(mod_w=l0_mod_w, mod_b=l0_mod_b, norm1_g=l0_norm1_g, w_in=l0_w_in, conv_w=l0_conv_w,
             conv_b=l0_conv_b, cnorm_g=l0_cnorm_g, cnorm_b=l0_cnorm_b, sink=l0_sink, w_out=l0_w_out,
             norm2_g=l0_norm2_g, router_w=l0_router_w, router_b=l0_router_b, up_w=l0_up_w,
             up_b=l0_up_b, down_w=l0_down_w, down_b=l0_down_b),
        dict(mod_w=l1_mod_w, mod_b=l1_mod_b, norm1_g=l1_norm1_g, w_in=l1_w_in, lam_q1=l1_lam_q1,
             lam_k1=l1_lam_k1, lam_q2=l1_lam_q2, lam_k2=l1_lam_k2, subln_g=l1_subln_g,
             w_out=l1_w_out, norm2_g=l1_norm2_g, router_w=l1_router_w, router_b=l1_router_b,
             up_w=l1_up_w, up_b=l1_up_b, down_w=l1_down_w, down_b=l1_down_b),
    ]
    caches = [(cache_l0_k, cache_l0_v), (cache_l1_k, cache_l1_v)]
    rope = axial_rope(x_sample.shape[1])
    xp, xs = x_prompt, x_sample
    new_state = []
    for i in range(DEPTH):
        lp = layers[i]
        sh1, sc1, g1, sh2, sc2, g2 = modulation(c_ctx[None, :], lp['mod_w'], lp['mod_b'])
        lsh1, lsc1, lg1, lsh2, lsc2, lg2 = modulation(c, lp['mod_w'], lp['mod_b'])
        hp = ada_norm(xp, lp['norm1_g'], sh1, sc1)
        hs = ada_norm(xs, lp['norm1_g'], lsh1, lsc1)
        if i % 2 == 0:
            op, kv = conv_window_mixer(hp, lp, None, None)
            os_, _ = conv_window_mixer(hs, lp, rope, caches[i])
        else:
            lam_init = 0.8 - 0.6 * math.exp(-0.3 * i)
            op, kv = diff_mixer(hp, lp, lam_init, None, None)
            os_, _ = diff_mixer(hs, lp, lam_init, rope, caches[i])
        new_state.extend(kv)
        xp = xp + g1 * op
        xs = xs + lg1 * os_
        xp = xp + g2 * moe(ada_norm(xp, lp['norm2_g'], sh2, sc2), lp)
        xs = xs + lg2 * moe(ada_norm(xs, lp['norm2_g'], lsh2, lsc2), lp)
    y_prompt = rmsnorm(xp, final_g)
    y_sample = rmsnorm(xs, final_g)
    state_l0_k, state_l0_v, state_l1_k, state_l1_v = new_state
    return (y_prompt, y_sample, state_l0_k, state_l0_v, state_l1_k, state_l1_v)
```

```python
import functools
import math

import jax
import jax.numpy as jnp
from jax import lax
from jax.experimental import pallas as pl
from jax.experimental.pallas import tpu as pltpu

F32 = jnp.float32
BF16 = jnp.bfloat16
I32 = jnp.int32

D_MODEL = 1024
BATCH = 16
SEQ = 256
DEC_BATCH = 8
DEC_SEQ = 2048
PAST_LEN = 256
GRID_W = 64
HEAD_DIM = 64
BLOCK = 128
WINDOW = 128
ROPE_BASE = 10000.0
EPS = 1e-6
NEG = -1e30
ATTN_SCALE = HEAD_DIM ** -0.5
CONV_CH = 512
CONV_W = 31
WIN_HEADS = 8
WIN_KV = 2
WIN_G = 4
IN0_W = 1792
DIFF_HEADS = 8
DIFF_V = 128
IN1_W = 3072
N_EXPERTS = 32
TOP_K = 4
D_FF = 1024
SWIGLU_LIMIT = 7.0
SWIGLU_ALPHA = 1.702

NP = BATCH * SEQ
NS = DEC_BATCH * DEC_SEQ
N = NP + NS
N_ASG = N * TOP_K

LANES = 128
SUBLANES = 8
ROW_TILES = D_MODEL // LANES
VMEM_LIMIT = 56 * 1024 * 1024

TM = 512
TMC = 256
TME = 256
CB = 256
HALO = 16
TQ = 256
N_EBLOCKS = -(-(N_ASG + N_EXPERTS * (TME - 1)) // TME)
CAP = N_EBLOCKS * TME

_NT = (((1,), (1,)), ((), ()))


def _cparams(sem):
    return pltpu.CompilerParams(dimension_semantics=sem, vmem_limit_bytes=VMEM_LIMIT)


def _mod_row(i, tm):
    npb = NP // tm
    return jnp.where(i < npb, DEC_BATCH, (i - npb) // (DEC_SEQ // tm))


def _mod_spec(j, tm):
    return pl.BlockSpec((1, 1, D_MODEL), lambda i, *_: (_mod_row(i, tm) * 6 + j, 0, 0))


def _const_spec(shape):
    nd = len(shape)
    return pl.BlockSpec(shape, lambda *_: (0,) * nd)


def _ada_norm(x, g, shift, scale):
    ms = jnp.mean(x * x, axis=-1, keepdims=True)
    return (x * lax.rsqrt(ms + EPS) * g) * (1.0 + scale) + shift


def _mod_kernel(c_ref, w_ref, b_ref, o_ref):
    c = c_ref[...]
    s = c * jax.nn.sigmoid(c)
    o_ref[...] = jnp.dot(s.astype(BF16), w_ref[...].astype(BF16), preferred_element_type=F32) + b_ref[...]


def _modulation(cond16, w, b):
    m = pl.pallas_call(
        _mod_kernel,
        grid=(6,),
        in_specs=[_const_spec((16, D_MODEL)),
                  pl.BlockSpec((D_MODEL, D_MODEL), lambda j: (0, j)),
                  pl.BlockSpec((1, D_MODEL), lambda j: (0, j))],
        out_specs=pl.BlockSpec((16, D_MODEL), lambda j: (0, j)),
        out_shape=jax.ShapeDtypeStruct((16, 6 * D_MODEL), F32),
        compiler_params=_cparams(("arbitrary",)),
        name="modulation",
    )(cond16, w, b.reshape(1, -1))
    return m.reshape(16 * 6, 1, D_MODEL)


def _rope(v, cos, sa, sb):
    return v * cos + pltpu.roll(v, LANES - 16, 1) * sa + pltpu.roll(v, 16, 1) * sb


def _inproj_kernel(x_ref, sh_ref, sc_ref, g_ref, w_ref, cos_ref, sa_ref, sb_ref, *outs, layer):
    i = pl.program_id(0)
    h = _ada_norm(x_ref[...], g_ref[...], sh_ref[0], sc_ref[0])
    acc = jnp.dot(h.astype(BF16), w_ref[...], preferred_element_type=F32)
    if layer == 0:
        ha_ref, hq_ref, st_ref = outs
        ha_ref[...] = acc[:, :2 * CONV_CH]
        base, n_q, n_rope, n_all, st_lo = 2 * CONV_CH, 4, 5, 6, 2 * CONV_CH + 512
    else:
        hq_ref, st_ref = outs
        base, n_q, n_rope, n_all, st_lo = 0, 8, 16, 24, 1024
    is_prompt = i < NP // TM

    def chunk(c):
        v = acc[:, base + c * LANES: base + (c + 1) * LANES]
        return v * ATTN_SCALE if c < n_q else v

    @pl.when(is_prompt)
    def _():
        for c in range(n_all):
            hq_ref[:, c * LANES:(c + 1) * LANES] = chunk(c).astype(BF16)
        st_ref[...] = acc[:, st_lo:]

    @pl.when(jnp.logical_not(is_prompt))
    def _():
        cos, sa, sb = cos_ref[...], sa_ref[...], sb_ref[...]
        for c in range(n_all):
            v = chunk(c)
            if c < n_rope:
                v = _rope(v, cos, sa, sb)
            hq_ref[:, c * LANES:(c + 1) * LANES] = v.astype(BF16)


def _inproj(x, mods, g, w_bf, rope_tabs, layer):
    npb = NP // TM
    spb = DEC_SEQ // TM
    nout = w_bf.shape[1]
    rope_spec = pl.BlockSpec((TM, LANES), lambda i: (jnp.where(i < npb, 0, (i - npb) % spb), 0))
    st_w = 256 if layer == 0 else 2048
    st_spec = pl.BlockSpec((TM, st_w), lambda i: (jnp.minimum(i, npb - 1), 0))
    if layer == 0:
        out_shape = (jax.ShapeDtypeStruct((N, 2 * CONV_CH), F32),
                     jax.ShapeDtypeStruct((N, 768), BF16),
                     jax.ShapeDtypeStruct((NP, st_w), F32))
        out_specs = (pl.BlockSpec((TM, 2 * CONV_CH), lambda i: (i, 0)),
                     pl.BlockSpec((TM, 768), lambda i: (i, 0)), st_spec)
    else:
        out_shape = (jax.ShapeDtypeStruct((N, IN1_W), BF16),
                     jax.ShapeDtypeStruct((NP, st_w), F32))
        out_specs = (pl.BlockSpec((TM, IN1_W), lambda i: (i, 0)), st_spec)
    return pl.pallas_call(
        functools.partial(_inproj_kernel, layer=layer),
        grid=(N // TM,),
        in_specs=[pl.BlockSpec((TM, D_MODEL), lambda i: (i, 0)),
                  _mod_spec(0, TM), _mod_spec(1, TM),
                  _const_spec((1, D_MODEL)),
                  _const_spec((D_MODEL, nout)),
                  rope_spec, rope_spec, rope_spec],
        out_specs=out_specs,
        out_shape=out_shape,
        compiler_params=_cparams(("arbitrary",)),
        name="inproj_l%d" % layer,
    )(x, mods, mods, g.reshape(1, -1), w_bf, *rope_tabs)


def _conv_kernel(prev_ref, cur_ref, next_ref, w_ref, b_ref, g_ref, bb_ref, o_ref, upad):
    i = pl.program_id(0)
    npb = NP // CB
    spb = DEC_SEQ // CB
    j = (i - npb) % spb
    first = jnp.logical_or(i < npb, j == 0)
    last = jnp.logical_or(i < npb, j == spb - 1)

    def glu(r):
        return r[:, :CONV_CH] * jax.nn.sigmoid(r[:, CONV_CH:])

    upad[HALO:HALO + CB, :] = glu(cur_ref[...])
    upad[0:HALO, :] = jnp.where(first, 0.0, glu(prev_ref[...]))
    upad[HALO + CB:, :] = jnp.where(last, 0.0, glu(next_ref[...]))
    acc = jnp.zeros((CB, CONV_CH), F32)
    off = HALO - CONV_W // 2
    for t in range(CONV_W):
        acc = acc + w_ref[t:t + 1, :] * upad[off + t:off + t + CB, :]
    u = acc + b_ref[...]
    mu = jnp.mean(u, axis=-1, keepdims=True)
    var = jnp.mean(jnp.square(u - mu), axis=-1, keepdims=True)
    y = (u - mu) * lax.rsqrt(var + EPS) * g_ref[...] + bb_ref[...]
    o_ref[...] = (y * jax.nn.sigmoid(y)).astype(BF16)


def _conv(ha, conv_w, conv_b, cg, cb):
    hb = CB // HALO
    nh = N // HALO
    w_pad = jnp.concatenate([conv_w, jnp.zeros((1, CONV_CH), F32)], axis=0)
    return pl.pallas_call(
        _conv_kernel,
        grid=(N // CB,),
        in_specs=[pl.BlockSpec((HALO, 2 * CONV_CH), lambda i: (jnp.maximum(i * hb - 1, 0), 0)),
                  pl.BlockSpec((CB, 2 * CONV_CH), lambda i: (i, 0)),
                  pl.BlockSpec((HALO, 2 * CONV_CH), lambda i: (jnp.minimum((i + 1) * hb, nh - 1), 0)),
                  _const_spec((CONV_W + 1, CONV_CH)),
                  _const_spec((1, CONV_CH)), _const_spec((1, CONV_CH)), _const_spec((1, CONV_CH))],
        out_specs=pl.BlockSpec((CB, CONV_CH), lambda i: (i, 0)),
        out_shape=jax.ShapeDtypeStruct((N, CONV_CH), BF16),
        scratch_shapes=[pltpu.VMEM((CB + 2 * HALO, CONV_CH), F32)],
        compiler_params=_cparams(("arbitrary",)),
        name="conformer_conv",
    )(ha, ha, ha, w_pad, conv_b.reshape(1, -1), cg.reshape(1, -1), cb.reshape(1, -1))


def _sink_attend(q, k, v, sink, mask):
    s = lax.dot_general(q, k, _NT, preferred_element_type=F32)
    if mask is not None:
        s = jnp.where(mask, s, NEG)
    m = jnp.maximum(jnp.max(s, axis=-1, keepdims=True), sink)
    p = jnp.exp(s - m)
    den = jnp.sum(p, axis=-1, keepdims=True) + jnp.exp(sink - m)
    return jnp.dot(p.astype(BF16), v, preferred_element_type=F32) / den


def _l0_prompt_attn_kernel(sink_ref, q_ref, k_ref, v_ref, o_ref):
    k = k_ref[...]
    v = v_ref[...]
    outs = []
    for h in range(WIN_HEADS):
        j = h // WIN_G
        outs.append(_sink_attend(q_ref[:, h * HEAD_DIM:(h + 1) * HEAD_DIM],
                                 k[:, j * HEAD_DIM:(j + 1) * HEAD_DIM],
                                 v[:, j * HEAD_DIM:(j + 1) * HEAD_DIM], sink_ref[h], None))
    o_ref[...] = jnp.concatenate(outs, axis=1).astype(BF16)


def _l0_prompt_attn(hq, sink):
    return pl.pallas_call(
        _l0_prompt_attn_kernel,
        grid=(BATCH,),
        in_specs=[pl.BlockSpec(memory_space=pltpu.SMEM),
                  pl.BlockSpec((SEQ, 512), lambda b: (b, 0)),
                  pl.BlockSpec((SEQ, LANES), lambda b: (b, 4)),
                  pl.BlockSpec((SEQ, LANES), lambda b: (b, 5))],
        out_specs=pl.BlockSpec((SEQ, 512), lambda b: (b, 0)),
        out_shape=jax.ShapeDtypeStruct((NP, 512), BF16),
        compiler_params=_cparams(("arbitrary",)),
        name="l0_prompt_attn",
    )(sink, hq, hq, hq)


def _l0_window_attn_kernel(sink_ref, q_ref, kp_ref, kc_ref, kn_ref, vp_ref, vc_ref, vn_ref,
                           ck_ref, cv_ref, o_ref):
    n = pl.program_id(1)
    k = jnp.concatenate([kp_ref[...], kc_ref[...], kn_ref[...], ck_ref[...].astype(BF16)], axis=0)
    v = jnp.concatenate([vp_ref[...], vc_ref[...], vn_ref[...], cv_ref[...].astype(BF16)], axis=0)
    qpos = n * BLOCK + lax.broadcasted_iota(I32, (BLOCK, 3 * BLOCK + PAST_LEN), 0)
    col = lax.broadcasted_iota(I32, (BLOCK, 3 * BLOCK + PAST_LEN), 1)
    kpos = (n - 1) * BLOCK + col
    local_ok = (jnp.abs(kpos - qpos) <= WINDOW) & (kpos >= 0) & (kpos < DEC_SEQ)
    mask = jnp.logical_or(col >= 3 * BLOCK, local_ok)
    outs = []
    for h in range(WIN_HEADS):
        j = h // WIN_G
        outs.append(_sink_attend(q_ref[:, h * HEAD_DIM:(h + 1) * HEAD_DIM],
                                 k[:, j * HEAD_DIM:(j + 1) * HEAD_DIM],
                                 v[:, j * HEAD_DIM:(j + 1) * HEAD_DIM], sink_ref[h], mask))
    o_ref[...] = jnp.concatenate(outs, axis=1).astype(BF16)


def _l0_window_attn(hq, sink, ck, cv):
    nb = DEC_SEQ // BLOCK
    r0 = NP // BLOCK

    def kv_spec(col, d):
        return pl.BlockSpec((BLOCK, LANES),
                            lambda b, n: (r0 + b * nb + jnp.clip(n + d, 0, nb - 1), col))

    ctx_spec = pl.BlockSpec((None, PAST_LEN, LANES), lambda b, n: (b, 0, 0))
    return pl.pallas_call(
        _l0_window_attn_kernel,
        grid=(DEC_BATCH, nb),
        in_specs=[pl.BlockSpec(memory_space=pltpu.SMEM),
                  pl.BlockSpec((BLOCK, 512), lambda b, n: (r0 + b * nb + n, 0)),
                  kv_spec(4, -1), kv_spec(4, 0), kv_spec(4, 1),
                  kv_spec(5, -1), kv_spec(5, 0), kv_spec(5, 1),
                  ctx_spec, ctx_spec],
        out_specs=pl.BlockSpec((BLOCK, 512), lambda b, n: (b * nb + n, 0)),
        out_shape=jax.ShapeDtypeStruct((NS, 512), BF16),
        compiler_params=_cparams(("arbitrary", "arbitrary")),
        name="l0_window_attn",
    )(sink, hq, hq, hq, hq, hq, hq, hq, ck, cv)


def _diff_attn_kernel(*refs, lam_init, has_ctx):
    if has_ctx:
        q_ref, k_ref, v_ref, ck_ref, cv_ref, lam_ref, g_ref, o_ref = refs
    else:
        q_ref, k_ref, v_ref, lam_ref, g_ref, o_ref = refs
    lv = lam_ref[...]
    lam = (jnp.exp(jnp.sum(lv[0:1] * lv[1:2], axis=-1, keepdims=True))
           - jnp.exp(jnp.sum(lv[2:3] * lv[3:4], axis=-1, keepdims=True)) + lam_init)
    g = g_ref[...]
    outs = []
    for h in range(DIFF_HEADS):
        probs = []
        for c in range(2):
            lo = h * DIFF_V + c * HEAD_DIM
            q = q_ref[:, lo:lo + HEAD_DIM]
            s = lax.dot_general(q, k_ref[:, lo:lo + HEAD_DIM], _NT, preferred_element_type=F32)
            m = jnp.max(s, axis=-1, keepdims=True)
            if has_ctx:
                sc = lax.dot_general(q, ck_ref[:, lo:lo + HEAD_DIM].astype(BF16), _NT,
                                     preferred_element_type=F32)
                m = jnp.maximum(m, jnp.max(sc, axis=-1, keepdims=True))
                pc = jnp.exp(sc - m)
            p = jnp.exp(s - m)
            den = jnp.sum(p, axis=-1, keepdims=True)
            if has_ctx:
                den = den + jnp.sum(pc, axis=-1, keepdims=True)
                probs.append((p, pc, den))
            else:
                probs.append((p, None, den))
        (p0, pc0, d0), (p1, pc1, d1) = probs
        r0 = 1.0 / d0
        r1 = lam / d1
        a = (p0 * r0 - p1 * r1).astype(BF16)
        o = jnp.dot(a, v_ref[:, h * DIFF_V:(h + 1) * DIFF_V], preferred_element_type=F32)
        if has_ctx:
            ac = (pc0 * r0 - pc1 * r1).astype(BF16)
            o = o + jnp.dot(ac, cv_ref[:, h * DIFF_V:(h + 1) * DIFF_V].astype(BF16),
                            preferred_element_type=F32)
        ms = jnp.mean(o * o, axis=-1, keepdims=True)
        outs.append(((o * lax.rsqrt(ms + EPS)) * g) * (1.0 - lam_init))
    o_ref[...] = jnp.concatenate(outs, axis=1).astype(BF16)


def _diff_attn(h1, lam_vecs, subln_g, lam_init, ctx):
    g = subln_g.reshape(1, DIFF_V)
    small = [_const_spec((4, HEAD_DIM)), _const_spec((1, DIFF_V))]
    if ctx is None:
        grid = (BATCH, 1)
        in_specs = [pl.BlockSpec((SEQ, D_MODEL), lambda b, i: (b, 0)),
                    pl.BlockSpec((SEQ, D_MODEL), lambda b, i: (b, 1)),
                    pl.BlockSpec((SEQ, D_MODEL), lambda b, i: (b, 2))] + small
        out_specs = pl.BlockSpec((SEQ, D_MODEL), lambda b, i: (b, 0))
        rows = NP
        args = (h1, h1, h1, lam_vecs, g)
        name = "diff_attn_prompt"
    else:
        nq = DEC_SEQ // TQ
        q0 = NP // TQ
        s0 = NP // DEC_SEQ
        ctx_spec = pl.BlockSpec((None, PAST_LEN, D_MODEL), lambda b, i: (b, 0, 0))
        grid = (DEC_BATCH, nq)
        in_specs = [pl.BlockSpec((TQ, D_MODEL), lambda b, i: (q0 + b * nq + i, 0)),
                    pl.BlockSpec((DEC_SEQ, D_MODEL), lambda b, i: (s0 + b, 1)),
                    pl.BlockSpec((DEC_SEQ, D_MODEL), lambda b, i: (s0 + b, 2)),
                    ctx_spec, ctx_spec] + small
        out_specs = pl.BlockSpec((TQ, D_MODEL), lambda b, i: (b * nq + i, 0))
        rows = NS
        args = (h1, h1, h1, ctx[0], ctx[1], lam_vecs, g)
        name = "diff_attn_sample"
    return pl.pallas_call(
        functools.partial(_diff_attn_kernel, lam_init=lam_init, has_ctx=ctx is not None),
        grid=grid, in_specs=in_specs, out_specs=out_specs,
        out_shape=jax.ShapeDtypeStruct((rows, D_MODEL), BF16),
        compiler_params=_cparams(("arbitrary", "arbitrary")),
        name=name,
    )(*args)


def _store_row_tiles(ref, val, rows):
    for s in range(ROW_TILES):
        ref[pl.ds(s, rows, stride=ROW_TILES), :] = val[:, s * LANES:(s + 1) * LANES]


def _post_kernel(*refs, layer):
    if layer == 0:
        (a_ref, op_ref, os_ref, x_ref, g1_ref, sh2_ref, sc2_ref, w_ref, n2g_ref, rw_ref, rb_ref,
         xo_ref, xn_ref, eid_ref, gate_ref) = refs
    else:
        (op_ref, os_ref, x_ref, g1_ref, sh2_ref, sc2_ref, w_ref, n2g_ref, rw_ref, rb_ref,
         xo_ref, xn_ref, eid_ref, gate_ref) = refs
    i = pl.program_id(0)
    o = jnp.where(i < NP // TM, op_ref[...], os_ref[...])
    if layer == 0:
        mix = (jnp.dot(a_ref[...], w_ref[:CONV_CH, :], preferred_element_type=F32)
               + jnp.dot(o, w_ref[CONV_CH:, :], preferred_element_type=F32))
    else:
        mix = jnp.dot(o, w_ref[...], preferred_element_type=F32)
    x1 = x_ref[...] + g1_ref[0] * mix
    xo_ref[...] = x1
    xn = _ada_norm(x1, n2g_ref[...], sh2_ref[0], sc2_ref[0])
    _store_row_tiles(xn_ref, xn, TM)
    logits = jnp.dot(xn, rw_ref[...], preferred_element_type=F32,
                     precision=lax.Precision.HIGHEST) + rb_ref[...]
    lane = lax.broadcasted_iota(I32, logits.shape, 1)
    vals, idxs = [], []
    for _ in range(TOP_K):
        m = jnp.max(logits, axis=-1, keepdims=True)
        idx = jnp.min(jnp.where(logits == m, lane, N_EXPERTS), axis=-1, keepdims=True)
        vals.append(m)
        idxs.append(idx)
        logits = jnp.where(lane == idx, -jnp.inf, logits)
    es = [jnp.exp(v - vals[0]) for v in vals]
    den = es[0] + es[1] + es[2] + es[3]
    eid_ref[...] = jnp.concatenate(idxs, axis=1)
    gate_ref[...] = jnp.concatenate([e / den for e in es], axis=1)


def _post(layer, mix_parts, x, mods, w_bf, n2g, rw, rb):
    npb = NP // TM
    if layer == 0:
        a, o_p, o_s = mix_parts
        wo = 512
        mix_specs = [pl.BlockSpec((TM, CONV_CH), lambda i: (i, 0))]
        mix_args = [a, o_p, o_s]
    else:
        o_p, o_s = mix_parts
        wo = D_MODEL
        mix_specs = []
        mix_args = [o_p, o_s]
    mix_specs += [pl.BlockSpec((TM, wo), lambda i: (jnp.minimum(i, npb - 1), 0)),
                  pl.BlockSpec((TM, wo), lambda i: (jnp.maximum(i - npb, 0), 0))]
    return pl.pallas_call(
        functools.partial(_post_kernel, layer=layer),
        grid=(N // TM,),
        in_specs=mix_specs + [pl.BlockSpec((TM, D_MODEL), lambda i: (i, 0)),
                              _mod_spec(2, TM), _mod_spec(3, TM), _mod_spec(4, TM),
                              _const_spec((D_MODEL, D_MODEL)), _const_spec((1, D_MODEL)),
                              _const_spec((D_MODEL, N_EXPERTS)), _const_spec((1, N_EXPERTS))],
        out_specs=(pl.BlockSpec((TM, D_MODEL), lambda i: (i, 0)),
                   pl.BlockSpec((TM * ROW_TILES, LANES), lambda i: (i, 0)),
                   pl.BlockSpec((TM, TOP_K), lambda i: (i, 0)),
                   pl.BlockSpec((TM, TOP_K), lambda i: (i, 0))),
        out_shape=(jax.ShapeDtypeStruct((N, D_MODEL), F32),
                   jax.ShapeDtypeStruct((N * ROW_TILES, LANES), F32),
                   jax.ShapeDtypeStruct((N, TOP_K), I32),
                   jax.ShapeDtypeStruct((N, TOP_K), F32)),
        compiler_params=_cparams(("arbitrary",)),
        name="post_l%d" % layer,
    )(*mix_args, x, mods, mods, mods, w_bf, n2g.reshape(1, -1), rw, rb.reshape(1, -1))


def _route(eid):
    e_flat = eid.reshape(-1)
    oh = (e_flat[:, None] == jnp.arange(N_EXPERTS, dtype=I32)[None, :]).astype(I32)
    csum = jnp.cumsum(oh, axis=0)
    rank = jnp.sum(oh * csum, axis=1) - 1
    counts = csum[-1]
    nblk = (counts + TME - 1) // TME
    bend = jnp.cumsum(nblk)
    bstart = bend - nblk
    dest = (bstart[e_flat] * TME + rank).astype(I32)
    slot_tok = jnp.zeros((CAP,), I32).at[dest].set(jnp.arange(N_ASG, dtype=I32) // TOP_K)
    blk = jnp.arange(N_EBLOCKS, dtype=I32)
    n_used = bend[-1]
    last_e = jnp.max(jnp.where(counts > 0, jnp.arange(N_EXPERTS, dtype=I32), 0))
    blk_e = jnp.minimum(jnp.searchsorted(bend, blk, side="right"), N_EXPERTS - 1).astype(I32)
    blk_e = jnp.where(blk < n_used, blk_e, last_e)
    nvalid = jnp.clip(counts[blk_e] - (blk - bstart[blk_e]) * TME, 0, TME)
    nvalid = jnp.where(blk < n_used, nvalid, 0).astype(I32)
    return dest, slot_tok, blk_e, nvalid


def _expert_kernel(slot_ref, blke_ref, nval_ref, xn_hbm, upw_ref, upb_ref, dww_ref, dwb_ref,
                   y_ref, gbuf, upbf, dwbf, sem):
    i = pl.program_id(0)
    nv = nval_ref[i]

    @pl.when(nv > 0)
    def _():
        def issue(r, carry):
            tok = slot_ref[i * TME + r]
            pltpu.make_async_copy(
                xn_hbm.at[pl.ds(pl.multiple_of(tok * ROW_TILES, ROW_TILES), ROW_TILES), :],
                gbuf.at[pl.ds(pl.multiple_of(r * ROW_TILES, ROW_TILES), ROW_TILES), :],
                sem).start()
            return carry

        lax.fori_loop(0, TME, issue, 0)
        e = blke_ref[i]
        prev = blke_ref[jnp.maximum(i - 1, 0)]

        @pl.when(jnp.logical_or(i == 0, e != prev))
        def _():
            upbf[...] = upw_ref[0].astype(BF16)
            dwbf[...] = dww_ref[0].astype(BF16)

        pltpu.make_async_copy(xn_hbm.at[pl.ds(0, TME * ROW_TILES), :], gbuf, sem).wait()
        xb = jnp.concatenate(
            [gbuf[pl.ds(s, TME, stride=ROW_TILES), :] for s in range(ROW_TILES)], axis=1).astype(BF16)
        gu = jnp.dot(xb, upbf[...], preferred_element_type=F32) + upb_ref[0]
        g = jnp.minimum(gu[:, :D_FF], SWIGLU_LIMIT)
        lin = jnp.clip(gu[:, D_FF:], -SWIGLU_LIMIT, SWIGLU_LIMIT)
        hdn = g * jax.nn.sigmoid(SWIGLU_ALPHA * g) * (lin + 1.0)
        y = jnp.dot(hdn.astype(BF16), dwbf[...], preferred_element_type=F32) + dwb_ref[0]
        _store_row_tiles(y_ref, y, TME)

    @pl.when(nv == 0)
    def _():
        y_ref[...] = jnp.zeros_like(y_ref)


def _experts(slot_tok, blk_e, nvalid, xn_tiles, up_w, up_b, down_w, down_b):
    grid_spec = pltpu.PrefetchScalarGridSpec(
        num_scalar_prefetch=3,
        grid=(N_EBLOCKS,),
        in_specs=[pl.BlockSpec(memory_space=pl.ANY),
                  pl.BlockSpec((1, D_MODEL, 2 * D_FF), lambda i, s, be, nv: (be[i], 0, 0)),
                  pl.BlockSpec((1, 1, 2 * D_FF), lambda i, s, be, nv: (be[i], 0, 0)),
                  pl.BlockSpec((1, D_FF, D_MODEL), lambda i, s, be, nv: (be[i], 0, 0)),
                  pl.BlockSpec((1, 1, D_MODEL), lambda i, s, be, nv: (be[i], 0, 0))],
        out_specs=pl.BlockSpec((TME * ROW_TILES, LANES), lambda i, s, be, nv: (i, 0)),
        scratch_shapes=[pltpu.VMEM((TME * ROW_TILES, LANES), F32),
                        pltpu.VMEM((D_MODEL, 2 * D_FF), BF16),
                        pltpu.VMEM((D_FF, D_MODEL), BF16),
                        pltpu.SemaphoreType.DMA],
    )
    return pl.pallas_call(
        _expert_kernel,
        grid_spec=grid_spec,
        out_shape=jax.ShapeDtypeStruct((CAP * ROW_TILES, LANES), F32),
        compiler_params=_cparams(("arbitrary",)),
        name="experts",
    )(slot_tok, blk_e, nvalid, xn_tiles, up_w, up_b.reshape(N_EXPERTS, 1, -1),
      down_w, down_b.reshape(N_EXPERTS, 1, -1))


def _combine_kernel(dest_ref, x_ref, gate_ref, g2_ref, fg_ref, y_hbm, o_ref, buf, sem, *, final):
    i = pl.program_id(0)
    n_rows = TMC * TOP_K

    def issue(a, carry):
        d = dest_ref[i * n_rows + a]
        pltpu.make_async_copy(
            y_hbm.at[pl.ds(pl.multiple_of(d * ROW_TILES, ROW_TILES), ROW_TILES), :],
            buf.at[pl.ds(pl.multiple_of(a * ROW_TILES, ROW_TILES), ROW_TILES), :],
            sem).start()
        return carry

    lax.fori_loop(0, n_rows, issue, 0)
    pltpu.make_async_copy(y_hbm.at[pl.ds(0, n_rows * ROW_TILES), :], buf, sem).wait()
    gate = gate_ref[...]
    cols = []
    for s in range(ROW_TILES):
        acc = None
        for k in range(TOP_K):
            term = gate[:, k:k + 1] * buf[pl.ds(k * ROW_TILES + s, TMC, stride=TOP_K * ROW_TILES), :]
            acc = term if acc is None else acc + term
        cols.append(acc)
    x2 = x_ref[...] + g2_ref[0] * jnp.concatenate(cols, axis=1)
    if final:
        ms = jnp.mean(x2 * x2, axis=-1, keepdims=True)
        x2 = x2 * lax.rsqrt(ms + EPS) * fg_ref[...]
    o_ref[...] = x2


def _combine(dest, x, gates, mods, final_g, y_tiles, final):
    grid_spec = pltpu.PrefetchScalarGridSpec(
        num_scalar_prefetch=1,
        grid=(N // TMC,),
        in_specs=[pl.BlockSpec((TMC, D_MODEL), lambda i, d: (i, 0)),
                  pl.BlockSpec((TMC, TOP_K), lambda i, d: (i, 0)),
                  _mod_spec(5, TMC),
                  _const_spec((1, D_MODEL)),
                  pl.BlockSpec(memory_space=pl.ANY)],
        out_specs=pl.BlockSpec((TMC, D_MODEL), lambda i, d: (i, 0)),
        scratch_shapes=[pltpu.VMEM((TMC * TOP_K * ROW_TILES, LANES), F32),
                        pltpu.SemaphoreType.DMA],
    )
    return pl.pallas_call(
        functools.partial(_combine_kernel, final=final),
        grid_spec=grid_spec,
        out_shape=jax.ShapeDtypeStruct((N, D_MODEL), F32),
        compiler_params=_cparams(("arbitrary",)),
        name="combine",
    )(dest, x, gates, mods, final_g.reshape(1, -1), y_tiles)


def _rope_tables():
    pos = jnp.arange(DEC_SEQ)
    r = (pos // GRID_W).astype(F32)
    col = (pos % GRID_W).astype(F32)
    quarter = HEAD_DIM // 4
    inv = ROPE_BASE ** (-jnp.arange(quarter, dtype=F32) / quarter)
    ar = r[:, None] * inv
    ac = col[:, None] * inv
    ang = jnp.concatenate([ar, ar, ac, ac], axis=-1)
    cos = jnp.tile(jnp.cos(ang), (1, LANES // HEAD_DIM))
    sin = jnp.tile(jnp.sin(ang), (1, LANES // HEAD_DIM))
    first = (jnp.arange(LANES) % 32) < 16
    sa = jnp.where(first[None, :], -sin, 0.0)
    sb = jnp.where(first[None, :], 0.0, sin)
    return cos, sa, sb


@jax.jit
def kernel(x_prompt, x_sample, cache_l0_k, cache_l0_v, cache_l1_k, cache_l1_v, c, c_ctx, final_g,
           l0_mod_w, l0_mod_b, l0_norm1_g, l0_w_in, l0_conv_w, l0_conv_b, l0_cnorm_g, l0_cnorm_b,
           l0_sink, l0_w_out, l0_norm2_g, l0_router_w, l0_router_b, l0_up_w, l0_up_b, l0_down_w,
           l0_down_b,
           l1_mod_w, l1_mod_b, l1_norm1_g, l1_w_in, l1_lam_q1, l1_lam_k1, l1_lam_q2, l1_lam_k2,
           l1_subln_g, l1_w_out, l1_norm2_g, l1_router_w, l1_router_b, l1_up_w, l1_up_b, l1_down_w,
           l1_down_b):
    x = jnp.concatenate([x_prompt.reshape(NP, D_MODEL), x_sample.reshape(NS, D_MODEL)], axis=0)
    cond16 = jnp.concatenate([c, c_ctx[None, :], jnp.zeros((16 - DEC_BATCH - 1, D_MODEL), F32)], axis=0)
    rope_tabs = _rope_tables()

    mods = _modulation(cond16, l0_mod_w, l0_mod_b)
    ha, hq, st0 = _inproj(x, mods, l0_norm1_g, l0_w_in.astype(BF16), rope_tabs, 0)
    a = _conv(ha, l0_conv_w, l0_conv_b, l0_cnorm_g, l0_cnorm_b)
    o_p = _l0_prompt_attn(hq, l0_sink)
    o_s = _l0_window_attn(hq, l0_sink, cache_l0_k.reshape(DEC_BATCH, PAST_LEN, LANES),
                          cache_l0_v.reshape(DEC_BATCH, PAST_LEN, LANES))
    x, xn, eid, gates = _post(0, (a, o_p, o_s), x, mods, l0_w_out.astype(BF16), l0_norm2_g,
                              l0_router_w, l0_router_b)
    dest, slot_tok, blk_e, nvalid = _route(eid)
    y = _experts(slot_tok, blk_e, nvalid, xn, l0_up_w, l0_up_b, l0_down_w, l0_down_b)
    x = _combine(dest, x, gates, mods, final_g, y, False)
    state_l0_k = st0[:, :LANES].reshape(BATCH, SEQ, WIN_KV, HEAD_DIM)
    state_l0_v = st0[:, LANES:].reshape(BATCH, SEQ, WIN_KV, HEAD_DIM)

    lam_init = 0.8 - 0.6 * math.exp(-0.3 * 1)
    mods = _modulation(cond16, l1_mod_w, l1_mod_b)
    h1, st1 = _inproj(x, mods, l1_norm1_g, l1_w_in.astype(BF16), rope_tabs, 1)
    lam_vecs = jnp.stack([l1_lam_q1, l1_lam_k1, l1_lam_q2, l1_lam_k2], axis=0)
    o_p = _diff_attn(h1, lam_vecs, l1_subln_g, lam_init, None)
    o_s = _diff_attn(h1, lam_vecs, l1_subln_g, lam_init,
                     (cache_l1_k.reshape(DEC_BATCH, PAST_LEN, D_MODEL),
                      cache_l1_v.reshape(DEC_BATCH, PAST_LEN, D_MODEL)))
    x, xn, eid, gates = _post(1, (o_p, o_s), x, mods, l1_w_out.astype(BF16), l1_norm2_g,
                              l1_router_w, l1_router_b)
    dest, slot_tok, blk_e, nvalid = _route(eid)
    y = _experts(slot_tok, blk_e, nvalid, xn, l1_up_w, l1_up_b, l1_down_w, l1_down_b)
    x = _combine(dest, x, gates, mods, final_g, y, True)
    state_l1_k = st1[:, :D_MODEL].reshape(BATCH, SEQ, DIFF_HEADS, 2, HEAD_DIM)
    state_l1_v = st1[:, D_MODEL:].reshape(BATCH, SEQ, DIFF_HEADS, DIFF_V)

    y_prompt = x[:NP].reshape(BATCH, SEQ, D_MODEL)
    y_sample = x[NP:].reshape(DEC_BATCH, DEC_SEQ, D_MODEL)
    return (y_prompt, y_sample, state_l0_k, state_l0_v, state_l1_k, state_l1_v)
```

```python
import functools
import math

import jax
import jax.numpy as jnp
from jax import lax
from jax.experimental import pallas as pl
from jax.experimental.pallas import tpu as pltpu

F32 = jnp.float32
BF16 = jnp.bfloat16
I32 = jnp.int32

D_MODEL = 1024
BATCH = 16
SEQ = 256
DEC_BATCH = 8
DEC_SEQ = 2048
PAST_LEN = 256
GRID_W = 64
HEAD_DIM = 64
BLOCK = 128
WINDOW = 128
ROPE_BASE = 10000.0
EPS = 1e-6
NEG = -1e30
ATTN_SCALE = HEAD_DIM ** -0.5
CONV_CH = 512
CONV_W = 31
WIN_HEADS = 8
WIN_KV = 2
WIN_G = 4
IN0_W = 1792
DIFF_HEADS = 8
DIFF_V = 128
IN1_W = 3072
N_EXPERTS = 32
TOP_K = 4
D_FF = 1024
SWIGLU_LIMIT = 7.0
SWIGLU_ALPHA = 1.702

NP = BATCH * SEQ
NS = DEC_BATCH * DEC_SEQ
N = NP + NS
N_ASG = N * TOP_K

LANES = 128
SUBLANES = 8
ROW_TILES = D_MODEL // LANES
VMEM_LIMIT = 56 * 1024 * 1024

TM = 512
TMC = 256
TME = 256
CB = 256
HALO = 16
TQ = 256
RT = 512
GATHER_UNROLL = 16
TOP_K_SHIFT = 2
N_EBLOCKS = (N_ASG + N_EXPERTS * (TME - 1)) // TME + 1
CAP = N_EBLOCKS * TME

_NT = (((1,), (1,)), ((), ()))


def _cparams(sem):
    return pltpu.CompilerParams(dimension_semantics=sem, vmem_limit_bytes=VMEM_LIMIT)


def _mod_row(i, tm):
    npb = NP // tm
    return jnp.where(i < npb, DEC_BATCH, (i - npb) // (DEC_SEQ // tm))


def _mod_spec(j, tm):
    return pl.BlockSpec((1, 1, D_MODEL), lambda i, *_: (_mod_row(i, tm) * 6 + j, 0, 0))


def _const_spec(shape):
    nd = len(shape)
    return pl.BlockSpec(shape, lambda *_: (0,) * nd)


def _ada_norm(x, g, shift, scale):
    ms = jnp.mean(x * x, axis=-1, keepdims=True)
    return (x * lax.rsqrt(ms + EPS) * g) * (1.0 + scale) + shift


def _mod_kernel(c_ref, w_ref, b_ref, o_ref):
    c = c_ref[...]
    s = c * jax.nn.sigmoid(c)
    o_ref[...] = jnp.dot(s.astype(BF16), w_ref[...].astype(BF16), preferred_element_type=F32) + b_ref[...]


def _modulation(cond16, w, b):
    m = pl.pallas_call(
        _mod_kernel,
        grid=(6,),
        in_specs=[_const_spec((16, D_MODEL)),
                  pl.BlockSpec((D_MODEL, D_MODEL), lambda j: (0, j)),
                  pl.BlockSpec((1, D_MODEL), lambda j: (0, j))],
        out_specs=pl.BlockSpec((16, D_MODEL), lambda j: (0, j)),
        out_shape=jax.ShapeDtypeStruct((16, 6 * D_MODEL), F32),
        compiler_params=_cparams(("arbitrary",)),
        name="modulation",
    )(cond16, w, b.reshape(1, -1))
    return m.reshape(16 * 6, 1, D_MODEL)


def _rope(v, cos, sa, sb):
    return v * cos + pltpu.roll(v, LANES - 16, 1) * sa + pltpu.roll(v, 16, 1) * sb


def _inproj_kernel(x_ref, sh_ref, sc_ref, g_ref, w_ref, cos_ref, sa_ref, sb_ref, *outs, layer):
    i = pl.program_id(0)
    h = _ada_norm(x_ref[...], g_ref[...], sh_ref[0], sc_ref[0])
    acc = jnp.dot(h.astype(BF16), w_ref[...], preferred_element_type=F32)
    if layer == 0:
        ha_ref, hq_ref, st_ref = outs
        ha_ref[...] = acc[:, :2 * CONV_CH]
        base, n_q, n_rope, n_all, st_lo = 2 * CONV_CH, 4, 5, 6, 2 * CONV_CH + 512
    else:
        hq_ref, st_ref = outs
        base, n_q, n_rope, n_all, st_lo = 0, 8, 16, 24, 1024
    is_prompt = i < NP // TM

    def chunk(c):
        v = acc[:, base + c * LANES: base + (c + 1) * LANES]
        return v * ATTN_SCALE if c < n_q else v

    @pl.when(is_prompt)
    def _():
        for c in range(n_all):
            hq_ref[:, c * LANES:(c + 1) * LANES] = chunk(c).astype(BF16)
        st_ref[...] = acc[:, st_lo:]

    @pl.when(jnp.logical_not(is_prompt))
    def _():
        cos, sa, sb = cos_ref[...], sa_ref[...], sb_ref[...]
        for c in range(n_all):
            v = chunk(c)
            if c < n_rope:
                v = _rope(v, cos, sa, sb)
            hq_ref[:, c * LANES:(c + 1) * LANES] = v.astype(BF16)


def _inproj(x, mods, g, w_bf, rope_tabs, layer):
    npb = NP // TM
    spb = DEC_SEQ // TM
    nout = w_bf.shape[1]
    rope_spec = pl.BlockSpec((TM, LANES), lambda i: (jnp.where(i < npb, 0, (i - npb) % spb), 0))
    st_w = 256 if layer == 0 else 2048
    st_spec = pl.BlockSpec((TM, st_w), lambda i: (jnp.minimum(i, npb - 1), 0))
    if layer == 0:
        out_shape = (jax.ShapeDtypeStruct((N, 2 * CONV_CH), F32),
                     jax.ShapeDtypeStruct((N, 768), BF16),
                     jax.ShapeDtypeStruct((NP, st_w), F32))
        out_specs = (pl.BlockSpec((TM, 2 * CONV_CH), lambda i: (i, 0)),
                     pl.BlockSpec((TM, 768), lambda i: (i, 0)), st_spec)
    else:
        out_shape = (jax.ShapeDtypeStruct((N, IN1_W), BF16),
                     jax.ShapeDtypeStruct((NP, st_w), F32))
        out_specs = (pl.BlockSpec((TM, IN1_W), lambda i: (i, 0)), st_spec)
    return pl.pallas_call(
        functools.partial(_inproj_kernel, layer=layer),
        grid=(N // TM,),
        in_specs=[pl.BlockSpec((TM, D_MODEL), lambda i: (i, 0)),
                  _mod_spec(0, TM), _mod_spec(1, TM),
                  _const_spec((1, D_MODEL)),
                  _const_spec((D_MODEL, nout)),
                  rope_spec, rope_spec, rope_spec],
        out_specs=out_specs,
        out_shape=out_shape,
        compiler_params=_cparams(("arbitrary",)),
        name="inproj_l%d" % layer,
    )(x, mods, mods, g.reshape(1, -1), w_bf, *rope_tabs)


def _conv_kernel(prev_ref, cur_ref, next_ref, w_ref, b_ref, g_ref, bb_ref, o_ref, upad):
    i = pl.program_id(0)
    npb = NP // CB
    spb = DEC_SEQ // CB
    j = (i - npb) % spb
    first = jnp.logical_or(i < npb, j == 0)
    last = jnp.logical_or(i < npb, j == spb - 1)

    def glu(r):
        return r[:, :CONV_CH] * jax.nn.sigmoid(r[:, CONV_CH:])

    upad[HALO:HALO + CB, :] = glu(cur_ref[...])
    upad[0:HALO, :] = jnp.where(first, 0.0, glu(prev_ref[...]))
    upad[HALO + CB:, :] = jnp.where(last, 0.0, glu(next_ref[...]))
    acc = jnp.zeros((CB, CONV_CH), F32)
    off = HALO - CONV_W // 2
    for t in range(CONV_W):
        acc = acc + w_ref[t:t + 1, :] * upad[off + t:off + t + CB, :]
    u = acc + b_ref[...]
    mu = jnp.mean(u, axis=-1, keepdims=True)
    var = jnp.mean(jnp.square(u - mu), axis=-1, keepdims=True)
    y = (u - mu) * lax.rsqrt(var + EPS) * g_ref[...] + bb_ref[...]
    o_ref[...] = (y * jax.nn.sigmoid(y)).astype(BF16)


def _conv(ha, conv_w, conv_b, cg, cb):
    hb = CB // HALO
    nh = N // HALO
    w_pad = jnp.concatenate([conv_w, jnp.zeros((1, CONV_CH), F32)], axis=0)
    return pl.pallas_call(
        _conv_kernel,
        grid=(N // CB,),
        in_specs=[pl.BlockSpec((HALO, 2 * CONV_CH), lambda i: (jnp.maximum(i * hb - 1, 0), 0)),
                  pl.BlockSpec((CB, 2 * CONV_CH), lambda i: (i, 0)),
                  pl.BlockSpec((HALO, 2 * CONV_CH), lambda i: (jnp.minimum((i + 1) * hb, nh - 1), 0)),
                  _const_spec((CONV_W + 1, CONV_CH)),
                  _const_spec((1, CONV_CH)), _const_spec((1, CONV_CH)), _const_spec((1, CONV_CH))],
        out_specs=pl.BlockSpec((CB, CONV_CH), lambda i: (i, 0)),
        out_shape=jax.ShapeDtypeStruct((N, CONV_CH), BF16),
        scratch_shapes=[pltpu.VMEM((CB + 2 * HALO, CONV_CH), F32)],
        compiler_params=_cparams(("arbitrary",)),
        name="conformer_conv",
    )(ha, ha, ha, w_pad, conv_b.reshape(1, -1), cg.reshape(1, -1), cb.reshape(1, -1))


def _sink_attend(q, k, v, sink, mask):
    s = lax.dot_general(q, k, _NT, preferred_element_type=F32)
    if mask is not None:
        s = jnp.where(mask, s, NEG)
    m = jnp.maximum(jnp.max(s, axis=-1, keepdims=True), sink)
    p = jnp.exp(s - m)
    den = jnp.sum(p, axis=-1, keepdims=True) + jnp.exp(sink - m)
    return jnp.dot(p.astype(BF16), v, preferred_element_type=F32) / den


def _l0_prompt_attn_kernel(sink_ref, q_ref, k_ref, v_ref, o_ref):
    k = k_ref[...]
    v = v_ref[...]
    outs = []
    for h in range(WIN_HEADS):
        j = h // WIN_G
        outs.append(_sink_attend(q_ref[:, h * HEAD_DIM:(h + 1) * HEAD_DIM],
                                 k[:, j * HEAD_DIM:(j + 1) * HEAD_DIM],
                                 v[:, j * HEAD_DIM:(j + 1) * HEAD_DIM], sink_ref[h], None))
    o_ref[...] = jnp.concatenate(outs, axis=1).astype(BF16)


def _l0_prompt_attn(hq, sink):
    return pl.pallas_call(
        _l0_prompt_attn_kernel,
        grid=(BATCH,),
        in_specs=[pl.BlockSpec(memory_space=pltpu.SMEM),
                  pl.BlockSpec((SEQ, 512), lambda b: (b, 0)),
                  pl.BlockSpec((SEQ, LANES), lambda b: (b, 4)),
                  pl.BlockSpec((SEQ, LANES), lambda b: (b, 5))],
        out_specs=pl.BlockSpec((SEQ, 512), lambda b: (b, 0)),
        out_shape=jax.ShapeDtypeStruct((NP, 512), BF16),
        compiler_params=_cparams(("arbitrary",)),
        name="l0_prompt_attn",
    )(sink, hq, hq, hq)


def _l0_window_attn_kernel(sink_ref, q_ref, kp_ref, kc_ref, kn_ref, vp_ref, vc_ref, vn_ref,
                           ck_ref, cv_ref, o_ref):
    n = pl.program_id(1)
    k = jnp.concatenate([kp_ref[...], kc_ref[...], kn_ref[...], ck_ref[...].astype(BF16)], axis=0)
    v = jnp.concatenate([vp_ref[...], vc_ref[...], vn_ref[...], cv_ref[...].astype(BF16)], axis=0)
    qpos = n * BLOCK + lax.broadcasted_iota(I32, (BLOCK, 3 * BLOCK + PAST_LEN), 0)
    col = lax.broadcasted_iota(I32, (BLOCK, 3 * BLOCK + PAST_LEN), 1)
    kpos = (n - 1) * BLOCK + col
    local_ok = (jnp.abs(kpos - qpos) <= WINDOW) & (kpos >= 0) & (kpos < DEC_SEQ)
    mask = jnp.logical_or(col >= 3 * BLOCK, local_ok)
    outs = []
    for h in range(WIN_HEADS):
        j = h // WIN_G
        outs.append(_sink_attend(q_ref[:, h * HEAD_DIM:(h + 1) * HEAD_DIM],
                                 k[:, j * HEAD_DIM:(j + 1) * HEAD_DIM],
                                 v[:, j * HEAD_DIM:(j + 1) * HEAD_DIM], sink_ref[h], mask))
    o_ref[...] = jnp.concatenate(outs, axis=1).astype(BF16)


def _l0_window_attn(hq, sink, ck, cv):
    nb = DEC_SEQ // BLOCK
    r0 = NP // BLOCK

    def kv_spec(col, d):
        return pl.BlockSpec((BLOCK, LANES),
                            lambda b, n: (r0 + b * nb + jnp.clip(n + d, 0, nb - 1), col))

    ctx_spec = pl.BlockSpec((None, PAST_LEN, LANES), lambda b, n: (b, 0, 0))
    return pl.pallas_call(
        _l0_window_attn_kernel,
        grid=(DEC_BATCH, nb),
        in_specs=[pl.BlockSpec(memory_space=pltpu.SMEM),
                  pl.BlockSpec((BLOCK, 512), lambda b, n: (r0 + b * nb + n, 0)),
                  kv_spec(4, -1), kv_spec(4, 0), kv_spec(4, 1),
                  kv_spec(5, -1), kv_spec(5, 0), kv_spec(5, 1),
                  ctx_spec, ctx_spec],
        out_specs=pl.BlockSpec((BLOCK, 512), lambda b, n: (b * nb + n, 0)),
        out_shape=jax.ShapeDtypeStruct((NS, 512), BF16),
        compiler_params=_cparams(("arbitrary", "arbitrary")),
        name="l0_window_attn",
    )(sink, hq, hq, hq, hq, hq, hq, hq, ck, cv)


def _diff_attn_kernel(*refs, lam_init, has_ctx):
    if has_ctx:
        q_ref, k_ref, v_ref, ck_ref, cv_ref, lam_ref, g_ref, o_ref = refs
    else:
        q_ref, k_ref, v_ref, lam_ref, g_ref, o_ref = refs
    lv = lam_ref[...]
    lam = (jnp.exp(jnp.sum(lv[0:1] * lv[1:2], axis=-1, keepdims=True))
           - jnp.exp(jnp.sum(lv[2:3] * lv[3:4], axis=-1, keepdims=True)) + lam_init)
    g = g_ref[...]
    outs = []
    for h in range(DIFF_HEADS):
        probs = []
        for c in range(2):
            lo = h * DIFF_V + c * HEAD_DIM
            q = q_ref[:, lo:lo + HEAD_DIM]
            s = lax.dot_general(q, k_ref[:, lo:lo + HEAD_DIM], _NT, preferred_element_type=F32)
            m = jnp.max(s, axis=-1, keepdims=True)
            if has_ctx:
                sc = lax.dot_general(q, ck_ref[:, lo:lo + HEAD_DIM].astype(BF16), _NT,
                                     preferred_element_type=F32)
                m = jnp.maximum(m, jnp.max(sc, axis=-1, keepdims=True))
                pc = jnp.exp(sc - m)
            p = jnp.exp(s - m)
            den = jnp.sum(p, axis=-1, keepdims=True)
            if has_ctx:
                den = den + jnp.sum(pc, axis=-1, keepdims=True)
                probs.append((p, pc, den))
            else:
                probs.append((p, None, den))
        (p0, pc0, d0), (p1, pc1, d1) = probs
        r0 = 1.0 / d0
        r1 = lam / d1
        a = (p0 * r0 - p1 * r1).astype(BF16)
        o = jnp.dot(a, v_ref[:, h * DIFF_V:(h + 1) * DIFF_V], preferred_element_type=F32)
        if has_ctx:
            ac = (pc0 * r0 - pc1 * r1).astype(BF16)
            o = o + jnp.dot(ac, cv_ref[:, h * DIFF_V:(h + 1) * DIFF_V].astype(BF16),
                            preferred_element_type=F32)
        ms = jnp.mean(o * o, axis=-1, keepdims=True)
        outs.append(((o * lax.rsqrt(ms + EPS)) * g) * (1.0 - lam_init))
    o_ref[...] = jnp.concatenate(outs, axis=1).astype(BF16)


def _diff_attn(h1, lam_vecs, subln_g, lam_init, ctx):
    g = subln_g.reshape(1, DIFF_V)
    small = [_const_spec((4, HEAD_DIM)), _const_spec((1, DIFF_V))]
    if ctx is None:
        grid = (BATCH, 1)
        in_specs = [pl.BlockSpec((SEQ, D_MODEL), lambda b, i: (b, 0)),
                    pl.BlockSpec((SEQ, D_MODEL), lambda b, i: (b, 1)),
                    pl.BlockSpec((SEQ, D_MODEL), lambda b, i: (b, 2))] + small
        out_specs = pl.BlockSpec((SEQ, D_MODEL), lambda b, i: (b, 0))
        rows = NP
        args = (h1, h1, h1, lam_vecs, g)
        name = "diff_attn_prompt"
    else:
        nq = DEC_SEQ // TQ
        q0 = NP // TQ
        s0 = NP // DEC_SEQ
        ctx_spec = pl.BlockSpec((None, PAST_LEN, D_MODEL), lambda b, i: (b, 0, 0))
        grid = (DEC_BATCH, nq)
        in_specs = [pl.BlockSpec((TQ, D_MODEL), lambda b, i: (q0 + b * nq + i, 0)),
                    pl.BlockSpec((DEC_SEQ, D_MODEL), lambda b, i: (s0 + b, 1)),
                    pl.BlockSpec((DEC_SEQ, D_MODEL), lambda b, i: (s0 + b, 2)),
                    ctx_spec, ctx_spec] + small
        out_specs = pl.BlockSpec((TQ, D_MODEL), lambda b, i: (b * nq + i, 0))
        rows = NS
        args = (h1, h1, h1, ctx[0], ctx[1], lam_vecs, g)
        name = "diff_attn_sample"
    return pl.pallas_call(
        functools.partial(_diff_attn_kernel, lam_init=lam_init, has_ctx=ctx is not None),
        grid=grid, in_specs=in_specs, out_specs=out_specs,
        out_shape=jax.ShapeDtypeStruct((rows, D_MODEL), BF16),
        compiler_params=_cparams(("arbitrary", "arbitrary")),
        name=name,
    )(*args)


def _store_row_tiles(ref, val, rows):
    for s in range(ROW_TILES):
        ref[pl.ds(s, rows, stride=ROW_TILES), :] = val[:, s * LANES:(s + 1) * LANES]


def _post_kernel(*refs, layer):
    if layer == 0:
        (a_ref, op_ref, os_ref, x_ref, g1_ref, sh2_ref, sc2_ref, w_ref, n2g_ref, rw_ref, rb_ref,
         xo_ref, xn_ref, eid_ref, gate_ref, cnt_ref) = refs
    else:
        (op_ref, os_ref, x_ref, g1_ref, sh2_ref, sc2_ref, w_ref, n2g_ref, rw_ref, rb_ref,
         xo_ref, xn_ref, eid_ref, gate_ref, cnt_ref) = refs
    i = pl.program_id(0)
    o = jnp.where(i < NP // TM, op_ref[...], os_ref[...])
    if layer == 0:
        mix = (jnp.dot(a_ref[...], w_ref[:CONV_CH, :], preferred_element_type=F32)
               + jnp.dot(o, w_ref[CONV_CH:, :], preferred_element_type=F32))
    else:
        mix = jnp.dot(o, w_ref[...], preferred_element_type=F32)
    x1 = x_ref[...] + g1_ref[0] * mix
    xo_ref[...] = x1
    xn = _ada_norm(x1, n2g_ref[...], sh2_ref[0], sc2_ref[0])
    _store_row_tiles(xn_ref, xn, TM)
    logits = jnp.dot(xn, rw_ref[...], preferred_element_type=F32,
                     precision=lax.Precision.HIGHEST) + rb_ref[...]
    lane = lax.broadcasted_iota(I32, logits.shape, 1)
    vals, idxs = [], []
    picked = jnp.zeros(logits.shape, F32)
    for _ in range(TOP_K):
        m = jnp.max(logits, axis=-1, keepdims=True)
        idx = jnp.min(jnp.where(logits == m, lane, N_EXPERTS), axis=-1, keepdims=True)
        vals.append(m)
        idxs.append(idx)
        hit = lane == idx
        picked = picked + hit.astype(F32)
        logits = jnp.where(hit, -jnp.inf, logits)
    es = [jnp.exp(v - vals[0]) for v in vals]
    den = es[0] + es[1] + es[2] + es[3]
    eid_ref[...] = jnp.concatenate(idxs, axis=1)
    gate_ref[...] = jnp.concatenate([e / den for e in es], axis=1)

    @pl.when(i == 0)
    def _():
        cnt_ref[...] = jnp.zeros_like(cnt_ref)

    cnt_ref[...] += jnp.sum(picked, axis=0, keepdims=True)


def _post(layer, mix_parts, x, mods, w_bf, n2g, rw, rb):
    npb = NP // TM
    if layer == 0:
        a, o_p, o_s = mix_parts
        wo = 512
        mix_specs = [pl.BlockSpec((TM, CONV_CH), lambda i: (i, 0))]
        mix_args = [a, o_p, o_s]
    else:
        o_p, o_s = mix_parts
        wo = D_MODEL
        mix_specs = []
        mix_args = [o_p, o_s]
    mix_specs += [pl.BlockSpec((TM, wo), lambda i: (jnp.minimum(i, npb - 1), 0)),
                  pl.BlockSpec((TM, wo), lambda i: (jnp.maximum(i - npb, 0), 0))]
    return pl.pallas_call(
        functools.partial(_post_kernel, layer=layer),
        grid=(N // TM,),
        in_specs=mix_specs + [pl.BlockSpec((TM, D_MODEL), lambda i: (i, 0)),
                              _mod_spec(2, TM), _mod_spec(3, TM), _mod_spec(4, TM),
                              _const_spec((D_MODEL, D_MODEL)), _const_spec((1, D_MODEL)),
                              _const_spec((D_MODEL, N_EXPERTS)), _const_spec((1, N_EXPERTS))],
        out_specs=(pl.BlockSpec((TM, D_MODEL), lambda i: (i, 0)),
                   pl.BlockSpec((TM * ROW_TILES, LANES), lambda i: (i, 0)),
                   pl.BlockSpec((TM, TOP_K), lambda i: (i, 0)),
                   pl.BlockSpec((TM, TOP_K), lambda i: (i, 0)),
                   _const_spec((1, N_EXPERTS))),
        out_shape=(jax.ShapeDtypeStruct((N, D_MODEL), F32),
                   jax.ShapeDtypeStruct((N * ROW_TILES, LANES), F32),
                   jax.ShapeDtypeStruct((N, TOP_K), I32),
                   jax.ShapeDtypeStruct((N, TOP_K), F32),
                   jax.ShapeDtypeStruct((1, N_EXPERTS), F32)),
        compiler_params=_cparams(("arbitrary",)),
        name="post_l%d" % layer,
    )(*mix_args, x, mods, mods, mods, w_bf, n2g.reshape(1, -1), rw, rb.reshape(1, -1))


def _slots_kernel(cnt_ref, bst_ref, eidt_ref, base_ref, triu_ref, destt_ref, slot_ref, carry, dsm, sem):
    j = pl.program_id(0)

    @pl.when(j == 0)
    def _():
        carry[...] = jnp.zeros_like(carry)

    e_iota = lax.broadcasted_iota(I32, (N_EXPERTS, RT), 0)
    ohs = [(eidt_ref[k:k + 1, :] == e_iota).astype(F32) for k in range(TOP_K)]
    ohsum = ohs[0] + ohs[1] + ohs[2] + ohs[3]
    cum = jnp.dot(ohsum.astype(BF16), triu_ref[...], preferred_element_type=F32)
    tot = cum + carry[...] + base_ref[...]
    destt_ref[...] = jnp.concatenate(
        [jnp.sum(oh * tot, axis=0, keepdims=True) for oh in ohs], axis=0).astype(I32)
    carry[...] += jnp.sum(ohsum, axis=1, keepdims=True)

    cp = pltpu.make_async_copy(destt_ref, dsm, sem)
    cp.start()
    cp.wait()

    def scatter(c, carry_):
        for u in range(SUBLANES):
            t = c * SUBLANES + u
            a0 = (j * RT + t) * TOP_K
            for k in range(TOP_K):
                slot_ref[dsm[k, t]] = a0 + k
        return carry_

    lax.fori_loop(0, RT // SUBLANES, scatter, 0)

    @pl.when(j == pl.num_programs(0) - 1)
    def _():
        def zero(s, c):
            slot_ref[s] = 0
            return c

        def per_expert(e, c):
            cnt = cnt_ref[e]
            b0 = bst_ref[e]
            lax.fori_loop(b0 * TME + cnt, (b0 + (cnt + TME - 1) // TME) * TME, zero, 0)
            return c

        lax.fori_loop(0, N_EXPERTS, per_expert, 0)
        lax.fori_loop(bst_ref[N_EXPERTS] * TME, CAP, zero, 0)


def _route(eid, counts_f):
    counts = counts_f.reshape(-1).astype(I32)
    nblk = (counts + TME - 1) // TME
    bend = jnp.cumsum(nblk)
    bstart = bend - nblk
    n_used = bend[-1]
    blk = jnp.arange(N_EBLOCKS, dtype=I32)
    blk_e = jnp.minimum(jnp.sum((bend[None, :] <= blk[:, None]).astype(I32), axis=1), N_EXPERTS - 1)
    last_e = jnp.max(jnp.where(counts > 0, jnp.arange(N_EXPERTS, dtype=I32), 0))
    blk_e = jnp.where(blk < n_used, blk_e, last_e).astype(I32)
    nvalid = jnp.clip(counts[blk_e] - (blk - bstart[blk_e]) * TME, 0, TME)
    nvalid = jnp.where(blk < n_used, nvalid, 0).astype(I32)
    bst = jnp.concatenate([bstart, n_used[None]]).astype(I32)
    base = (bstart * TME).astype(F32).reshape(N_EXPERTS, 1)
    ar = jnp.arange(RT, dtype=I32)
    triu = (ar[:, None] < ar[None, :]).astype(BF16)
    grid_spec = pltpu.PrefetchScalarGridSpec(
        num_scalar_prefetch=2,
        grid=(N // RT,),
        in_specs=[pl.BlockSpec((TOP_K, RT), lambda j, c, b: (0, j)),
                  _const_spec((N_EXPERTS, 1)),
                  _const_spec((RT, RT))],
        out_specs=(pl.BlockSpec((TOP_K, RT), lambda j, c, b: (0, j)),
                   pl.BlockSpec(memory_space=pltpu.SMEM)),
        scratch_shapes=[pltpu.VMEM((N_EXPERTS, 1), F32),
                        pltpu.SMEM((TOP_K, RT), I32),
                        pltpu.SemaphoreType.DMA],
    )
    destt, slot_asg = pl.pallas_call(
        _slots_kernel,
        grid_spec=grid_spec,
        out_shape=(jax.ShapeDtypeStruct((TOP_K, N), I32),
                   jax.ShapeDtypeStruct((CAP,), I32)),
        compiler_params=_cparams(("arbitrary",)),
        name="slots",
    )(counts, bst, eid.T, base, triu)
    return destt.reshape(-1), slot_asg, blk_e, nvalid


def _expert_kernel(slot_ref, blke_ref, nval_ref, xn_hbm, upw_ref, upb_ref, dww_ref, dwb_ref,
                   y_ref, gbuf, upbf, dwbf, sem):
    i = pl.program_id(0)
    nv = nval_ref[i]
    cur = i % 2

    def gather(blk, buf):
        def issue(c, carry):
            for u in range(GATHER_UNROLL):
                r = c * GATHER_UNROLL + u
                tok = lax.shift_right_logical(slot_ref[blk * TME + r], TOP_K_SHIFT)
                pltpu.make_async_copy(
                    xn_hbm.at[pl.ds(pl.multiple_of(tok * ROW_TILES, ROW_TILES), ROW_TILES), :],
                    gbuf.at[buf, pl.ds(pl.multiple_of(r * ROW_TILES, ROW_TILES), ROW_TILES), :],
                    sem.at[buf]).start()
            return carry

        lax.fori_loop(0, TME // GATHER_UNROLL, issue, 0)

    @pl.when(i == 0)
    def _():
        gather(0, 0)

    @pl.when(nv > 0)
    def _():
        gather(i + 1, 1 - cur)

    @pl.when(jnp.logical_or(i == 0, nval_ref[jnp.maximum(i - 1, 0)] > 0))
    def _():
        pltpu.make_async_copy(xn_hbm.at[pl.ds(0, TME * ROW_TILES), :], gbuf.at[cur], sem.at[cur]).wait()

    @pl.when(nv > 0)
    def _():
        e = blke_ref[i]
        prev = blke_ref[jnp.maximum(i - 1, 0)]

        @pl.when(jnp.logical_or(i == 0, e != prev))
        def _():
            upbf[...] = upw_ref[0].astype(BF16)
            dwbf[...] = dww_ref[0].astype(BF16)

        xb = jnp.concatenate(
            [gbuf[cur, pl.ds(s, TME, stride=ROW_TILES), :] for s in range(ROW_TILES)], axis=1).astype(BF16)
        gu = jnp.dot(xb, upbf[...], preferred_element_type=F32) + upb_ref[0]
        g = jnp.minimum(gu[:, :D_FF], SWIGLU_LIMIT)
        lin = jnp.clip(gu[:, D_FF:], -SWIGLU_LIMIT, SWIGLU_LIMIT)
        hdn = g * jax.nn.sigmoid(SWIGLU_ALPHA * g) * (lin + 1.0)
        y = jnp.dot(hdn.astype(BF16), dwbf[...], preferred_element_type=F32) + dwb_ref[0]
        _store_row_tiles(y_ref, y, TME)

    @pl.when(nv == 0)
    def _():
        y_ref[...] = jnp.zeros_like(y_ref)


def _experts(slot_tok, blk_e, nvalid, xn_tiles, up_w, up_b, down_w, down_b):
    grid_spec = pltpu.PrefetchScalarGridSpec(
        num_scalar_prefetch=3,
        grid=(N_EBLOCKS,),
        in_specs=[pl.BlockSpec(memory_space=pl.ANY),
                  pl.BlockSpec((1, D_MODEL, 2 * D_FF), lambda i, s, be, nv: (be[i], 0, 0)),
                  pl.BlockSpec((1, 1, 2 * D_FF), lambda i, s, be, nv: (be[i], 0, 0)),
                  pl.BlockSpec((1, D_FF, D_MODEL), lambda i, s, be, nv: (be[i], 0, 0)),
                  pl.BlockSpec((1, 1, D_MODEL), lambda i, s, be, nv: (be[i], 0, 0))],
        out_specs=pl.BlockSpec((TME * ROW_TILES, LANES), lambda i, s, be, nv: (i, 0)),
        scratch_shapes=[pltpu.VMEM((2, TME * ROW_TILES, LANES), F32),
                        pltpu.VMEM((D_MODEL, 2 * D_FF), BF16),
                        pltpu.VMEM((D_FF, D_MODEL), BF16),
                        pltpu.SemaphoreType.DMA((2,))],
    )
    return pl.pallas_call(
        _expert_kernel,
        grid_spec=grid_spec,
        out_shape=jax.ShapeDtypeStruct((CAP * ROW_TILES, LANES), F32),
        compiler_params=_cparams(("arbitrary",)),
        name="experts",
    )(slot_tok, blk_e, nvalid, xn_tiles, up_w, up_b.reshape(N_EXPERTS, 1, -1),
      down_w, down_b.reshape(N_EXPERTS, 1, -1))


def _combine_kernel(dest_ref, x_ref, gate_ref, g2_ref, fg_ref, y_hbm, o_ref, buf, sem, *, final):
    i = pl.program_id(0)
    n_rows = TMC * TOP_K
    cur = i % 2

    def gather(blk, b):
        def issue(c, carry):
            for u in range(SUBLANES):
                t = c * SUBLANES + u
                for k in range(TOP_K):
                    d = dest_ref[k * N + blk * TMC + t]
                    pltpu.make_async_copy(
                        y_hbm.at[pl.ds(pl.multiple_of(d * ROW_TILES, ROW_TILES), ROW_TILES), :],
                        buf.at[b, pl.ds(pl.multiple_of((k * TMC + t) * ROW_TILES, ROW_TILES), ROW_TILES), :],
                        sem.at[b]).start()
            return carry

        lax.fori_loop(0, TMC // SUBLANES, issue, 0)

    @pl.when(i == 0)
    def _():
        gather(0, 0)

    @pl.when(i + 1 < pl.num_programs(0))
    def _():
        gather(i + 1, 1 - cur)

    pltpu.make_async_copy(y_hbm.at[pl.ds(0, n_rows * ROW_TILES), :], buf.at[cur], sem.at[cur]).wait()
    gate = gate_ref[...]
    cols = []
    for s in range(ROW_TILES):
        acc = None
        for k in range(TOP_K):
            term = gate[:, k:k + 1] * buf[cur, pl.ds(k * TMC * ROW_TILES + s, TMC, stride=ROW_TILES), :]
            acc = term if acc is None else acc + term
        cols.append(acc)
    x2 = x_ref[...] + g2_ref[0] * jnp.concatenate(cols, axis=1)
    if final:
        ms = jnp.mean(x2 * x2, axis=-1, keepdims=True)
        x2 = x2 * lax.rsqrt(ms + EPS) * fg_ref[...]
    o_ref[...] = x2


def _combine(dest, x, gates, mods, final_g, y_tiles, final):
    grid_spec = pltpu.PrefetchScalarGridSpec(
        num_scalar_prefetch=1,
        grid=(N // TMC,),
        in_specs=[pl.BlockSpec((TMC, D_MODEL), lambda i, d: (i, 0)),
                  pl.BlockSpec((TMC, TOP_K), lambda i, d: (i, 0)),
                  _mod_spec(5, TMC),
                  _const_spec((1, D_MODEL)),
                  pl.BlockSpec(memory_space=pl.ANY)],
        out_specs=pl.BlockSpec((TMC, D_MODEL), lambda i, d: (i, 0)),
        scratch_shapes=[pltpu.VMEM((2, TMC * TOP_K * ROW_TILES, LANES), F32),
                        pltpu.SemaphoreType.DMA((2,))],
    )
    return pl.pallas_call(
        functools.partial(_combine_kernel, final=final),
        grid_spec=grid_spec,
        out_shape=jax.ShapeDtypeStruct((N, D_MODEL), F32),
        compiler_params=_cparams(("arbitrary",)),
        name="combine",
    )(dest, x, gates, mods, final_g.reshape(1, -1), y_tiles)


def _rope_tables():
    pos = jnp.arange(DEC_SEQ)
    r = (pos // GRID_W).astype(F32)
    col = (pos % GRID_W).astype(F32)
    quarter = HEAD_DIM // 4
    inv = ROPE_BASE ** (-jnp.arange(quarter, dtype=F32) / quarter)
    ar = r[:, None] * inv
    ac = col[:, None] * inv
    ang = jnp.concatenate([ar, ar, ac, ac], axis=-1)
    cos = jnp.tile(jnp.cos(ang), (1, LANES // HEAD_DIM))
    sin = jnp.tile(jnp.sin(ang), (1, LANES // HEAD_DIM))
    first = (jnp.arange(LANES) % 32) < 16
    sa = jnp.where(first[None, :], -sin, 0.0)
    sb = jnp.where(first[None, :], 0.0, sin)
    return cos, sa, sb


@jax.jit
def kernel(x_prompt, x_sample, cache_l0_k, cache_l0_v, cache_l1_k, cache_l1_v, c, c_ctx, final_g,
           l0_mod_w, l0_mod_b, l0_norm1_g, l0_w_in, l0_conv_w, l0_conv_b, l0_cnorm_g, l0_cnorm_b,
           l0_sink, l0_w_out, l0_norm2_g, l0_router_w, l0_router_b, l0_up_w, l0_up_b, l0_down_w,
           l0_down_b,
           l1_mod_w, l1_mod_b, l1_norm1_g, l1_w_in, l1_lam_q1, l1_lam_k1, l1_lam_q2, l1_lam_k2,
           l1_subln_g, l1_w_out, l1_norm2_g, l1_router_w, l1_router_b, l1_up_w, l1_up_b, l1_down_w,
           l1_down_b):
    x = jnp.concatenate([x_prompt.reshape(NP, D_MODEL), x_sample.reshape(NS, D_MODEL)], axis=0)
    cond16 = jnp.concatenate([c, c_ctx[None, :], jnp.zeros((16 - DEC_BATCH - 1, D_MODEL), F32)], axis=0)
    rope_tabs = _rope_tables()

    mods = _modulation(cond16, l0_mod_w, l0_mod_b)
    ha, hq, st0 = _inproj(x, mods, l0_norm1_g, l0_w_in.astype(BF16), rope_tabs, 0)
    a = _conv(ha, l0_conv_w, l0_conv_b, l0_cnorm_g, l0_cnorm_b)
    o_p = _l0_prompt_attn(hq, l0_sink)
    o_s = _l0_window_attn(hq, l0_sink, cache_l0_k.reshape(DEC_BATCH, PAST_LEN, LANES),
                          cache_l0_v.reshape(DEC_BATCH, PAST_LEN, LANES))
    x, xn, eid, gates, counts = _post(0, (a, o_p, o_s), x, mods, l0_w_out.astype(BF16), l0_norm2_g,
                                      l0_router_w, l0_router_b)
    dest, slot_tok, blk_e, nvalid = _route(eid, counts)
    y = _experts(slot_tok, blk_e, nvalid, xn, l0_up_w, l0_up_b, l0_down_w, l0_down_b)
    x = _combine(dest, x, gates, mods, final_g, y, False)
    state_l0_k = st0[:, :LANES].reshape(BATCH, SEQ, WIN_KV, HEAD_DIM)
    state_l0_v = st0[:, LANES:].reshape(BATCH, SEQ, WIN_KV, HEAD_DIM)

    lam_init = 0.8 - 0.6 * math.exp(-0.3 * 1)
    mods = _modulation(cond16, l1_mod_w, l1_mod_b)
    h1, st1 = _inproj(x, mods, l1_norm1_g, l1_w_in.astype(BF16), rope_tabs, 1)
    lam_vecs = jnp.stack([l1_lam_q1, l1_lam_k1, l1_lam_q2, l1_lam_k2], axis=0)
    o_p = _diff_attn(h1, lam_vecs, l1_subln_g, lam_init, None)
    o_s = _diff_attn(h1, lam_vecs, l1_subln_g, lam_init,
                     (cache_l1_k.reshape(DEC_BATCH, PAST_LEN, D_MODEL),
                      cache_l1_v.reshape(DEC_BATCH, PAST_LEN, D_MODEL)))
    x, xn, eid, gates, counts = _post(1, (o_p, o_s), x, mods, l1_w_out.astype(BF16), l1_norm2_g,
                                      l1_router_w, l1_router_b)
    dest, slot_tok, blk_e, nvalid = _route(eid, counts)
    y = _experts(slot_tok, blk_e, nvalid, xn, l1_up_w, l1_up_b, l1_down_w, l1_down_b)
    x = _combine(dest, x, gates, mods, final_g, y, True)
    state_l1_k = st1[:, :D_MODEL].reshape(BATCH, SEQ, DIFF_HEADS, 2, HEAD_DIM)
    state_l1_v = st1[:, D_MODEL:].reshape(BATCH, SEQ, DIFF_HEADS, DIFF_V)

    y_prompt = x[:NP].reshape(BATCH, SEQ, D_MODEL)
    y_sample = x[NP:].reshape(DEC_BATCH, DEC_SEQ, D_MODEL)
    return (y_prompt, y_sample, state_l0_k, state_l0_v, state_l1_k, state_l1_v)
```

```python
import functools
import math

import jax
import jax.numpy as jnp
from jax import lax
from jax.experimental import pallas as pl
from jax.experimental.pallas import tpu as pltpu

F32 = jnp.float32
BF16 = jnp.bfloat16
I32 = jnp.int32

D_MODEL = 1024
BATCH = 16
SEQ = 256
DEC_BATCH = 8
DEC_SEQ = 2048
PAST_LEN = 256
GRID_W = 64
HEAD_DIM = 64
BLOCK = 128
WINDOW = 128
ROPE_BASE = 10000.0
EPS = 1e-6
NEG = -1e30
ATTN_SCALE = HEAD_DIM ** -0.5
LOG2E = math.log2(math.e)
CONV_CH = 512
CONV_W = 31
WIN_HEADS = 8
WIN_KV = 2
WIN_G = 4
IN0_W = 1792
DIFF_HEADS = 8
DIFF_V = 128
IN1_W = 3072
N_EXPERTS = 32
TOP_K = 4
TOP_K_SHIFT = 2
D_FF = 1024
SWIGLU_LIMIT = 7.0
SWIGLU_ALPHA = 1.702

NP = BATCH * SEQ
NS = DEC_BATCH * DEC_SEQ
N = NP + NS
N_ASG = N * TOP_K

LANES = 128
SUBLANES = 8
ROW_TILES = D_MODEL // LANES
VMEM_LIMIT = 56 * 1024 * 1024

TM = 512
TMC = 512
TME = 256
CB = 256
HALO = 16
TQ = 256
RT = 512
MAX_EBLOCKS = (N_ASG + N_EXPERTS * (TME - 1)) // TME
SLOT_LEAD = 2 * TME
N_SLOTS = (MAX_EBLOCKS + 3) * TME

_NT = (((1,), (1,)), ((), ()))


def _cparams(sem):
    return pltpu.CompilerParams(dimension_semantics=sem, vmem_limit_bytes=VMEM_LIMIT)


def _mod_row(i, tm):
    npb = NP // tm
    return jnp.where(i < npb, DEC_BATCH, (i - npb) // (DEC_SEQ // tm))


def _mod_spec(j, tm):
    return pl.BlockSpec((1, 1, D_MODEL), lambda i, *_: (_mod_row(i, tm) * 6 + j, 0, 0))


def _const_spec(shape):
    nd = len(shape)
    return pl.BlockSpec(shape, lambda *_: (0,) * nd)


def _ada_norm(x, g, shift, scale):
    ms = jnp.mean(x * x, axis=-1, keepdims=True)
    return (x * lax.rsqrt(ms + EPS) * g) * (1.0 + scale) + shift


def _mod_kernel(c_ref, w_ref, b_ref, o_ref):
    c = c_ref[...]
    s = c * jax.nn.sigmoid(c)
    o_ref[...] = jnp.dot(s.astype(BF16), w_ref[...].astype(BF16), preferred_element_type=F32) + b_ref[...]


def _modulation(cond16, w, b):
    m = pl.pallas_call(
        _mod_kernel,
        grid=(6,),
        in_specs=[_const_spec((16, D_MODEL)),
                  pl.BlockSpec((D_MODEL, D_MODEL), lambda j: (0, j)),
                  pl.BlockSpec((1, D_MODEL), lambda j: (0, j))],
        out_specs=pl.BlockSpec((16, D_MODEL), lambda j: (0, j)),
        out_shape=jax.ShapeDtypeStruct((16, 6 * D_MODEL), F32),
        compiler_params=_cparams(("arbitrary",)),
        name="modulation",
    )(cond16, w, b.reshape(1, -1))
    return m.reshape(16 * 6, 1, D_MODEL)


def _rope(v, cos, sa, sb):
    return v * cos + pltpu.roll(v, LANES - 16, 1) * sa + pltpu.roll(v, 16, 1) * sb


def _inproj_kernel(x_ref, sh_ref, sc_ref, g_ref, w_ref, cos_ref, sa_ref, sb_ref, *outs, layer):
    i = pl.program_id(0)
    h = _ada_norm(x_ref[...], g_ref[...], sh_ref[0], sc_ref[0])
    acc = jnp.dot(h.astype(BF16), w_ref[...], preferred_element_type=F32)
    if layer == 0:
        ha_ref, hq_ref, st_ref = outs
        ha_ref[...] = acc[:, :2 * CONV_CH]
        base, n_q, n_rope, n_all, st_lo = 2 * CONV_CH, 4, 5, 6, 2 * CONV_CH + 512
        q_scale = ATTN_SCALE
    else:
        hq_ref, st_ref = outs
        base, n_q, n_rope, n_all, st_lo = 0, 8, 16, 24, 1024
        q_scale = ATTN_SCALE * LOG2E
    is_prompt = i < NP // TM

    def chunk(c):
        v = acc[:, base + c * LANES: base + (c + 1) * LANES]
        return v * q_scale if c < n_q else v

    @pl.when(is_prompt)
    def _():
        for c in range(n_all):
            hq_ref[:, c * LANES:(c + 1) * LANES] = chunk(c).astype(BF16)
        st_ref[...] = acc[:, st_lo:]

    @pl.when(jnp.logical_not(is_prompt))
    def _():
        cos, sa, sb = cos_ref[...], sa_ref[...], sb_ref[...]
        for c in range(n_all):
            v = chunk(c)
            if c < n_rope:
                v = _rope(v, cos, sa, sb)
            hq_ref[:, c * LANES:(c + 1) * LANES] = v.astype(BF16)


def _inproj(x, mods, g, w_bf, rope_tabs, layer):
    npb = NP // TM
    spb = DEC_SEQ // TM
    nout = w_bf.shape[1]
    rope_spec = pl.BlockSpec((TM, LANES), lambda i: (jnp.where(i < npb, 0, (i - npb) % spb), 0))
    st_w = 256 if layer == 0 else 2048
    st_spec = pl.BlockSpec((TM, st_w), lambda i: (jnp.minimum(i, npb - 1), 0))
    if layer == 0:
        out_shape = (jax.ShapeDtypeStruct((N, 2 * CONV_CH), F32),
                     jax.ShapeDtypeStruct((N, 768), BF16),
                     jax.ShapeDtypeStruct((NP, st_w), F32))
        out_specs = (pl.BlockSpec((TM, 2 * CONV_CH), lambda i: (i, 0)),
                     pl.BlockSpec((TM, 768), lambda i: (i, 0)), st_spec)
    else:
        out_shape = (jax.ShapeDtypeStruct((N, IN1_W), BF16),
                     jax.ShapeDtypeStruct((NP, st_w), F32))
        out_specs = (pl.BlockSpec((TM, IN1_W), lambda i: (i, 0)), st_spec)
    return pl.pallas_call(
        functools.partial(_inproj_kernel, layer=layer),
        grid=(N // TM,),
        in_specs=[pl.BlockSpec((TM, D_MODEL), lambda i: (i, 0)),
                  _mod_spec(0, TM), _mod_spec(1, TM),
                  _const_spec((1, D_MODEL)),
                  _const_spec((D_MODEL, nout)),
                  rope_spec, rope_spec, rope_spec],
        out_specs=out_specs,
        out_shape=out_shape,
        compiler_params=_cparams(("arbitrary",)),
        name="inproj_l%d" % layer,
    )(x, mods, mods, g.reshape(1, -1), w_bf, *rope_tabs)


def _conv_kernel(prev_ref, cur_ref, next_ref, w_ref, b_ref, g_ref, bb_ref, o_ref, upad):
    i = pl.program_id(0)
    npb = NP // CB
    spb = DEC_SEQ // CB
    j = (i - npb) % spb
    first = jnp.logical_or(i < npb, j == 0)
    last = jnp.logical_or(i < npb, j == spb - 1)

    def glu(r):
        return r[:, :CONV_CH] * jax.nn.sigmoid(r[:, CONV_CH:])

    upad[HALO:HALO + CB, :] = glu(cur_ref[...])
    upad[0:HALO, :] = jnp.where(first, 0.0, glu(prev_ref[...]))
    upad[HALO + CB:, :] = jnp.where(last, 0.0, glu(next_ref[...]))
    acc = jnp.zeros((CB, CONV_CH), F32)
    off = HALO - CONV_W // 2
    for t in range(CONV_W):
        acc = acc + w_ref[t:t + 1, :] * upad[off + t:off + t + CB, :]
    u = acc + b_ref[...]
    mu = jnp.mean(u, axis=-1, keepdims=True)
    var = jnp.mean(jnp.square(u - mu), axis=-1, keepdims=True)
    y = (u - mu) * lax.rsqrt(var + EPS) * g_ref[...] + bb_ref[...]
    o_ref[...] = (y * jax.nn.sigmoid(y)).astype(BF16)


def _conv(ha, conv_w, conv_b, cg, cb):
    hb = CB // HALO
    nh = N // HALO
    w_pad = jnp.concatenate([conv_w, jnp.zeros((1, CONV_CH), F32)], axis=0)
    return pl.pallas_call(
        _conv_kernel,
        grid=(N // CB,),
        in_specs=[pl.BlockSpec((HALO, 2 * CONV_CH), lambda i: (jnp.maximum(i * hb - 1, 0), 0)),
                  pl.BlockSpec((CB, 2 * CONV_CH), lambda i: (i, 0)),
                  pl.BlockSpec((HALO, 2 * CONV_CH), lambda i: (jnp.minimum((i + 1) * hb, nh - 1), 0)),
                  _const_spec((CONV_W + 1, CONV_CH)),
                  _const_spec((1, CONV_CH)), _const_spec((1, CONV_CH)), _const_spec((1, CONV_CH))],
        out_specs=pl.BlockSpec((CB, CONV_CH), lambda i: (i, 0)),
        out_shape=jax.ShapeDtypeStruct((N, CONV_CH), BF16),
        scratch_shapes=[pltpu.VMEM((CB + 2 * HALO, CONV_CH), F32)],
        compiler_params=_cparams(("arbitrary",)),
        name="conformer_conv",
    )(ha, ha, ha, w_pad, conv_b.reshape(1, -1), cg.reshape(1, -1), cb.reshape(1, -1))


def _sink_attend(q, k, v, sink, mask):
    s = lax.dot_general(q, k, _NT, preferred_element_type=F32)
    if mask is not None:
        s = jnp.where(mask, s, NEG)
    m = jnp.maximum(jnp.max(s, axis=-1, keepdims=True), sink)
    p = jnp.exp(s - m)
    den = jnp.sum(p, axis=-1, keepdims=True) + jnp.exp(sink - m)
    return jnp.dot(p.astype(BF16), v, preferred_element_type=F32) / den


def _l0_prompt_attn_kernel(sink_ref, q_ref, k_ref, v_ref, o_ref):
    k = k_ref[...]
    v = v_ref[...]
    outs = []
    for h in range(WIN_HEADS):
        j = h // WIN_G
        outs.append(_sink_attend(q_ref[:, h * HEAD_DIM:(h + 1) * HEAD_DIM],
                                 k[:, j * HEAD_DIM:(j + 1) * HEAD_DIM],
                                 v[:, j * HEAD_DIM:(j + 1) * HEAD_DIM], sink_ref[h], None))
    o_ref[...] = jnp.concatenate(outs, axis=1).astype(BF16)


def _l0_prompt_attn(hq, sink):
    return pl.pallas_call(
        _l0_prompt_attn_kernel,
        grid=(BATCH,),
        in_specs=[pl.BlockSpec(memory_space=pltpu.SMEM),
                  pl.BlockSpec((SEQ, 512), lambda b: (b, 0)),
                  pl.BlockSpec((SEQ, LANES), lambda b: (b, 4)),
                  pl.BlockSpec((SEQ, LANES), lambda b: (b, 5))],
        out_specs=pl.BlockSpec((SEQ, 512), lambda b: (b, 0)),
        out_shape=jax.ShapeDtypeStruct((NP, 512), BF16),
        compiler_params=_cparams(("arbitrary",)),
        name="l0_prompt_attn",
    )(sink, hq, hq, hq)


def _l0_window_attn_kernel(sink_ref, q_ref, kp_ref, kc_ref, kn_ref, vp_ref, vc_ref, vn_ref,
                           ck_ref, cv_ref, o_ref):
    n = pl.program_id(1)
    k = jnp.concatenate([kp_ref[...], kc_ref[...], kn_ref[...], ck_ref[...].astype(BF16)], axis=0)
    v = jnp.concatenate([vp_ref[...], vc_ref[...], vn_ref[...], cv_ref[...].astype(BF16)], axis=0)
    qpos = n * BLOCK + lax.broadcasted_iota(I32, (BLOCK, 3 * BLOCK + PAST_LEN), 0)
    col = lax.broadcasted_iota(I32, (BLOCK, 3 * BLOCK + PAST_LEN), 1)
    kpos = (n - 1) * BLOCK + col
    local_ok = (jnp.abs(kpos - qpos) <= WINDOW) & (kpos >= 0) & (kpos < DEC_SEQ)
    mask = jnp.logical_or(col >= 3 * BLOCK, local_ok)
    outs = []
    for h in range(WIN_HEADS):
        j = h // WIN_G
        outs.append(_sink_attend(q_ref[:, h * HEAD_DIM:(h + 1) * HEAD_DIM],
                                 k[:, j * HEAD_DIM:(j + 1) * HEAD_DIM],
                                 v[:, j * HEAD_DIM:(j + 1) * HEAD_DIM], sink_ref[h], mask))
    o_ref[...] = jnp.concatenate(outs, axis=1).astype(BF16)


def _l0_window_attn(hq, sink, ck, cv):
    nb = DEC_SEQ // BLOCK
    r0 = NP // BLOCK

    def kv_spec(col, d):
        return pl.BlockSpec((BLOCK, LANES),
                            lambda b, n: (r0 + b * nb + jnp.clip(n + d, 0, nb - 1), col))

    ctx_spec = pl.BlockSpec((None, PAST_LEN, LANES), lambda b, n: (b, 0, 0))
    return pl.pallas_call(
        _l0_window_attn_kernel,
        grid=(DEC_BATCH, nb),
        in_specs=[pl.BlockSpec(memory_space=pltpu.SMEM),
                  pl.BlockSpec((BLOCK, 512), lambda b, n: (r0 + b * nb + n, 0)),
                  kv_spec(4, -1), kv_spec(4, 0), kv_spec(4, 1),
                  kv_spec(5, -1), kv_spec(5, 0), kv_spec(5, 1),
                  ctx_spec, ctx_spec],
        out_specs=pl.BlockSpec((BLOCK, 512), lambda b, n: (b * nb + n, 0)),
        out_shape=jax.ShapeDtypeStruct((NS, 512), BF16),
        compiler_params=_cparams(("arbitrary", "arbitrary")),
        name="l0_window_attn",
    )(sink, hq, hq, hq, hq, hq, hq, hq, ck, cv)


def _diff_attn_kernel(*refs, lam_init, has_ctx):
    if has_ctx:
        q_ref, k_ref, v_ref, ck_ref, cv_ref, lam_ref, g_ref, o_ref = refs
    else:
        q_ref, k_ref, v_ref, lam_ref, g_ref, o_ref = refs
    lv = lam_ref[...]
    lam = (jnp.exp(jnp.sum(lv[0:1] * lv[1:2], axis=-1, keepdims=True))
           - jnp.exp(jnp.sum(lv[2:3] * lv[3:4], axis=-1, keepdims=True)) + lam_init)
    g = g_ref[...]
    outs = []
    for h in range(DIFF_HEADS):
        probs = []
        for c in range(2):
            lo = h * DIFF_V + c * HEAD_DIM
            q = q_ref[:, lo:lo + HEAD_DIM]
            s = lax.dot_general(q, k_ref[:, lo:lo + HEAD_DIM], _NT, preferred_element_type=F32)
            m = jnp.max(s, axis=-1, keepdims=True)
            if has_ctx:
                sc = lax.dot_general(q, ck_ref[:, lo:lo + HEAD_DIM].astype(BF16), _NT,
                                     preferred_element_type=F32)
                m = jnp.maximum(m, jnp.max(sc, axis=-1, keepdims=True))
                pc = jnp.exp2(sc - m)
            p = jnp.exp2(s - m)
            den = jnp.sum(p, axis=-1, keepdims=True)
            if has_ctx:
                den = den + jnp.sum(pc, axis=-1, keepdims=True)
                probs.append((p, pc, den))
            else:
                probs.append((p, None, den))
        (p0, pc0, d0), (p1, pc1, d1) = probs
        ratio = lam * d0 / d1
        a = (p0 - p1 * ratio).astype(BF16)
        o = jnp.dot(a, v_ref[:, h * DIFF_V:(h + 1) * DIFF_V], preferred_element_type=F32)
        if has_ctx:
            ac = (pc0 - pc1 * ratio).astype(BF16)
            o = o + jnp.dot(ac, cv_ref[:, h * DIFF_V:(h + 1) * DIFF_V].astype(BF16),
                            preferred_element_type=F32)
        o = o / d0
        ms = jnp.mean(o * o, axis=-1, keepdims=True)
        outs.append(((o * lax.rsqrt(ms + EPS)) * g) * (1.0 - lam_init))
    o_ref[...] = jnp.concatenate(outs, axis=1).astype(BF16)


def _diff_attn(h1, lam_vecs, subln_g, lam_init, ctx):
    g = subln_g.reshape(1, DIFF_V)
    small = [_const_spec((4, HEAD_DIM)), _const_spec((1, DIFF_V))]
    if ctx is None:
        grid = (BATCH, 1)
        in_specs = [pl.BlockSpec((SEQ, D_MODEL), lambda b, i: (b, 0)),
                    pl.BlockSpec((SEQ, D_MODEL), lambda b, i: (b, 1)),
                    pl.BlockSpec((SEQ, D_MODEL), lambda b, i: (b, 2))] + small
        out_specs = pl.BlockSpec((SEQ, D_MODEL), lambda b, i: (b, 0))
        rows = NP
        args = (h1, h1, h1, lam_vecs, g)
        name = "diff_attn_prompt"
    else:
        nq = DEC_SEQ // TQ
        q0 = NP // TQ
        s0 = NP // DEC_SEQ
        ctx_spec = pl.BlockSpec((None, PAST_LEN, D_MODEL), lambda b, i: (b, 0, 0))
        grid = (DEC_BATCH, nq)
        in_specs = [pl.BlockSpec((TQ, D_MODEL), lambda b, i: (q0 + b * nq + i, 0)),
                    pl.BlockSpec((DEC_SEQ, D_MODEL), lambda b, i: (s0 + b, 1)),
                    pl.BlockSpec((DEC_SEQ, D_MODEL), lambda b, i: (s0 + b, 2)),
                    ctx_spec, ctx_spec] + small
        out_specs = pl.BlockSpec((TQ, D_MODEL), lambda b, i: (b * nq + i, 0))
        rows = NS
        args = (h1, h1, h1, ctx[0], ctx[1], lam_vecs, g)
        name = "diff_attn_sample"
    return pl.pallas_call(
        functools.partial(_diff_attn_kernel, lam_init=lam_init, has_ctx=ctx is not None),
        grid=grid, in_specs=in_specs, out_specs=out_specs,
        out_shape=jax.ShapeDtypeStruct((rows, D_MODEL), BF16),
        compiler_params=_cparams(("arbitrary", "arbitrary")),
        name=name,
    )(*args)


def _store_row_tiles(ref, val, rows):
    for s in range(ROW_TILES):
        ref[pl.ds(s, rows, stride=ROW_TILES), :] = val[:, s * LANES:(s + 1) * LANES]


def _post_kernel(*refs, layer):
    if layer == 0:
        (a_ref, op_ref, os_ref, x_ref, g1_ref, sh2_ref, sc2_ref, w_ref, n2g_ref, rw_ref, rb_ref,
         xo_ref, xn_ref, eid_ref, gate_ref, cnt_ref) = refs
    else:
        (op_ref, os_ref, x_ref, g1_ref, sh2_ref, sc2_ref, w_ref, n2g_ref, rw_ref, rb_ref,
         xo_ref, xn_ref, eid_ref, gate_ref, cnt_ref) = refs
    i = pl.program_id(0)
    o = jnp.where(i < NP // TM, op_ref[...], os_ref[...])
    if layer == 0:
        mix = (jnp.dot(a_ref[...], w_ref[:CONV_CH, :], preferred_element_type=F32)
               + jnp.dot(o, w_ref[CONV_CH:, :], preferred_element_type=F32))
    else:
        mix = jnp.dot(o, w_ref[...], preferred_element_type=F32)
    x1 = x_ref[...] + g1_ref[0] * mix
    xo_ref[...] = x1
    xn = _ada_norm(x1, n2g_ref[...], sh2_ref[0], sc2_ref[0])
    _store_row_tiles(xn_ref, xn, TM)
    logits = jnp.dot(xn, rw_ref[...], preferred_element_type=F32,
                     precision=lax.Precision.HIGHEST) + rb_ref[...]
    lane = lax.broadcasted_iota(I32, logits.shape, 1)
    vals, idxs = [], []
    picked = jnp.zeros(logits.shape, F32)
    for _ in range(TOP_K):
        m = jnp.max(logits, axis=-1, keepdims=True)
        idx = jnp.min(jnp.where(logits == m, lane, N_EXPERTS), axis=-1, keepdims=True)
        vals.append(m)
        idxs.append(idx)
        hit = lane == idx
        picked = picked + hit.astype(F32)
        logits = jnp.where(hit, -jnp.inf, logits)
    es = [jnp.exp(v - vals[0]) for v in vals]
    den = es[0] + es[1] + es[2] + es[3]
    eid_ref[...] = jnp.concatenate(idxs, axis=1)
    gate_ref[...] = jnp.concatenate([e / den for e in es], axis=1)

    @pl.when(i == 0)
    def _():
        cnt_ref[...] = jnp.zeros_like(cnt_ref)

    cnt_ref[...] += jnp.sum(picked, axis=0, keepdims=True)


def _post(layer, mix_parts, x, mods, w_bf, n2g, rw, rb):
    npb = NP // TM
    if layer == 0:
        a, o_p, o_s = mix_parts
        wo = 512
        mix_specs = [pl.BlockSpec((TM, CONV_CH), lambda i: (i, 0))]
        mix_args = [a, o_p, o_s]
    else:
        o_p, o_s = mix_parts
        wo = D_MODEL
        mix_specs = []
        mix_args = [o_p, o_s]
    mix_specs += [pl.BlockSpec((TM, wo), lambda i: (jnp.minimum(i, npb - 1), 0)),
                  pl.BlockSpec((TM, wo), lambda i: (jnp.maximum(i - npb, 0), 0))]
    return pl.pallas_call(
        functools.partial(_post_kernel, layer=layer),
        grid=(N // TM,),
        in_specs=mix_specs + [pl.BlockSpec((TM, D_MODEL), lambda i: (i, 0)),
                              _mod_spec(2, TM), _mod_spec(3, TM), _mod_spec(4, TM),
                              _const_spec((D_MODEL, D_MODEL)), _const_spec((1, D_MODEL)),
                              _const_spec((D_MODEL, N_EXPERTS)), _const_spec((1, N_EXPERTS))],
        out_specs=(pl.BlockSpec((TM, D_MODEL), lambda i: (i, 0)),
                   pl.BlockSpec((TM * ROW_TILES, LANES), lambda i: (i, 0)),
                   pl.BlockSpec((TM, TOP_K), lambda i: (i, 0)),
                   pl.BlockSpec((TM, TOP_K), lambda i: (i, 0)),
                   _const_spec((1, N_EXPERTS))),
        out_shape=(jax.ShapeDtypeStruct((N, D_MODEL), F32),
                   jax.ShapeDtypeStruct((N * ROW_TILES, LANES), F32),
                   jax.ShapeDtypeStruct((N, TOP_K), I32),
                   jax.ShapeDtypeStruct((N, TOP_K), F32),
                   jax.ShapeDtypeStruct((1, N_EXPERTS), F32)),
        compiler_params=_cparams(("arbitrary",)),
        name="post_l%d" % layer,
    )(*mix_args, x, mods, mods, mods, w_bf, n2g.reshape(1, -1), rw, rb.reshape(1, -1))


def _slots_kernel(cnt_ref, bst_ref, eidt_ref, base_ref, triu_ref, slot_ref, carry, dvm, dsm, sem):
    j = pl.program_id(0)

    @pl.when(j == 0)
    def _():
        carry[...] = jnp.zeros_like(carry)

    e_iota = lax.broadcasted_iota(I32, (N_EXPERTS, RT), 0)
    ohs = [(eidt_ref[k:k + 1, :] == e_iota).astype(F32) for k in range(TOP_K)]
    ohsum = ohs[0] + ohs[1] + ohs[2] + ohs[3]
    cum = jnp.dot(ohsum.astype(BF16), triu_ref[...], preferred_element_type=F32)
    tot = cum + carry[...] + base_ref[...]
    dvm[...] = jnp.concatenate(
        [jnp.sum(oh * tot, axis=0, keepdims=True) for oh in ohs], axis=0).astype(I32)
    carry[...] += jnp.sum(ohsum, axis=1, keepdims=True)

    cp = pltpu.make_async_copy(dvm, dsm, sem)
    cp.start()
    cp.wait()

    def scatter(c, carry_):
        for u in range(SUBLANES):
            t = c * SUBLANES + u
            a0 = (j * RT + t) * TOP_K
            for k in range(TOP_K):
                slot_ref[dsm[k, t]] = a0 + k
        return carry_

    lax.fori_loop(0, RT // SUBLANES, scatter, 0)

    @pl.when(j == pl.num_programs(0) - 1)
    def _():
        def pad(s, c):
            slot_ref[s] = (N + s % SLOT_LEAD) * TOP_K + (TOP_K - 1)
            return c

        def per_expert(e, c):
            cnt = cnt_ref[e]
            b0 = bst_ref[e]
            lax.fori_loop(SLOT_LEAD + b0 * TME + cnt,
                          SLOT_LEAD + (b0 + (cnt + TME - 1) // TME) * TME, pad, 0)
            return c

        lax.fori_loop(0, SLOT_LEAD, pad, 0)
        lax.fori_loop(0, N_EXPERTS, per_expert, 0)
        lax.fori_loop(SLOT_LEAD + bst_ref[N_EXPERTS] * TME, N_SLOTS, pad, 0)


def _route(eid, counts_f):
    counts = counts_f.reshape(-1).astype(I32)
    nblk = (counts + TME - 1) // TME
    bend = jnp.cumsum(nblk)
    bstart = bend - nblk
    bst = jnp.concatenate([bstart, bend[-1:]]).astype(I32)
    base = (bstart * TME + SLOT_LEAD).astype(F32).reshape(N_EXPERTS, 1)
    ar = jnp.arange(RT, dtype=I32)
    triu = (ar[:, None] < ar[None, :]).astype(BF16)
    grid_spec = pltpu.PrefetchScalarGridSpec(
        num_scalar_prefetch=2,
        grid=(N // RT,),
        in_specs=[pl.BlockSpec((TOP_K, RT), lambda j, c, b: (0, j)),
                  _const_spec((N_EXPERTS, 1)),
                  _const_spec((RT, RT))],
        out_specs=pl.BlockSpec(memory_space=pltpu.SMEM),
        scratch_shapes=[pltpu.VMEM((N_EXPERTS, 1), F32),
                        pltpu.VMEM((TOP_K, RT), I32),
                        pltpu.SMEM((TOP_K, RT), I32),
                        pltpu.SemaphoreType.DMA],
    )
    slot_asg = pl.pallas_call(
        _slots_kernel,
        grid_spec=grid_spec,
        out_shape=jax.ShapeDtypeStruct((N_SLOTS,), I32),
        compiler_params=_cparams(("arbitrary",)),
        name="slots",
    )(counts, bst, eid.T, base, triu)
    return slot_asg, bst, counts


def _row_tile(idx):
    if isinstance(idx, int):
        return pl.ds(idx * ROW_TILES, ROW_TILES)
    return pl.ds(pl.multiple_of(idx * ROW_TILES, ROW_TILES), ROW_TILES)


def _expert_kernel(slot_ref, bst_ref, cnt_ref, xn_hbm, upw_ref, upb_ref, dww_ref, dwb_ref,
                   y_hbm, gbuf, sbuf, upbf, dwbf, gsem, ssem):
    e = pl.program_id(0)
    nb = (cnt_ref[e] + TME - 1) // TME
    b0 = bst_ref[e]

    def gather_row(g, buf, r):
        v = slot_ref[(g + 2) * TME + r]
        tok = jnp.minimum(lax.shift_right_logical(v, TOP_K_SHIFT), N - 1)
        pltpu.make_async_copy(xn_hbm.at[_row_tile(tok), :], gbuf.at[buf, _row_tile(r), :],
                              gsem.at[buf]).start()

    def scatter_row(g, buf, r):
        v = slot_ref[(g + 2) * TME + r]
        dst = (v & (TOP_K - 1)) * N + lax.shift_right_logical(v, TOP_K_SHIFT)
        pltpu.make_async_copy(sbuf.at[buf, _row_tile(r), :], y_hbm.at[_row_tile(dst), :],
                              ssem.at[buf]).start()

    def rolled(row_fn, g, buf):
        def body(r, c):
            row_fn(g, buf, r)
            return c
        lax.fori_loop(0, TME, body, 0)

    def wait_gather(buf):
        pltpu.make_async_copy(xn_hbm.at[pl.ds(0, TME * ROW_TILES), :], gbuf.at[buf], gsem.at[buf]).wait()

    def wait_scatter(buf):
        pltpu.make_async_copy(sbuf.at[buf], y_hbm.at[pl.ds(0, TME * ROW_TILES), :], ssem.at[buf]).wait()

    @pl.when(e == 0)
    def _():
        sbuf[...] = jnp.zeros_like(sbuf)
        rolled(gather_row, 0, 0)
        rolled(scatter_row, -2, 0)

    @pl.when(nb > 0)
    def _():
        upbf[...] = upw_ref[0].astype(BF16)
        dwbf[...] = dww_ref[0].astype(BF16)

    def block(j, carry):
        g = b0 + j
        cur = g % 2
        wait_gather(cur)
        for r in range(TME):
            gather_row(g + 1, 1 - cur, r)
        for r in range(TME):
            scatter_row(g - 1, 1 - cur, r)
        xb = jnp.concatenate(
            [gbuf[cur, pl.ds(s, TME, stride=ROW_TILES), :] for s in range(ROW_TILES)], axis=1).astype(BF16)
        gu = jnp.dot(xb, upbf[...], preferred_element_type=F32) + upb_ref[0]
        gg = jnp.minimum(gu[:, :D_FF], SWIGLU_LIMIT)
        lin = jnp.clip(gu[:, D_FF:], -SWIGLU_LIMIT, SWIGLU_LIMIT)
        hdn = gg * jax.nn.sigmoid(SWIGLU_ALPHA * gg) * (lin + 1.0)
        y = jnp.dot(hdn.astype(BF16), dwbf[...], preferred_element_type=F32) + dwb_ref[0]
        wait_scatter(cur)
        for s in range(ROW_TILES):
            sbuf[cur, pl.ds(s, TME, stride=ROW_TILES), :] = y[:, s * LANES:(s + 1) * LANES]
        return carry

    lax.fori_loop(0, nb, block, 0)

    @pl.when(e == pl.num_programs(0) - 1)
    def _():
        g_end = bst_ref[N_EXPERTS]
        rolled(scatter_row, g_end - 1, (g_end - 1) % 2)
        wait_scatter(0)
        wait_scatter(1)
        wait_gather(g_end % 2)


def _experts(slot_asg, bst, counts, xn_tiles, up_w, up_b, down_w, down_b):
    grid_spec = pltpu.PrefetchScalarGridSpec(
        num_scalar_prefetch=3,
        grid=(N_EXPERTS,),
        in_specs=[pl.BlockSpec(memory_space=pl.ANY),
                  pl.BlockSpec((1, D_MODEL, 2 * D_FF), lambda e, *_: (e, 0, 0)),
                  pl.BlockSpec((1, 1, 2 * D_FF), lambda e, *_: (e, 0, 0)),
                  pl.BlockSpec((1, D_FF, D_MODEL), lambda e, *_: (e, 0, 0)),
                  pl.BlockSpec((1, 1, D_MODEL), lambda e, *_: (e, 0, 0))],
        out_specs=pl.BlockSpec(memory_space=pl.ANY),
        scratch_shapes=[pltpu.VMEM((2, TME * ROW_TILES, LANES), F32),
                        pltpu.VMEM((2, TME * ROW_TILES, LANES), F32),
                        pltpu.VMEM((D_MODEL, 2 * D_FF), BF16),
                        pltpu.VMEM((D_FF, D_MODEL), BF16),
                        pltpu.SemaphoreType.DMA((2,)),
                        pltpu.SemaphoreType.DMA((2,))],
    )
    return pl.pallas_call(
        _expert_kernel,
        grid_spec=grid_spec,
        out_shape=jax.ShapeDtypeStruct(((N_ASG + SLOT_LEAD) * ROW_TILES, LANES), F32),
        compiler_params=_cparams(("arbitrary",)),
        name="experts",
    )(slot_asg, bst, counts, xn_tiles, up_w, up_b.reshape(N_EXPERTS, 1, -1),
      down_w, down_b.reshape(N_EXPERTS, 1, -1))


def _combine_kernel(x_ref, gate_ref, g2_ref, fg_ref, y0_ref, y1_ref, y2_ref, y3_ref, o_ref, *, final):
    gate = gate_ref[...]
    cols = []
    for s in range(ROW_TILES):
        acc = None
        for k, y_ref in enumerate((y0_ref, y1_ref, y2_ref, y3_ref)):
            term = gate[:, k:k + 1] * y_ref[pl.ds(s, TMC, stride=ROW_TILES), :]
            acc = term if acc is None else acc + term
        cols.append(acc)
    x2 = x_ref[...] + g2_ref[0] * jnp.concatenate(cols, axis=1)
    if final:
        ms = jnp.mean(x2 * x2, axis=-1, keepdims=True)
        x2 = x2 * lax.rsqrt(ms + EPS) * fg_ref[...]
    o_ref[...] = x2


def _combine(x, gates, mods, final_g, y_tiles, final):
    nblk = N // TMC

    def y_spec(k):
        return pl.BlockSpec((TMC * ROW_TILES, LANES), lambda i: (k * nblk + i, 0))

    return pl.pallas_call(
        functools.partial(_combine_kernel, final=final),
        grid=(nblk,),
        in_specs=[pl.BlockSpec((TMC, D_MODEL), lambda i: (i, 0)),
                  pl.BlockSpec((TMC, TOP_K), lambda i: (i, 0)),
                  _mod_spec(5, TMC),
                  _const_spec((1, D_MODEL)),
                  y_spec(0), y_spec(1), y_spec(2), y_spec(3)],
        out_specs=pl.BlockSpec((TMC, D_MODEL), lambda i: (i, 0)),
        out_shape=jax.ShapeDtypeStruct((N, D_MODEL), F32),
        compiler_params=_cparams(("arbitrary",)),
        name="combine",
    )(x, gates, mods, final_g.reshape(1, -1), y_tiles, y_tiles, y_tiles, y_tiles)


def _rope_tables():
    pos = jnp.arange(DEC_SEQ)
    r = (pos // GRID_W).astype(F32)
    col = (pos % GRID_W).astype(F32)
    quarter = HEAD_DIM // 4
    inv = ROPE_BASE ** (-jnp.arange(quarter, dtype=F32) / quarter)
    ar = r[:, None] * inv
    ac = col[:, None] * inv
    ang = jnp.concatenate([ar, ar, ac, ac], axis=-1)
    cos = jnp.tile(jnp.cos(ang), (1, LANES // HEAD_DIM))
    sin = jnp.tile(jnp.sin(ang), (1, LANES // HEAD_DIM))
    first = (jnp.arange(LANES) % 32) < 16
    sa = jnp.where(first[None, :], -sin, 0.0)
    sb = jnp.where(first[None, :], 0.0, sin)
    return cos, sa, sb


def _moe(x, xn, eid, gates, counts, mods, final_g, up_w, up_b, down_w, down_b, final):
    slot_asg, bst, counts_i = _route(eid, counts)
    y = _experts(slot_asg, bst, counts_i, xn, up_w, up_b, down_w, down_b)
    return _combine(x, gates, mods, final_g, y, final)


@jax.jit
def kernel(x_prompt, x_sample, cache_l0_k, cache_l0_v, cache_l1_k, cache_l1_v, c, c_ctx, final_g,
           l0_mod_w, l0_mod_b, l0_norm1_g, l0_w_in, l0_conv_w, l0_conv_b, l0_cnorm_g, l0_cnorm_b,
           l0_sink, l0_w_out, l0_norm2_g, l0_router_w, l0_router_b, l0_up_w, l0_up_b, l0_down_w,
           l0_down_b,
           l1_mod_w, l1_mod_b, l1_norm1_g, l1_w_in, l1_lam_q1, l1_lam_k1, l1_lam_q2, l1_lam_k2,
           l1_subln_g, l1_w_out, l1_norm2_g, l1_router_w, l1_router_b, l1_up_w, l1_up_b, l1_down_w,
           l1_down_b):
    x = jnp.concatenate([x_prompt.reshape(NP, D_MODEL), x_sample.reshape(NS, D_MODEL)], axis=0)
    cond16 = jnp.concatenate([c, c_ctx[None, :], jnp.zeros((16 - DEC_BATCH - 1, D_MODEL), F32)], axis=0)
    rope_tabs = _rope_tables()

    mods = _modulation(cond16, l0_mod_w, l0_mod_b)
    ha, hq, st0 = _inproj(x, mods, l0_norm1_g, l0_w_in.astype(BF16), rope_tabs, 0)
    a = _conv(ha, l0_conv_w, l0_conv_b, l0_cnorm_g, l0_cnorm_b)
    o_p = _l0_prompt_attn(hq, l0_sink)
    o_s = _l0_window_attn(hq, l0_sink, cache_l0_k.reshape(DEC_BATCH, PAST_LEN, LANES),
                          cache_l0_v.reshape(DEC_BATCH, PAST_LEN, LANES))
    x, xn, eid, gates, counts = _post(0, (a, o_p, o_s), x, mods, l0_w_out.astype(BF16), l0_norm2_g,
                                      l0_router_w, l0_router_b)
    x = _moe(x, xn, eid, gates, counts, mods, final_g, l0_up_w, l0_up_b, l0_down_w, l0_down_b, False)
    state_l0_k = st0[:, :LANES].reshape(BATCH, SEQ, WIN_KV, HEAD_DIM)
    state_l0_v = st0[:, LANES:].reshape(BATCH, SEQ, WIN_KV, HEAD_DIM)

    lam_init = 0.8 - 0.6 * math.exp(-0.3 * 1)
    mods = _modulation(cond16, l1_mod_w, l1_mod_b)
    h1, st1 = _inproj(x, mods, l1_norm1_g, l1_w_in.astype(BF16), rope_tabs, 1)
    lam_vecs = jnp.stack([l1_lam_q1, l1_lam_k1, l1_lam_q2, l1_lam_k2], axis=0)
    o_p = _diff_attn(h1, lam_vecs, l1_subln_g, lam_init, None)
    o_s = _diff_attn(h1, lam_vecs, l1_subln_g, lam_init,
                     (cache_l1_k.reshape(DEC_BATCH, PAST_LEN, D_MODEL),
                      cache_l1_v.reshape(DEC_BATCH, PAST_LEN, D_MODEL)))
    x, xn, eid, gates, counts = _post(1, (o_p, o_s), x, mods, l1_w_out.astype(BF16), l1_norm2_g,
                                      l1_router_w, l1_router_b)
    x = _moe(x, xn, eid, gates, counts, mods, final_g, l1_up_w, l1_up_b, l1_down_w, l1_down_b, True)
    state_l1_k = st1[:, :D_MODEL].reshape(BATCH, SEQ, DIFF_HEADS, 2, HEAD_DIM)
    state_l1_v = st1[:, D_MODEL:].reshape(BATCH, SEQ, DIFF_HEADS, DIFF_V)

    y_prompt = x[:NP].reshape(BATCH, SEQ, D_MODEL)
    y_sample = x[NP:].reshape(DEC_BATCH, DEC_SEQ, D_MODEL)
    return (y_prompt, y_sample, state_l0_k, state_l0_v, state_l1_k, state_l1_v)
```

```python
import functools
import math

import jax
import jax.numpy as jnp
from jax import lax
from jax.experimental import pallas as pl
from jax.experimental.pallas import tpu as pltpu

F32 = jnp.float32
BF16 = jnp.bfloat16
I32 = jnp.int32

D_MODEL = 1024
BATCH = 16
SEQ = 256
DEC_BATCH = 8
DEC_SEQ = 2048
PAST_LEN = 256
GRID_W = 64
HEAD_DIM = 64
BLOCK = 128
WINDOW = 128
ROPE_BASE = 10000.0
EPS = 1e-6
NEG = -1e30
ATTN_SCALE = HEAD_DIM ** -0.5
LOG2E = math.log2(math.e)
CONV_CH = 512
CONV_W = 31
WIN_HEADS = 8
WIN_KV = 2
WIN_G = 4
IN0_W = 1792
DIFF_HEADS = 8
DIFF_V = 128
IN1_W = 3072
N_EXPERTS = 32
TOP_K = 4
TOP_K_SHIFT = 2
D_FF = 1024
SWIGLU_LIMIT = 7.0
SWIGLU_ALPHA = 1.702

NP = BATCH * SEQ
NS = DEC_BATCH * DEC_SEQ
N = NP + NS
N_ASG = N * TOP_K

LANES = 128
SUBLANES = 8
ROW_TILES = D_MODEL // LANES
VMEM_LIMIT = 56 * 1024 * 1024

TM = 512
TMC = 512
TME = 256
CB = 256
HALO = 16
SHIFT_ROWS = CB + 2 * HALO - SUBLANES
TQ = 256
RT = 512
SCATTER_UNROLL = 64
MAX_EBLOCKS = (N_ASG + N_EXPERTS * (TME - 1)) // TME
SLOT_LEAD = 2 * TME
N_SLOTS = (MAX_EBLOCKS + 3) * TME

_NT = (((1,), (1,)), ((), ()))


def _cparams(sem):
    return pltpu.CompilerParams(dimension_semantics=sem, vmem_limit_bytes=VMEM_LIMIT)


def _mod_row(i, tm):
    npb = NP // tm
    return jnp.where(i < npb, DEC_BATCH, (i - npb) // (DEC_SEQ // tm))


def _mod_spec(j, tm):
    return pl.BlockSpec((1, 1, D_MODEL), lambda i, *_: (_mod_row(i, tm) * 6 + j, 0, 0))


def _const_spec(shape):
    nd = len(shape)
    return pl.BlockSpec(shape, lambda *_: (0,) * nd)


def _ada_norm(x, g, shift, scale):
    ms = jnp.mean(x * x, axis=-1, keepdims=True)
    return (x * lax.rsqrt(ms + EPS) * g) * (1.0 + scale) + shift


def _mod_kernel(c_ref, w_ref, b_ref, o_ref):
    c = c_ref[...]
    s = c * jax.nn.sigmoid(c)
    o_ref[...] = jnp.dot(s.astype(BF16), w_ref[...].astype(BF16), preferred_element_type=F32) + b_ref[...]


def _modulation(cond16, w, b):
    m = pl.pallas_call(
        _mod_kernel,
        grid=(6,),
        in_specs=[_const_spec((16, D_MODEL)),
                  pl.BlockSpec((D_MODEL, D_MODEL), lambda j: (0, j)),
                  pl.BlockSpec((1, D_MODEL), lambda j: (0, j))],
        out_specs=pl.BlockSpec((16, D_MODEL), lambda j: (0, j)),
        out_shape=jax.ShapeDtypeStruct((16, 6 * D_MODEL), F32),
        compiler_params=_cparams(("arbitrary",)),
        name="modulation",
    )(cond16, w, b.reshape(1, -1))
    return m.reshape(16 * 6, 1, D_MODEL)


def _rope(v, cos, sa, sb):
    return v * cos + pltpu.roll(v, LANES - 16, 1) * sa + pltpu.roll(v, 16, 1) * sb


def _inproj_kernel(x_ref, sh_ref, sc_ref, g_ref, w_ref, cos_ref, sa_ref, sb_ref, *outs, layer):
    i = pl.program_id(0)
    h = _ada_norm(x_ref[...], g_ref[...], sh_ref[0], sc_ref[0])
    acc = jnp.dot(h.astype(BF16), w_ref[...], preferred_element_type=F32)
    if layer == 0:
        ha_ref, hq_ref, st_ref = outs
        ha_ref[...] = acc[:, :2 * CONV_CH]
        base, n_q, n_rope, n_all, st_lo = 2 * CONV_CH, 4, 5, 6, 2 * CONV_CH + 512
        q_scale = ATTN_SCALE
    else:
        hq_ref, st_ref = outs
        base, n_q, n_rope, n_all, st_lo = 0, 8, 16, 24, 1024
        q_scale = ATTN_SCALE * LOG2E
    is_prompt = i < NP // TM

    def chunk(c):
        v = acc[:, base + c * LANES: base + (c + 1) * LANES]
        return v * q_scale if c < n_q else v

    @pl.when(is_prompt)
    def _():
        for c in range(n_all):
            hq_ref[:, c * LANES:(c + 1) * LANES] = chunk(c).astype(BF16)
        st_ref[...] = acc[:, st_lo:]

    @pl.when(jnp.logical_not(is_prompt))
    def _():
        cos, sa, sb = cos_ref[...], sa_ref[...], sb_ref[...]
        for c in range(n_all):
            v = chunk(c)
            if c < n_rope:
                v = _rope(v, cos, sa, sb)
            hq_ref[:, c * LANES:(c + 1) * LANES] = v.astype(BF16)


def _inproj(x, mods, g, w_bf, rope_tabs, layer):
    npb = NP // TM
    spb = DEC_SEQ // TM
    nout = w_bf.shape[1]
    rope_spec = pl.BlockSpec((TM, LANES), lambda i: (jnp.where(i < npb, 0, (i - npb) % spb), 0))
    st_w = 256 if layer == 0 else 2048
    st_spec = pl.BlockSpec((TM, st_w), lambda i: (jnp.minimum(i, npb - 1), 0))
    if layer == 0:
        out_shape = (jax.ShapeDtypeStruct((N, 2 * CONV_CH), F32),
                     jax.ShapeDtypeStruct((N, 768), BF16),
                     jax.ShapeDtypeStruct((NP, st_w), F32))
        out_specs = (pl.BlockSpec((TM, 2 * CONV_CH), lambda i: (i, 0)),
                     pl.BlockSpec((TM, 768), lambda i: (i, 0)), st_spec)
    else:
        out_shape = (jax.ShapeDtypeStruct((N, IN1_W), BF16),
                     jax.ShapeDtypeStruct((NP, st_w), F32))
        out_specs = (pl.BlockSpec((TM, IN1_W), lambda i: (i, 0)), st_spec)
    return pl.pallas_call(
        functools.partial(_inproj_kernel, layer=layer),
        grid=(N // TM,),
        in_specs=[pl.BlockSpec((TM, D_MODEL), lambda i: (i, 0)),
                  _mod_spec(0, TM), _mod_spec(1, TM),
                  _const_spec((1, D_MODEL)),
                  _const_spec((D_MODEL, nout)),
                  rope_spec, rope_spec, rope_spec],
        out_specs=out_specs,
        out_shape=out_shape,
        compiler_params=_cparams(("arbitrary",)),
        name="inproj_l%d" % layer,
    )(x, mods, mods, g.reshape(1, -1), w_bf, *rope_tabs)


def _conv_kernel(prev_ref, cur_ref, next_ref, w_ref, b_ref, g_ref, bb_ref, o_ref, upad, ush):
    i = pl.program_id(0)
    npb = NP // CB
    spb = DEC_SEQ // CB
    j = (i - npb) % spb
    first = jnp.logical_or(i < npb, j == 0)
    last = jnp.logical_or(i < npb, j == spb - 1)

    def glu(r):
        return r[:, :CONV_CH] * jax.nn.sigmoid(r[:, CONV_CH:])

    upad[HALO:HALO + CB, :] = glu(cur_ref[...])
    upad[0:HALO, :] = jnp.where(first, 0.0, glu(prev_ref[...]))
    upad[HALO + CB:, :] = jnp.where(last, 0.0, glu(next_ref[...]))
    acc = jnp.zeros((CB, CONV_CH), F32)
    off = HALO - CONV_W // 2
    for b in range(SUBLANES):
        taps = [t for t in range(CONV_W) if (off + t) % SUBLANES == b]
        if b > 0:
            ush[b - 1] = upad[b:b + SHIFT_ROWS, :]
        for t in taps:
            a8 = off + t - b
            win = upad[a8:a8 + CB, :] if b == 0 else ush[b - 1, a8:a8 + CB, :]
            acc = acc + w_ref[t:t + 1, :] * win
    u = acc + b_ref[...]
    mu = jnp.mean(u, axis=-1, keepdims=True)
    var = jnp.mean(jnp.square(u - mu), axis=-1, keepdims=True)
    y = (u - mu) * lax.rsqrt(var + EPS) * g_ref[...] + bb_ref[...]
    o_ref[...] = (y * jax.nn.sigmoid(y)).astype(BF16)


def _conv(ha, conv_w, conv_b, cg, cb):
    hb = CB // HALO
    nh = N // HALO
    w_pad = jnp.concatenate([conv_w, jnp.zeros((1, CONV_CH), F32)], axis=0)
    return pl.pallas_call(
        _conv_kernel,
        grid=(N // CB,),
        in_specs=[pl.BlockSpec((HALO, 2 * CONV_CH), lambda i: (jnp.maximum(i * hb - 1, 0), 0)),
                  pl.BlockSpec((CB, 2 * CONV_CH), lambda i: (i, 0)),
                  pl.BlockSpec((HALO, 2 * CONV_CH), lambda i: (jnp.minimum((i + 1) * hb, nh - 1), 0)),
                  _const_spec((CONV_W + 1, CONV_CH)),
                  _const_spec((1, CONV_CH)), _const_spec((1, CONV_CH)), _const_spec((1, CONV_CH))],
        out_specs=pl.BlockSpec((CB, CONV_CH), lambda i: (i, 0)),
        out_shape=jax.ShapeDtypeStruct((N, CONV_CH), BF16),
        scratch_shapes=[pltpu.VMEM((CB + 2 * HALO, CONV_CH), F32),
                        pltpu.VMEM((SUBLANES - 1, SHIFT_ROWS, CONV_CH), F32)],
        compiler_params=_cparams(("arbitrary",)),
        name="conformer_conv",
    )(ha, ha, ha, w_pad, conv_b.reshape(1, -1), cg.reshape(1, -1), cb.reshape(1, -1))


def _sink_attend(q, k, v, sink, mask):
    s = lax.dot_general(q, k, _NT, preferred_element_type=F32)
    if mask is not None:
        s = jnp.where(mask, s, NEG)
    m = jnp.maximum(jnp.max(s, axis=-1, keepdims=True), sink)
    p = jnp.exp(s - m)
    den = jnp.sum(p, axis=-1, keepdims=True) + jnp.exp(sink - m)
    return jnp.dot(p.astype(BF16), v, preferred_element_type=F32) / den


def _l0_prompt_attn_kernel(sink_ref, q_ref, k_ref, v_ref, o_ref):
    k = k_ref[...]
    v = v_ref[...]
    outs = []
    for h in range(WIN_HEADS):
        j = h // WIN_G
        outs.append(_sink_attend(q_ref[:, h * HEAD_DIM:(h + 1) * HEAD_DIM],
                                 k[:, j * HEAD_DIM:(j + 1) * HEAD_DIM],
                                 v[:, j * HEAD_DIM:(j + 1) * HEAD_DIM], sink_ref[h], None))
    o_ref[...] = jnp.concatenate(outs, axis=1).astype(BF16)


def _l0_prompt_attn(hq, sink):
    return pl.pallas_call(
        _l0_prompt_attn_kernel,
        grid=(BATCH,),
        in_specs=[pl.BlockSpec(memory_space=pltpu.SMEM),
                  pl.BlockSpec((SEQ, 512), lambda b: (b, 0)),
                  pl.BlockSpec((SEQ, LANES), lambda b: (b, 4)),
                  pl.BlockSpec((SEQ, LANES), lambda b: (b, 5))],
        out_specs=pl.BlockSpec((SEQ, 512), lambda b: (b, 0)),
        out_shape=jax.ShapeDtypeStruct((NP, 512), BF16),
        compiler_params=_cparams(("arbitrary",)),
        name="l0_prompt_attn",
    )(sink, hq, hq, hq)


def _l0_window_attn_kernel(sink_ref, q_ref, kp_ref, kc_ref, kn_ref, vp_ref, vc_ref, vn_ref,
                           ck_ref, cv_ref, o_ref):
    n = pl.program_id(1)
    k = jnp.concatenate([kp_ref[...], kc_ref[...], kn_ref[...], ck_ref[...].astype(BF16)], axis=0)
    v = jnp.concatenate([vp_ref[...], vc_ref[...], vn_ref[...], cv_ref[...].astype(BF16)], axis=0)
    qpos = n * BLOCK + lax.broadcasted_iota(I32, (BLOCK, 3 * BLOCK + PAST_LEN), 0)
    col = lax.broadcasted_iota(I32, (BLOCK, 3 * BLOCK + PAST_LEN), 1)
    kpos = (n - 1) * BLOCK + col
    local_ok = (jnp.abs(kpos - qpos) <= WINDOW) & (kpos >= 0) & (kpos < DEC_SEQ)
    mask = jnp.logical_or(col >= 3 * BLOCK, local_ok)
    outs = []
    for h in range(WIN_HEADS):
        j = h // WIN_G
        outs.append(_sink_attend(q_ref[:, h * HEAD_DIM:(h + 1) * HEAD_DIM],
                                 k[:, j * HEAD_DIM:(j + 1) * HEAD_DIM],
                                 v[:, j * HEAD_DIM:(j + 1) * HEAD_DIM], sink_ref[h], mask))
    o_ref[...] = jnp.concatenate(outs, axis=1).astype(BF16)


def _l0_window_attn(hq, sink, ck, cv):
    nb = DEC_SEQ // BLOCK
    r0 = NP // BLOCK

    def kv_spec(col, d):
        return pl.BlockSpec((BLOCK, LANES),
                            lambda b, n: (r0 + b * nb + jnp.clip(n + d, 0, nb - 1), col))

    ctx_spec = pl.BlockSpec((None, PAST_LEN, LANES), lambda b, n: (b, 0, 0))
    return pl.pallas_call(
        _l0_window_attn_kernel,
        grid=(DEC_BATCH, nb),
        in_specs=[pl.BlockSpec(memory_space=pltpu.SMEM),
                  pl.BlockSpec((BLOCK, 512), lambda b, n: (r0 + b * nb + n, 0)),
                  kv_spec(4, -1), kv_spec(4, 0), kv_spec(4, 1),
                  kv_spec(5, -1), kv_spec(5, 0), kv_spec(5, 1),
                  ctx_spec, ctx_spec],
        out_specs=pl.BlockSpec((BLOCK, 512), lambda b, n: (b * nb + n, 0)),
        out_shape=jax.ShapeDtypeStruct((NS, 512), BF16),
        compiler_params=_cparams(("arbitrary", "arbitrary")),
        name="l0_window_attn",
    )(sink, hq, hq, hq, hq, hq, hq, hq, ck, cv)


def _diff_attn_kernel(*refs, lam_init, has_ctx):
    if has_ctx:
        q_ref, k_ref, v_ref, ck_ref, cv_ref, lam_ref, g_ref, o_ref = refs
    else:
        q_ref, k_ref, v_ref, lam_ref, g_ref, o_ref = refs
    lv = lam_ref[...]
    lam = (jnp.exp(jnp.sum(lv[0:1] * lv[1:2], axis=-1, keepdims=True))
           - jnp.exp(jnp.sum(lv[2:3] * lv[3:4], axis=-1, keepdims=True)) + lam_init)
    g = g_ref[...]
    outs = []
    for h in range(DIFF_HEADS):
        probs = []
        for c in range(2):
            lo = h * DIFF_V + c * HEAD_DIM
            q = q_ref[:, lo:lo + HEAD_DIM]
            s = lax.dot_general(q, k_ref[:, lo:lo + HEAD_DIM], _NT, preferred_element_type=F32)
            m = jnp.max(s, axis=-1, keepdims=True)
            if has_ctx:
                sc = lax.dot_general(q, ck_ref[:, lo:lo + HEAD_DIM].astype(BF16), _NT,
                                     preferred_element_type=F32)
                m = jnp.maximum(m, jnp.max(sc, axis=-1, keepdims=True))
                pc = jnp.exp2(sc - m)
            p = jnp.exp2(s - m)
            den = jnp.sum(p, axis=-1, keepdims=True)
            if has_ctx:
                den = den + jnp.sum(pc, axis=-1, keepdims=True)
                probs.append((p, pc, den))
            else:
                probs.append((p, None, den))
        (p0, pc0, d0), (p1, pc1, d1) = probs
        ratio = lam * d0 / d1
        a = (p0 - p1 * ratio).astype(BF16)
        o = jnp.dot(a, v_ref[:, h * DIFF_V:(h + 1) * DIFF_V], preferred_element_type=F32)
        if has_ctx:
            ac = (pc0 - pc1 * ratio).astype(BF16)
            o = o + jnp.dot(ac, cv_ref[:, h * DIFF_V:(h + 1) * DIFF_V].astype(BF16),
                            preferred_element_type=F32)
        o = o / d0
        ms = jnp.mean(o * o, axis=-1, keepdims=True)
        outs.append(((o * lax.rsqrt(ms + EPS)) * g) * (1.0 - lam_init))
    o_ref[...] = jnp.concatenate(outs, axis=1).astype(BF16)


def _diff_attn(h1, lam_vecs, subln_g, lam_init, ctx):
    g = subln_g.reshape(1, DIFF_V)
    small = [_const_spec((4, HEAD_DIM)), _const_spec((1, DIFF_V))]
    if ctx is None:
        grid = (BATCH, 1)
        in_specs = [pl.BlockSpec((SEQ, D_MODEL), lambda b, i: (b, 0)),
                    pl.BlockSpec((SEQ, D_MODEL), lambda b, i: (b, 1)),
                    pl.BlockSpec((SEQ, D_MODEL), lambda b, i: (b, 2))] + small
        out_specs = pl.BlockSpec((SEQ, D_MODEL), lambda b, i: (b, 0))
        rows = NP
        args = (h1, h1, h1, lam_vecs, g)
        name = "diff_attn_prompt"
    else:
        nq = DEC_SEQ // TQ
        q0 = NP // TQ
        s0 = NP // DEC_SEQ
        ctx_spec = pl.BlockSpec((None, PAST_LEN, D_MODEL), lambda b, i: (b, 0, 0))
        grid = (DEC_BATCH, nq)
        in_specs = [pl.BlockSpec((TQ, D_MODEL), lambda b, i: (q0 + b * nq + i, 0)),
                    pl.BlockSpec((DEC_SEQ, D_MODEL), lambda b, i: (s0 + b, 1)),
                    pl.BlockSpec((DEC_SEQ, D_MODEL), lambda b, i: (s0 + b, 2)),
                    ctx_spec, ctx_spec] + small
        out_specs = pl.BlockSpec((TQ, D_MODEL), lambda b, i: (b * nq + i, 0))
        rows = NS
        args = (h1, h1, h1, ctx[0], ctx[1], lam_vecs, g)
        name = "diff_attn_sample"
    return pl.pallas_call(
        functools.partial(_diff_attn_kernel, lam_init=lam_init, has_ctx=ctx is not None),
        grid=grid, in_specs=in_specs, out_specs=out_specs,
        out_shape=jax.ShapeDtypeStruct((rows, D_MODEL), BF16),
        compiler_params=_cparams(("arbitrary", "arbitrary")),
        name=name,
    )(*args)


def _store_row_tiles(ref, val, rows):
    for s in range(ROW_TILES):
        ref[pl.ds(s, rows, stride=ROW_TILES), :] = val[:, s * LANES:(s + 1) * LANES]


def _post_kernel(*refs, layer):
    if layer == 0:
        (a_ref, op_ref, os_ref, x_ref, g1_ref, sh2_ref, sc2_ref, w_ref, n2g_ref, rw_ref, rb_ref,
         xo_ref, xn_ref, eid_ref, gate_ref, cnt_ref) = refs
    else:
        (op_ref, os_ref, x_ref, g1_ref, sh2_ref, sc2_ref, w_ref, n2g_ref, rw_ref, rb_ref,
         xo_ref, xn_ref, eid_ref, gate_ref, cnt_ref) = refs
    i = pl.program_id(0)
    o = jnp.where(i < NP // TM, op_ref[...], os_ref[...])
    if layer == 0:
        mix = (jnp.dot(a_ref[...], w_ref[:CONV_CH, :], preferred_element_type=F32)
               + jnp.dot(o, w_ref[CONV_CH:, :], preferred_element_type=F32))
    else:
        mix = jnp.dot(o, w_ref[...], preferred_element_type=F32)
    x1 = x_ref[...] + g1_ref[0] * mix
    xo_ref[...] = x1
    xn = _ada_norm(x1, n2g_ref[...], sh2_ref[0], sc2_ref[0])
    _store_row_tiles(xn_ref, xn, TM)
    rw = rw_ref[...]
    xh = xn.astype(BF16)
    xl = (xn - xh.astype(F32)).astype(BF16)
    wh = rw.astype(BF16)
    wl = (rw - wh.astype(F32)).astype(BF16)
    logits = (jnp.dot(xh, wh, preferred_element_type=F32)
              + (jnp.dot(xl, wh, preferred_element_type=F32)
                 + jnp.dot(xh, wl, preferred_element_type=F32))) + rb_ref[...]
    lane = lax.broadcasted_iota(I32, logits.shape, 1)
    vals, idxs = [], []
    picked = jnp.zeros(logits.shape, F32)
    for _ in range(TOP_K):
        m = jnp.max(logits, axis=-1, keepdims=True)
        idx = jnp.min(jnp.where(logits == m, lane, N_EXPERTS), axis=-1, keepdims=True)
        vals.append(m)
        idxs.append(idx)
        hit = lane == idx
        picked = picked + hit.astype(F32)
        logits = jnp.where(hit, -jnp.inf, logits)
    es = [jnp.exp(v - vals[0]) for v in vals]
    den = es[0] + es[1] + es[2] + es[3]
    eid_ref[...] = jnp.concatenate(idxs, axis=1)
    gate_ref[...] = jnp.concatenate([e / den for e in es], axis=1)

    @pl.when(i == 0)
    def _():
        cnt_ref[...] = jnp.zeros_like(cnt_ref)

    cnt_ref[...] += jnp.sum(picked, axis=0, keepdims=True)


def _post(layer, mix_parts, x, mods, w_bf, n2g, rw, rb):
    npb = NP // TM
    if layer == 0:
        a, o_p, o_s = mix_parts
        wo = 512
        mix_specs = [pl.BlockSpec((TM, CONV_CH), lambda i: (i, 0))]
        mix_args = [a, o_p, o_s]
    else:
        o_p, o_s = mix_parts
        wo = D_MODEL
        mix_specs = []
        mix_args = [o_p, o_s]
    mix_specs += [pl.BlockSpec((TM, wo), lambda i: (jnp.minimum(i, npb - 1), 0)),
                  pl.BlockSpec((TM, wo), lambda i: (jnp.maximum(i - npb, 0), 0))]
    return pl.pallas_call(
        functools.partial(_post_kernel, layer=layer),
        grid=(N // TM,),
        in_specs=mix_specs + [pl.BlockSpec((TM, D_MODEL), lambda i: (i, 0)),
                              _mod_spec(2, TM), _mod_spec(3, TM), _mod_spec(4, TM),
                              _const_spec((D_MODEL, D_MODEL)), _const_spec((1, D_MODEL)),
                              _const_spec((D_MODEL, N_EXPERTS)), _const_spec((1, N_EXPERTS))],
        out_specs=(pl.BlockSpec((TM, D_MODEL), lambda i: (i, 0)),
                   pl.BlockSpec((TM * ROW_TILES, LANES), lambda i: (i, 0)),
                   pl.BlockSpec((TM, TOP_K), lambda i: (i, 0)),
                   pl.BlockSpec((TM, TOP_K), lambda i: (i, 0)),
                   _const_spec((1, N_EXPERTS))),
        out_shape=(jax.ShapeDtypeStruct((N, D_MODEL), F32),
                   jax.ShapeDtypeStruct((N * ROW_TILES, LANES), F32),
                   jax.ShapeDtypeStruct((N, TOP_K), I32),
                   jax.ShapeDtypeStruct((N, TOP_K), F32),
                   jax.ShapeDtypeStruct((1, N_EXPERTS), F32)),
        compiler_params=_cparams(("arbitrary",)),
        name="post_l%d" % layer,
    )(*mix_args, x, mods, mods, mods, w_bf, n2g.reshape(1, -1), rw, rb.reshape(1, -1))


def _slots_kernel(cnt_ref, bst_ref, eidt_ref, base_ref, triu_ref, slot_ref, carry, dvm,
                  dsm0, dsm1, dsm2, dsm3, sem):
    j = pl.program_id(0)

    @pl.when(j == 0)
    def _():
        carry[...] = jnp.zeros_like(carry)

    e_iota = lax.broadcasted_iota(I32, (N_EXPERTS, RT), 0)
    ohs = [(eidt_ref[k:k + 1, :] == e_iota).astype(F32) for k in range(TOP_K)]
    ohsum = ohs[0] + ohs[1] + ohs[2] + ohs[3]
    cum = jnp.dot(ohsum.astype(BF16), triu_ref[...], preferred_element_type=F32)
    tot = cum + carry[...] + base_ref[...]
    dvm[...] = jnp.concatenate(
        [jnp.sum(oh * tot, axis=0, keepdims=True) for oh in ohs], axis=0).astype(I32)
    carry[...] += jnp.sum(ohsum, axis=1, keepdims=True)

    dsm = (dsm0, dsm1, dsm2, dsm3)
    copies = [pltpu.make_async_copy(dvm.at[k], dsm[k], sem.at[k]) for k in range(TOP_K)]
    for cp in copies:
        cp.start()
    for cp in copies:
        cp.wait()

    def scatter(c, carry_):
        t0 = c * SCATTER_UNROLL
        a0 = (j * RT + t0) * TOP_K
        for u in range(SCATTER_UNROLL):
            for k in range(TOP_K):
                slot_ref[dsm[k][t0 + u]] = a0 + (u * TOP_K + k)
        return carry_

    lax.fori_loop(0, RT // SCATTER_UNROLL, scatter, 0)

    @pl.when(j == pl.num_programs(0) - 1)
    def _():
        def pad(s, c):
            slot_ref[s] = (N + s % SLOT_LEAD) * TOP_K + (TOP_K - 1)
            return c

        def per_expert(e, c):
            cnt = cnt_ref[e]
            b0 = bst_ref[e]
            lax.fori_loop(SLOT_LEAD + b0 * TME + cnt,
                          SLOT_LEAD + (b0 + (cnt + TME - 1) // TME) * TME, pad, 0)
            return c

        lax.fori_loop(0, SLOT_LEAD, pad, 0)
        lax.fori_loop(0, N_EXPERTS, per_expert, 0)
        lax.fori_loop(SLOT_LEAD + bst_ref[N_EXPERTS] * TME, N_SLOTS, pad, 0)


def _route(eid, counts_f):
    counts = counts_f.reshape(-1).astype(I32)
    nblk = (counts + TME - 1) // TME
    bend = jnp.cumsum(nblk)
    bstart = bend - nblk
    bst = jnp.concatenate([bstart, bend[-1:]]).astype(I32)
    base = (bstart * TME + SLOT_LEAD).astype(F32).reshape(N_EXPERTS, 1)
    ar = jnp.arange(RT, dtype=I32)
    triu = (ar[:, None] < ar[None, :]).astype(BF16)
    grid_spec = pltpu.PrefetchScalarGridSpec(
        num_scalar_prefetch=2,
        grid=(N // RT,),
        in_specs=[pl.BlockSpec((TOP_K, RT), lambda j, c, b: (0, j)),
                  _const_spec((N_EXPERTS, 1)),
                  _const_spec((RT, RT))],
        out_specs=pl.BlockSpec(memory_space=pltpu.SMEM),
        scratch_shapes=[pltpu.VMEM((N_EXPERTS, 1), F32),
                        pltpu.VMEM((TOP_K, RT), I32)]
        + [pltpu.SMEM((RT,), I32)] * TOP_K
        + [pltpu.SemaphoreType.DMA((TOP_K,))],
    )
    slot_asg = pl.pallas_call(
        _slots_kernel,
        grid_spec=grid_spec,
        out_shape=jax.ShapeDtypeStruct((N_SLOTS,), I32),
        compiler_params=_cparams(("arbitrary",)),
        name="slots",
    )(counts, bst, eid.T, base, triu)
    return slot_asg, bst, counts


def _row_tile(idx):
    if isinstance(idx, int):
        return pl.ds(idx * ROW_TILES, ROW_TILES)
    return pl.ds(pl.multiple_of(idx * ROW_TILES, ROW_TILES), ROW_TILES)


def _expert_kernel(slot_ref, bst_ref, cnt_ref, xn_hbm, upw_ref, upb_ref, dww_ref, dwb_ref,
                   y_hbm, gbuf0, gbuf1, sbuf0, sbuf1, upbf, dwbf, gsem, ssem):
    e = pl.program_id(0)
    nb = (cnt_ref[e] + TME - 1) // TME
    b0 = bst_ref[e]
    gbufs = (gbuf0, gbuf1)
    sbufs = (sbuf0, sbuf1)

    def gather_row(g, buf, r):
        v = slot_ref[(g + 2) * TME + r]
        tok = jnp.minimum(lax.shift_right_logical(v, TOP_K_SHIFT), N - 1)
        pltpu.make_async_copy(xn_hbm.at[_row_tile(tok), :], gbufs[buf].at[_row_tile(r), :],
                              gsem.at[buf]).start()

    def scatter_row(g, buf, r):
        v = slot_ref[(g + 2) * TME + r]
        dst = (v & (TOP_K - 1)) * N + lax.shift_right_logical(v, TOP_K_SHIFT)
        pltpu.make_async_copy(sbufs[buf].at[_row_tile(r), :], y_hbm.at[_row_tile(dst), :],
                              ssem.at[buf]).start()

    def rolled(row_fn, g, buf):
        def body(r, c):
            row_fn(g, buf, r)
            return c
        lax.fori_loop(0, TME, body, 0)

    def wait_gather(buf):
        pltpu.make_async_copy(xn_hbm.at[pl.ds(0, TME * ROW_TILES), :], gbufs[buf], gsem.at[buf]).wait()

    def wait_scatter(buf):
        pltpu.make_async_copy(sbufs[buf], y_hbm.at[pl.ds(0, TME * ROW_TILES), :], ssem.at[buf]).wait()

    @pl.when(e == 0)
    def _():
        sbuf0[...] = jnp.zeros_like(sbuf0)
        sbuf1[...] = jnp.zeros_like(sbuf1)
        rolled(gather_row, 0, 0)
        rolled(scatter_row, -2, 0)

    @pl.when(nb > 0)
    def _():
        upbf[...] = upw_ref[0].astype(BF16)
        dwbf[...] = dww_ref[0].astype(BF16)

    def work(g, cur):
        oth = 1 - cur
        wait_gather(cur)
        xb = jnp.concatenate(
            [gbufs[cur][pl.ds(s, TME, stride=ROW_TILES), :] for s in range(ROW_TILES)], axis=1).astype(BF16)
        for r in range(TME):
            gather_row(g + 1, oth, r)
        for r in range(TME):
            scatter_row(g - 1, oth, r)
        gu = jnp.dot(xb, upbf[...], preferred_element_type=F32) + upb_ref[0]
        gg = jnp.minimum(gu[:, :D_FF], SWIGLU_LIMIT)
        lin = jnp.clip(gu[:, D_FF:], -SWIGLU_LIMIT, SWIGLU_LIMIT)
        hdn = gg * jax.nn.sigmoid(SWIGLU_ALPHA * gg) * (lin + 1.0)
        y = jnp.dot(hdn.astype(BF16), dwbf[...], preferred_element_type=F32) + dwb_ref[0]
        wait_scatter(cur)
        for s in range(ROW_TILES):
            sbufs[cur][pl.ds(s, TME, stride=ROW_TILES), :] = y[:, s * LANES:(s + 1) * LANES]

    def block(j, carry):
        g = b0 + j
        for parity in range(2):
            @pl.when(g % 2 == parity)
            def _():
                work(g, parity)
        return carry

    lax.fori_loop(0, nb, block, 0)

    @pl.when(e == pl.num_programs(0) - 1)
    def _():
        g_end = bst_ref[N_EXPERTS]
        for parity in range(2):
            @pl.when(g_end % 2 == parity)
            def _():
                rolled(scatter_row, g_end - 1, 1 - parity)
                wait_gather(parity)
        wait_scatter(0)
        wait_scatter(1)


def _experts(slot_asg, bst, counts, xn_tiles, up_w, up_b, down_w, down_b):
    grid_spec = pltpu.PrefetchScalarGridSpec(
        num_scalar_prefetch=3,
        grid=(N_EXPERTS,),
        in_specs=[pl.BlockSpec(memory_space=pl.ANY),
                  pl.BlockSpec((1, D_MODEL, 2 * D_FF), lambda e, *_: (e, 0, 0)),
                  pl.BlockSpec((1, 1, 2 * D_FF), lambda e, *_: (e, 0, 0)),
                  pl.BlockSpec((1, D_FF, D_MODEL), lambda e, *_: (e, 0, 0)),
                  pl.BlockSpec((1, 1, D_MODEL), lambda e, *_: (e, 0, 0))],
        out_specs=pl.BlockSpec(memory_space=pl.ANY),
        scratch_shapes=[pltpu.VMEM((TME * ROW_TILES, LANES), F32),
                        pltpu.VMEM((TME * ROW_TILES, LANES), F32),
                        pltpu.VMEM((TME * ROW_TILES, LANES), F32),
                        pltpu.VMEM((TME * ROW_TILES, LANES), F32),
                        pltpu.VMEM((D_MODEL, 2 * D_FF), BF16),
                        pltpu.VMEM((D_FF, D_MODEL), BF16),
                        pltpu.SemaphoreType.DMA((2,)),
                        pltpu.SemaphoreType.DMA((2,))],
    )
    return pl.pallas_call(
        _expert_kernel,
        grid_spec=grid_spec,
        out_shape=jax.ShapeDtypeStruct(((N_ASG + SLOT_LEAD) * ROW_TILES, LANES), F32),
        compiler_params=_cparams(("arbitrary",)),
        name="experts",
    )(slot_asg, bst, counts, xn_tiles, up_w, up_b.reshape(N_EXPERTS, 1, -1),
      down_w, down_b.reshape(N_EXPERTS, 1, -1))


def _combine_kernel(x_ref, gate_ref, g2_ref, fg_ref, y0_ref, y1_ref, y2_ref, y3_ref, o_ref, *, final):
    gate = gate_ref[...]
    cols = []
    for s in range(ROW_TILES):
        acc = None
        for k, y_ref in enumerate((y0_ref, y1_ref, y2_ref, y3_ref)):
            term = gate[:, k:k + 1] * y_ref[pl.ds(s, TMC, stride=ROW_TILES), :]
            acc = term if acc is None else acc + term
        cols.append(acc)
    x2 = x_ref[...] + g2_ref[0] * jnp.concatenate(cols, axis=1)
    if final:
        ms = jnp.mean(x2 * x2, axis=-1, keepdims=True)
        x2 = x2 * lax.rsqrt(ms + EPS) * fg_ref[...]
    o_ref[...] = x2


def _combine(x, gates, mods, final_g, y_tiles, final):
    nblk = N // TMC

    def y_spec(k):
        return pl.BlockSpec((TMC * ROW_TILES, LANES), lambda i: (k * nblk + i, 0))

    return pl.pallas_call(
        functools.partial(_combine_kernel, final=final),
        grid=(nblk,),
        in_specs=[pl.BlockSpec((TMC, D_MODEL), lambda i: (i, 0)),
                  pl.BlockSpec((TMC, TOP_K), lambda i: (i, 0)),
                  _mod_spec(5, TMC),
                  _const_spec((1, D_MODEL)),
                  y_spec(0), y_spec(1), y_spec(2), y_spec(3)],
        out_specs=pl.BlockSpec((TMC, D_MODEL), lambda i: (i, 0)),
        out_shape=jax.ShapeDtypeStruct((N, D_MODEL), F32),
        compiler_params=_cparams(("arbitrary",)),
        name="combine",
    )(x, gates, mods, final_g.reshape(1, -1), y_tiles, y_tiles, y_tiles, y_tiles)


def _rope_tables():
    pos = jnp.arange(DEC_SEQ)
    r = (pos // GRID_W).astype(F32)
    col = (pos % GRID_W).astype(F32)
    quarter = HEAD_DIM // 4
    inv = ROPE_BASE ** (-jnp.arange(quarter, dtype=F32) / quarter)
    ar = r[:, None] * inv
    ac = col[:, None] * inv
    ang = jnp.concatenate([ar, ar, ac, ac], axis=-1)
    cos = jnp.tile(jnp.cos(ang), (1, LANES // HEAD_DIM))
    sin = jnp.tile(jnp.sin(ang), (1, LANES // HEAD_DIM))
    first = (jnp.arange(LANES) % 32) < 16
    sa = jnp.where(first[None, :], -sin, 0.0)
    sb = jnp.where(first[None, :], 0.0, sin)
    return cos, sa, sb


def _moe(x, xn, eid, gates, counts, mods, final_g, up_w, up_b, down_w, down_b, final):
    slot_asg, bst, counts_i = _route(eid, counts)
    y = _experts(slot_asg, bst, counts_i, xn, up_w, up_b, down_w, down_b)
    return _combine(x, gates, mods, final_g, y, final)


@jax.jit
def kernel(x_prompt, x_sample, cache_l0_k, cache_l0_v, cache_l1_k, cache_l1_v, c, c_ctx, final_g,
           l0_mod_w, l0_mod_b, l0_norm1_g, l0_w_in, l0_conv_w, l0_conv_b, l0_cnorm_g, l0_cnorm_b,
           l0_sink, l0_w_out, l0_norm2_g, l0_router_w, l0_router_b, l0_up_w, l0_up_b, l0_down_w,
           l0_down_b,
           l1_mod_w, l1_mod_b, l1_norm1_g, l1_w_in, l1_lam_q1, l1_lam_k1, l1_lam_q2, l1_lam_k2,
           l1_subln_g, l1_w_out, l1_norm2_g, l1_router_w, l1_router_b, l1_up_w, l1_up_b, l1_down_w,
           l1_down_b):
    x = jnp.concatenate([x_prompt.reshape(NP, D_MODEL), x_sample.reshape(NS, D_MODEL)], axis=0)
    cond16 = jnp.concatenate([c, c_ctx[None, :], jnp.zeros((16 - DEC_BATCH - 1, D_MODEL), F32)], axis=0)
    rope_tabs = _rope_tables()

    mods = _modulation(cond16, l0_mod_w, l0_mod_b)
    ha, hq, st0 = _inproj(x, mods, l0_norm1_g, l0_w_in.astype(BF16), rope_tabs, 0)
    a = _conv(ha, l0_conv_w, l0_conv_b, l0_cnorm_g, l0_cnorm_b)
    o_p = _l0_prompt_attn(hq, l0_sink)
    o_s = _l0_window_attn(hq, l0_sink, cache_l0_k.reshape(DEC_BATCH, PAST_LEN, LANES),
                          cache_l0_v.reshape(DEC_BATCH, PAST_LEN, LANES))
    x, xn, eid, gates, counts = _post(0, (a, o_p, o_s), x, mods, l0_w_out.astype(BF16), l0_norm2_g,
                                      l0_router_w, l0_router_b)
    x = _moe(x, xn, eid, gates, counts, mods, final_g, l0_up_w, l0_up_b, l0_down_w, l0_down_b, False)
    state_l0_k = st0[:, :LANES].reshape(BATCH, SEQ, WIN_KV, HEAD_DIM)
    state_l0_v = st0[:, LANES:].reshape(BATCH, SEQ, WIN_KV, HEAD_DIM)

    lam_init = 0.8 - 0.6 * math.exp(-0.3 * 1)
    mods = _modulation(cond16, l1_mod_w, l1_mod_b)
    h1, st1 = _inproj(x, mods, l1_norm1_g, l1_w_in.astype(BF16), rope_tabs, 1)
    lam_vecs = jnp.stack([l1_lam_q1, l1_lam_k1, l1_lam_q2, l1_lam_k2], axis=0)
    o_p = _diff_attn(h1, lam_vecs, l1_subln_g, lam_init, None)
    o_s = _diff_attn(h1, lam_vecs, l1_subln_g, lam_init,
                     (cache_l1_k.reshape(DEC_BATCH, PAST_LEN, D_MODEL),
                      cache_l1_v.reshape(DEC_BATCH, PAST_LEN, D_MODEL)))
    x, xn, eid, gates, counts = _post(1, (o_p, o_s), x, mods, l1_w_out.astype(BF16), l1_norm2_g,
                                      l1_router_w, l1_router_b)
    x = _moe(x, xn, eid, gates, counts, mods, final_g, l1_up_w, l1_up_b, l1_down_w, l1_down_b, True)
    state_l1_k = st1[:, :D_MODEL].reshape(BATCH, SEQ, DIFF_HEADS, 2, HEAD_DIM)
    state_l1_v = st1[:, D_MODEL:].reshape(BATCH, SEQ, DIFF_HEADS, DIFF_V)

    y_prompt = x[:NP].reshape(BATCH, SEQ, D_MODEL)
    y_sample = x[NP:].reshape(DEC_BATCH, DEC_SEQ, D_MODEL)
    return (y_prompt, y_sample, state_l0_k, state_l0_v, state_l1_k, state_l1_v)
```

```python
import functools
import math

import jax
import jax.numpy as jnp
from jax import lax
from jax.experimental import pallas as pl
from jax.experimental.pallas import tpu as pltpu

F32 = jnp.float32
BF16 = jnp.bfloat16
I32 = jnp.int32

D_MODEL = 1024
BATCH = 16
SEQ = 256
DEC_BATCH = 8
DEC_SEQ = 2048
PAST_LEN = 256
GRID_W = 64
HEAD_DIM = 64
BLOCK = 128
WINDOW = 128
ROPE_BASE = 10000.0
EPS = 1e-6
NEG = -1e30
ATTN_SCALE = HEAD_DIM ** -0.5
LOG2E = math.log2(math.e)
CONV_CH = 512
CONV_W = 31
WIN_HEADS = 8
WIN_KV = 2
WIN_G = 4
IN0_W = 1792
DIFF_HEADS = 8
DIFF_V = 128
IN1_W = 3072
N_EXPERTS = 32
TOP_K = 4
TOP_K_SHIFT = 2
D_FF = 1024
SWIGLU_LIMIT = 7.0
SWIGLU_ALPHA = 1.702

NP = BATCH * SEQ
NS = DEC_BATCH * DEC_SEQ
N = NP + NS
N_ASG = N * TOP_K

LANES = 128
SUBLANES = 8
ROW_TILES = D_MODEL // LANES
VMEM_LIMIT = 56 * 1024 * 1024

TM = 512
TMC = 512
TME = 256
CB = 256
HALO = 16
SHIFT_ROWS = CB + 2 * HALO - SUBLANES
TQ = 256
RT = 512
SCATTER_UNROLL = 64
MAX_EBLOCKS = (N_ASG + N_EXPERTS * (TME - 1)) // TME
SLOT_LEAD = 2 * TME
N_SLOTS = (MAX_EBLOCKS + 3) * TME

_NT = (((1,), (1,)), ((), ()))


def _cparams(sem):
    return pltpu.CompilerParams(dimension_semantics=sem, vmem_limit_bytes=VMEM_LIMIT)


def _mod_row(i, tm):
    npb = NP // tm
    return jnp.where(i < npb, DEC_BATCH, (i - npb) // (DEC_SEQ // tm))


def _mod_spec(j, tm):
    return pl.BlockSpec((1, 1, D_MODEL), lambda i, *_: (_mod_row(i, tm) * 6 + j, 0, 0))


def _const_spec(shape):
    nd = len(shape)
    return pl.BlockSpec(shape, lambda *_: (0,) * nd)


def _ada_norm(x, g, shift, scale):
    ms = jnp.mean(x * x, axis=-1, keepdims=True)
    return (x * lax.rsqrt(ms + EPS) * g) * (1.0 + scale) + shift


def _mod_kernel(c_ref, w_ref, b_ref, o_ref):
    c = c_ref[...]
    s = c * jax.nn.sigmoid(c)
    o_ref[...] = jnp.dot(s.astype(BF16), w_ref[...].astype(BF16), preferred_element_type=F32) + b_ref[...]


def _modulation(cond16, w, b):
    m = pl.pallas_call(
        _mod_kernel,
        grid=(6,),
        in_specs=[_const_spec((16, D_MODEL)),
                  pl.BlockSpec((D_MODEL, D_MODEL), lambda j: (0, j)),
                  pl.BlockSpec((1, D_MODEL), lambda j: (0, j))],
        out_specs=pl.BlockSpec((16, D_MODEL), lambda j: (0, j)),
        out_shape=jax.ShapeDtypeStruct((16, 6 * D_MODEL), F32),
        compiler_params=_cparams(("arbitrary",)),
        name="modulation",
    )(cond16, w, b.reshape(1, -1))
    return m.reshape(16 * 6, 1, D_MODEL)


def _rope(v, cos, sa, sb):
    return v * cos + pltpu.roll(v, LANES - 16, 1) * sa + pltpu.roll(v, 16, 1) * sb


def _inproj_kernel(x_ref, sh_ref, sc_ref, g_ref, w_ref, cos_ref, sa_ref, sb_ref, *outs, layer):
    i = pl.program_id(0)
    h = _ada_norm(x_ref[...], g_ref[...], sh_ref[0], sc_ref[0])
    acc = jnp.dot(h.astype(BF16), w_ref[...], preferred_element_type=F32)
    if layer == 0:
        ha_ref, hq_ref, st_ref = outs
        ha_ref[...] = acc[:, :2 * CONV_CH]
        base, n_q, n_rope, n_all, st_lo = 2 * CONV_CH, 4, 5, 6, 2 * CONV_CH + 512
        q_scale = ATTN_SCALE
    else:
        hq_ref, st_ref = outs
        base, n_q, n_rope, n_all, st_lo = 0, 8, 16, 24, 1024
        q_scale = ATTN_SCALE * LOG2E
    is_prompt = i < NP // TM

    def chunk(c):
        v = acc[:, base + c * LANES: base + (c + 1) * LANES]
        return v * q_scale if c < n_q else v

    @pl.when(is_prompt)
    def _():
        for c in range(n_all):
            hq_ref[:, c * LANES:(c + 1) * LANES] = chunk(c).astype(BF16)
        st_ref[...] = acc[:, st_lo:]

    @pl.when(jnp.logical_not(is_prompt))
    def _():
        cos, sa, sb = cos_ref[...], sa_ref[...], sb_ref[...]
        for c in range(n_all):
            v = chunk(c)
            if c < n_rope:
                v = _rope(v, cos, sa, sb)
            hq_ref[:, c * LANES:(c + 1) * LANES] = v.astype(BF16)


def _inproj(x, mods, g, w_bf, rope_tabs, layer):
    npb = NP // TM
    spb = DEC_SEQ // TM
    nout = w_bf.shape[1]
    rope_spec = pl.BlockSpec((TM, LANES), lambda i: (jnp.where(i < npb, 0, (i - npb) % spb), 0))
    st_w = 256 if layer == 0 else 2048
    st_spec = pl.BlockSpec((TM, st_w), lambda i: (jnp.minimum(i, npb - 1), 0))
    if layer == 0:
        out_shape = (jax.ShapeDtypeStruct((N, 2 * CONV_CH), F32),
                     jax.ShapeDtypeStruct((N, 768), BF16),
                     jax.ShapeDtypeStruct((NP, st_w), F32))
        out_specs = (pl.BlockSpec((TM, 2 * CONV_CH), lambda i: (i, 0)),
                     pl.BlockSpec((TM, 768), lambda i: (i, 0)), st_spec)
    else:
        out_shape = (jax.ShapeDtypeStruct((N, IN1_W), BF16),
                     jax.ShapeDtypeStruct((NP, st_w), F32))
        out_specs = (pl.BlockSpec((TM, IN1_W), lambda i: (i, 0)), st_spec)
    return pl.pallas_call(
        functools.partial(_inproj_kernel, layer=layer),
        grid=(N // TM,),
        in_specs=[pl.BlockSpec((TM, D_MODEL), lambda i: (i, 0)),
                  _mod_spec(0, TM), _mod_spec(1, TM),
                  _const_spec((1, D_MODEL)),
                  _const_spec((D_MODEL, nout)),
                  rope_spec, rope_spec, rope_spec],
        out_specs=out_specs,
        out_shape=out_shape,
        compiler_params=_cparams(("arbitrary",)),
        name="inproj_l%d" % layer,
    )(x, mods, mods, g.reshape(1, -1), w_bf, *rope_tabs)


def _conv_kernel(prev_ref, cur_ref, next_ref, w_ref, b_ref, g_ref, bb_ref, o_ref, upad, ush):
    i = pl.program_id(0)
    npb = NP // CB
    spb = DEC_SEQ // CB
    j = (i - npb) % spb
    first = jnp.logical_or(i < npb, j == 0)
    last = jnp.logical_or(i < npb, j == spb - 1)

    def glu(r):
        return r[:, :CONV_CH] * jax.nn.sigmoid(r[:, CONV_CH:])

    upad[HALO:HALO + CB, :] = glu(cur_ref[...])
    upad[0:HALO, :] = jnp.where(first, 0.0, glu(prev_ref[...]))
    upad[HALO + CB:, :] = jnp.where(last, 0.0, glu(next_ref[...]))
    acc = jnp.zeros((CB, CONV_CH), F32)
    off = HALO - CONV_W // 2
    for b in range(SUBLANES):
        taps = [t for t in range(CONV_W) if (off + t) % SUBLANES == b]
        if b > 0:
            ush[b - 1] = upad[b:b + SHIFT_ROWS, :]
        for t in taps:
            a8 = off + t - b
            win = upad[a8:a8 + CB, :] if b == 0 else ush[b - 1, a8:a8 + CB, :]
            acc = acc + w_ref[t:t + 1, :] * win
    u = acc + b_ref[...]
    mu = jnp.mean(u, axis=-1, keepdims=True)
    var = jnp.mean(jnp.square(u - mu), axis=-1, keepdims=True)
    y = (u - mu) * lax.rsqrt(var + EPS) * g_ref[...] + bb_ref[...]
    o_ref[...] = (y * jax.nn.sigmoid(y)).astype(BF16)


def _conv(ha, conv_w, conv_b, cg, cb):
    hb = CB // HALO
    nh = N // HALO
    w_pad = jnp.concatenate([conv_w, jnp.zeros((1, CONV_CH), F32)], axis=0)
    return pl.pallas_call(
        _conv_kernel,
        grid=(N // CB,),
        in_specs=[pl.BlockSpec((HALO, 2 * CONV_CH), lambda i: (jnp.maximum(i * hb - 1, 0), 0)),
                  pl.BlockSpec((CB, 2 * CONV_CH), lambda i: (i, 0)),
                  pl.BlockSpec((HALO, 2 * CONV_CH), lambda i: (jnp.minimum((i + 1) * hb, nh - 1), 0)),
                  _const_spec((CONV_W + 1, CONV_CH)),
                  _const_spec((1, CONV_CH)), _const_spec((1, CONV_CH)), _const_spec((1, CONV_CH))],
        out_specs=pl.BlockSpec((CB, CONV_CH), lambda i: (i, 0)),
        out_shape=jax.ShapeDtypeStruct((N, CONV_CH), BF16),
        scratch_shapes=[pltpu.VMEM((CB + 2 * HALO, CONV_CH), F32),
                        pltpu.VMEM((SUBLANES - 1, SHIFT_ROWS, CONV_CH), F32)],
        compiler_params=_cparams(("arbitrary",)),
        name="conformer_conv",
    )(ha, ha, ha, w_pad, conv_b.reshape(1, -1), cg.reshape(1, -1), cb.reshape(1, -1))


def _sink_attend(q, k, v, sink, mask):
    s = lax.dot_general(q, k, _NT, preferred_element_type=F32)
    if mask is not None:
        s = jnp.where(mask, s, NEG)
    m = jnp.maximum(jnp.max(s, axis=-1, keepdims=True), sink)
    p = jnp.exp(s - m)
    den = jnp.sum(p, axis=-1, keepdims=True) + jnp.exp(sink - m)
    return jnp.dot(p.astype(BF16), v, preferred_element_type=F32) / den


def _l0_prompt_attn_kernel(sink_ref, q_ref, k_ref, v_ref, o_ref):
    k = k_ref[...]
    v = v_ref[...]
    outs = []
    for h in range(WIN_HEADS):
        j = h // WIN_G
        outs.append(_sink_attend(q_ref[:, h * HEAD_DIM:(h + 1) * HEAD_DIM],
                                 k[:, j * HEAD_DIM:(j + 1) * HEAD_DIM],
                                 v[:, j * HEAD_DIM:(j + 1) * HEAD_DIM], sink_ref[h], None))
    o_ref[...] = jnp.concatenate(outs, axis=1).astype(BF16)


def _l0_prompt_attn(hq, sink):
    return pl.pallas_call(
        _l0_prompt_attn_kernel,
        grid=(BATCH,),
        in_specs=[pl.BlockSpec(memory_space=pltpu.SMEM),
                  pl.BlockSpec((SEQ, 512), lambda b: (b, 0)),
                  pl.BlockSpec((SEQ, LANES), lambda b: (b, 4)),
                  pl.BlockSpec((SEQ, LANES), lambda b: (b, 5))],
        out_specs=pl.BlockSpec((SEQ, 512), lambda b: (b, 0)),
        out_shape=jax.ShapeDtypeStruct((NP, 512), BF16),
        compiler_params=_cparams(("arbitrary",)),
        name="l0_prompt_attn",
    )(sink, hq, hq, hq)


def _l0_window_attn_kernel(sink_ref, q_ref, kp_ref, kc_ref, kn_ref, vp_ref, vc_ref, vn_ref,
                           ck_ref, cv_ref, o_ref):
    n = pl.program_id(1)
    k = jnp.concatenate([kp_ref[...], kc_ref[...], kn_ref[...], ck_ref[...].astype(BF16)], axis=0)
    v = jnp.concatenate([vp_ref[...], vc_ref[...], vn_ref[...], cv_ref[...].astype(BF16)], axis=0)
    qpos = n * BLOCK + lax.broadcasted_iota(I32, (BLOCK, 3 * BLOCK + PAST_LEN), 0)
    col = lax.broadcasted_iota(I32, (BLOCK, 3 * BLOCK + PAST_LEN), 1)
    kpos = (n - 1) * BLOCK + col
    local_ok = (jnp.abs(kpos - qpos) <= WINDOW) & (kpos >= 0) & (kpos < DEC_SEQ)
    ok = jnp.logical_or(col >= 3 * BLOCK, local_ok).astype(F32)
    mask = jnp.concatenate([ok] * WIN_G, axis=0) > 0.5
    grp = lax.broadcasted_iota(I32, (WIN_G * BLOCK, 1), 0) // BLOCK
    outs = []
    for j in range(WIN_KV):
        heads = range(j * WIN_G, (j + 1) * WIN_G)
        q4 = jnp.concatenate([q_ref[:, h * HEAD_DIM:(h + 1) * HEAD_DIM] for h in heads], axis=0)
        sink4 = jnp.zeros((WIN_G * BLOCK, 1), F32)
        for i, h in enumerate(heads):
            sink4 = jnp.where(grp == i, sink_ref[h], sink4)
        o4 = _sink_attend(q4, k[:, j * HEAD_DIM:(j + 1) * HEAD_DIM],
                          v[:, j * HEAD_DIM:(j + 1) * HEAD_DIM], sink4, mask)
        outs += [o4[i * BLOCK:(i + 1) * BLOCK, :] for i in range(WIN_G)]
    o_ref[...] = jnp.concatenate(outs, axis=1).astype(BF16)


def _l0_window_attn(hq, sink, ck, cv):
    nb = DEC_SEQ // BLOCK
    r0 = NP // BLOCK

    def kv_spec(col, d):
        return pl.BlockSpec((BLOCK, LANES),
                            lambda b, n: (r0 + b * nb + jnp.clip(n + d, 0, nb - 1), col))

    ctx_spec = pl.BlockSpec((None, PAST_LEN, LANES), lambda b, n: (b, 0, 0))
    return pl.pallas_call(
        _l0_window_attn_kernel,
        grid=(DEC_BATCH, nb),
        in_specs=[pl.BlockSpec(memory_space=pltpu.SMEM),
                  pl.BlockSpec((BLOCK, 512), lambda b, n: (r0 + b * nb + n, 0)),
                  kv_spec(4, -1), kv_spec(4, 0), kv_spec(4, 1),
                  kv_spec(5, -1), kv_spec(5, 0), kv_spec(5, 1),
                  ctx_spec, ctx_spec],
        out_specs=pl.BlockSpec((BLOCK, 512), lambda b, n: (b * nb + n, 0)),
        out_shape=jax.ShapeDtypeStruct((NS, 512), BF16),
        compiler_params=_cparams(("arbitrary", "arbitrary")),
        name="l0_window_attn",
    )(sink, hq, hq, hq, hq, hq, hq, hq, ck, cv)


def _diff_attn_kernel(*refs, lam_init, has_ctx):
    if has_ctx:
        q_ref, k_ref, v_ref, ck_ref, cv_ref, lam_ref, g_ref, o_ref = refs
    else:
        q_ref, k_ref, v_ref, lam_ref, g_ref, o_ref = refs
    lv = lam_ref[...]
    lam = (jnp.exp(jnp.sum(lv[0:1] * lv[1:2], axis=-1, keepdims=True))
           - jnp.exp(jnp.sum(lv[2:3] * lv[3:4], axis=-1, keepdims=True)) + lam_init)
    g = g_ref[...]
    outs = []
    for h in range(DIFF_HEADS):
        probs = []
        for c in range(2):
            lo = h * DIFF_V + c * HEAD_DIM
            q = q_ref[:, lo:lo + HEAD_DIM]
            s = lax.dot_general(q, k_ref[:, lo:lo + HEAD_DIM], _NT, preferred_element_type=F32)
            m = jnp.max(s, axis=-1, keepdims=True)
            if has_ctx:
                sc = lax.dot_general(q, ck_ref[:, lo:lo + HEAD_DIM].astype(BF16), _NT,
                                     preferred_element_type=F32)
                m = jnp.maximum(m, jnp.max(sc, axis=-1, keepdims=True))
                pc = jnp.exp2(sc - m)
            p = jnp.exp2(s - m)
            den = jnp.sum(p, axis=-1, keepdims=True)
            if has_ctx:
                den = den + jnp.sum(pc, axis=-1, keepdims=True)
                probs.append((p, pc, den))
            else:
                probs.append((p, None, den))
        (p0, pc0, d0), (p1, pc1, d1) = probs
        ratio = lam * d0 / d1
        a = (p0 - p1 * ratio).astype(BF16)
        o = jnp.dot(a, v_ref[:, h * DIFF_V:(h + 1) * DIFF_V], preferred_element_type=F32)
        if has_ctx:
            ac = (pc0 - pc1 * ratio).astype(BF16)
            o = o + jnp.dot(ac, cv_ref[:, h * DIFF_V:(h + 1) * DIFF_V].astype(BF16),
                            preferred_element_type=F32)
        o = o / d0
        ms = jnp.mean(o * o, axis=-1, keepdims=True)
        outs.append(((o * lax.rsqrt(ms + EPS)) * g) * (1.0 - lam_init))
    o_ref[...] = jnp.concatenate(outs, axis=1).astype(BF16)


def _diff_attn(h1, lam_vecs, subln_g, lam_init, ctx):
    g = subln_g.reshape(1, DIFF_V)
    small = [_const_spec((4, HEAD_DIM)), _const_spec((1, DIFF_V))]
    if ctx is None:
        grid = (BATCH, 1)
        in_specs = [pl.BlockSpec((SEQ, D_MODEL), lambda b, i: (b, 0)),
                    pl.BlockSpec((SEQ, D_MODEL), lambda b, i: (b, 1)),
                    pl.BlockSpec((SEQ, D_MODEL), lambda b, i: (b, 2))] + small
        out_specs = pl.BlockSpec((SEQ, D_MODEL), lambda b, i: (b, 0))
        rows = NP
        args = (h1, h1, h1, lam_vecs, g)
        name = "diff_attn_prompt"
    else:
        nq = DEC_SEQ // TQ
        q0 = NP // TQ
        s0 = NP // DEC_SEQ
        ctx_spec = pl.BlockSpec((None, PAST_LEN, D_MODEL), lambda b, i: (b, 0, 0))
        grid = (DEC_BATCH, nq)
        in_specs = [pl.BlockSpec((TQ, D_MODEL), lambda b, i: (q0 + b * nq + i, 0)),
                    pl.BlockSpec((DEC_SEQ, D_MODEL), lambda b, i: (s0 + b, 1)),
                    pl.BlockSpec((DEC_SEQ, D_MODEL), lambda b, i: (s0 + b, 2)),
                    ctx_spec, ctx_spec] + small
        out_specs = pl.BlockSpec((TQ, D_MODEL), lambda b, i: (b * nq + i, 0))
        rows = NS
        args = (h1, h1, h1, ctx[0], ctx[1], lam_vecs, g)
        name = "diff_attn_sample"
    return pl.pallas_call(
        functools.partial(_diff_attn_kernel, lam_init=lam_init, has_ctx=ctx is not None),
        grid=grid, in_specs=in_specs, out_specs=out_specs,
        out_shape=jax.ShapeDtypeStruct((rows, D_MODEL), BF16),
        compiler_params=_cparams(("arbitrary", "arbitrary")),
        name=name,
    )(*args)


def _store_row_tiles(ref, val, rows):
    for s in range(ROW_TILES):
        ref[pl.ds(s, rows, stride=ROW_TILES), :] = val[:, s * LANES:(s + 1) * LANES]


def _post_kernel(*refs, layer):
    if layer == 0:
        (a_ref, op_ref, os_ref, x_ref, g1_ref, sh2_ref, sc2_ref, w_ref, n2g_ref, rw_ref, rb_ref,
         xo_ref, xn_ref, eid_ref, gate_ref, cnt_ref) = refs
    else:
        (op_ref, os_ref, x_ref, g1_ref, sh2_ref, sc2_ref, w_ref, n2g_ref, rw_ref, rb_ref,
         xo_ref, xn_ref, eid_ref, gate_ref, cnt_ref) = refs
    i = pl.program_id(0)
    o = jnp.where(i < NP // TM, op_ref[...], os_ref[...])
    if layer == 0:
        mix = (jnp.dot(a_ref[...], w_ref[:CONV_CH, :], preferred_element_type=F32)
               + jnp.dot(o, w_ref[CONV_CH:, :], preferred_element_type=F32))
    else:
        mix = jnp.dot(o, w_ref[...], preferred_element_type=F32)
    x1 = x_ref[...] + g1_ref[0] * mix
    xo_ref[...] = x1
    xn = _ada_norm(x1, n2g_ref[...], sh2_ref[0], sc2_ref[0])
    _store_row_tiles(xn_ref, xn, TM)
    rw = rw_ref[...]
    xh = xn.astype(BF16)
    xl = (xn - xh.astype(F32)).astype(BF16)
    wh = rw.astype(BF16)
    wl = (rw - wh.astype(F32)).astype(BF16)
    logits = (jnp.dot(xh, wh, preferred_element_type=F32)
              + (jnp.dot(xl, wh, preferred_element_type=F32)
                 + jnp.dot(xh, wl, preferred_element_type=F32))) + rb_ref[...]
    lane = lax.broadcasted_iota(I32, logits.shape, 1)
    vals, idxs = [], []
    picked = jnp.zeros(logits.shape, F32)
    for _ in range(TOP_K):
        m = jnp.max(logits, axis=-1, keepdims=True)
        idx = jnp.min(jnp.where(logits == m, lane, N_EXPERTS), axis=-1, keepdims=True)
        vals.append(m)
        idxs.append(idx)
        hit = lane == idx
        picked = picked + hit.astype(F32)
        logits = jnp.where(hit, -jnp.inf, logits)
    es = [jnp.exp(v - vals[0]) for v in vals]
    den = es[0] + es[1] + es[2] + es[3]
    eid_ref[...] = jnp.concatenate(idxs, axis=1)
    gate_ref[...] = jnp.concatenate([e / den for e in es], axis=1)

    @pl.when(i == 0)
    def _():
        cnt_ref[...] = jnp.zeros_like(cnt_ref)

    cnt_ref[...] += jnp.sum(picked, axis=0, keepdims=True)


def _post(layer, mix_parts, x, mods, w_bf, n2g, rw, rb):
    npb = NP // TM
    if layer == 0:
        a, o_p, o_s = mix_parts
        wo = 512
        mix_specs = [pl.BlockSpec((TM, CONV_CH), lambda i: (i, 0))]
        mix_args = [a, o_p, o_s]
    else:
        o_p, o_s = mix_parts
        wo = D_MODEL
        mix_specs = []
        mix_args = [o_p, o_s]
    mix_specs += [pl.BlockSpec((TM, wo), lambda i: (jnp.minimum(i, npb - 1), 0)),
                  pl.BlockSpec((TM, wo), lambda i: (jnp.maximum(i - npb, 0), 0))]
    return pl.pallas_call(
        functools.partial(_post_kernel, layer=layer),
        grid=(N // TM,),
        in_specs=mix_specs + [pl.BlockSpec((TM, D_MODEL), lambda i: (i, 0)),
                              _mod_spec(2, TM), _mod_spec(3, TM), _mod_spec(4, TM),
                              _const_spec((D_MODEL, D_MODEL)), _const_spec((1, D_MODEL)),
                              _const_spec((D_MODEL, N_EXPERTS)), _const_spec((1, N_EXPERTS))],
        out_specs=(pl.BlockSpec((TM, D_MODEL), lambda i: (i, 0)),
                   pl.BlockSpec((TM * ROW_TILES, LANES), lambda i: (i, 0)),
                   pl.BlockSpec((TM, TOP_K), lambda i: (i, 0)),
                   pl.BlockSpec((TM, TOP_K), lambda i: (i, 0)),
                   _const_spec((1, N_EXPERTS))),
        out_shape=(jax.ShapeDtypeStruct((N, D_MODEL), F32),
                   jax.ShapeDtypeStruct((N * ROW_TILES, LANES), F32),
                   jax.ShapeDtypeStruct((N, TOP_K), I32),
                   jax.ShapeDtypeStruct((N, TOP_K), F32),
                   jax.ShapeDtypeStruct((1, N_EXPERTS), F32)),
        compiler_params=_cparams(("arbitrary",)),
        name="post_l%d" % layer,
    )(*mix_args, x, mods, mods, mods, w_bf, n2g.reshape(1, -1), rw, rb.reshape(1, -1))


def _slots_kernel(cnt_ref, bst_ref, eidt_ref, base_ref, triu_ref, slot_ref, carry, dvm,
                  dsm0, dsm1, dsm2, dsm3, sem):
    j = pl.program_id(0)

    @pl.when(j == 0)
    def _():
        carry[...] = jnp.zeros_like(carry)

    e_iota = lax.broadcasted_iota(I32, (N_EXPERTS, RT), 0)
    ohs = [(eidt_ref[k:k + 1, :] == e_iota).astype(F32) for k in range(TOP_K)]
    ohsum = ohs[0] + ohs[1] + ohs[2] + ohs[3]
    cum = jnp.dot(ohsum.astype(BF16), triu_ref[...], preferred_element_type=F32)
    tot = cum + carry[...] + base_ref[...]
    dvm[...] = jnp.concatenate(
        [jnp.sum(oh * tot, axis=0, keepdims=True) for oh in ohs], axis=0).astype(I32)
    carry[...] += jnp.sum(ohsum, axis=1, keepdims=True)

    dsm = (dsm0, dsm1, dsm2, dsm3)
    copies = [pltpu.make_async_copy(dvm.at[k], dsm[k], sem.at[k]) for k in range(TOP_K)]
    for cp in copies:
        cp.start()
    for cp in copies:
        cp.wait()

    def scatter(c, carry_):
        t0 = c * SCATTER_UNROLL
        a0 = (j * RT + t0) * TOP_K
        for u in range(SCATTER_UNROLL):
            for k in range(TOP_K):
                slot_ref[dsm[k][t0 + u]] = a0 + (u * TOP_K + k)
        return carry_

    lax.fori_loop(0, RT // SCATTER_UNROLL, scatter, 0)

    @pl.when(j == pl.num_programs(0) - 1)
    def _():
        def pad(s, c):
            slot_ref[s] = (N + s % SLOT_LEAD) * TOP_K + (TOP_K - 1)
            return c

        def per_expert(e, c):
            cnt = cnt_ref[e]
            b0 = bst_ref[e]
            lax.fori_loop(SLOT_LEAD + b0 * TME + cnt,
                          SLOT_LEAD + (b0 + (cnt + TME - 1) // TME) * TME, pad, 0)
            return c

        lax.fori_loop(0, SLOT_LEAD, pad, 0)
        lax.fori_loop(0, N_EXPERTS, per_expert, 0)
        lax.fori_loop(SLOT_LEAD + bst_ref[N_EXPERTS] * TME, N_SLOTS, pad, 0)


def _route(eid, counts_f):
    counts = counts_f.reshape(-1).astype(I32)
    nblk = (counts + TME - 1) // TME
    bend = jnp.cumsum(nblk)
    bstart = bend - nblk
    bst = jnp.concatenate([bstart, bend[-1:]]).astype(I32)
    base = (bstart * TME + SLOT_LEAD).astype(F32).reshape(N_EXPERTS, 1)
    ar = jnp.arange(RT, dtype=I32)
    triu = (ar[:, None] < ar[None, :]).astype(BF16)
    grid_spec = pltpu.PrefetchScalarGridSpec(
        num_scalar_prefetch=2,
        grid=(N // RT,),
        in_specs=[pl.BlockSpec((TOP_K, RT), lambda j, c, b: (0, j)),
                  _const_spec((N_EXPERTS, 1)),
                  _const_spec((RT, RT))],
        out_specs=pl.BlockSpec(memory_space=pltpu.SMEM),
        scratch_shapes=[pltpu.VMEM((N_EXPERTS, 1), F32),
                        pltpu.VMEM((TOP_K, RT), I32)]
        + [pltpu.SMEM((RT,), I32)] * TOP_K
        + [pltpu.SemaphoreType.DMA((TOP_K,))],
    )
    slot_asg = pl.pallas_call(
        _slots_kernel,
        grid_spec=grid_spec,
        out_shape=jax.ShapeDtypeStruct((N_SLOTS,), I32),
        compiler_params=_cparams(("arbitrary",)),
        name="slots",
    )(counts, bst, eid.T, base, triu)
    return slot_asg, bst, counts


def _row_tile(idx):
    if isinstance(idx, int):
        return pl.ds(idx * ROW_TILES, ROW_TILES)
    return pl.ds(pl.multiple_of(idx * ROW_TILES, ROW_TILES), ROW_TILES)


def _expert_kernel(slot_ref, bst_ref, cnt_ref, xn_hbm, upw_ref, upb_ref, dww_ref, dwb_ref,
                   y_hbm, gbuf0, gbuf1, sbuf0, sbuf1, upbf, dwbf, xb_ref, hdn_ref, gsem, ssem):
    e = pl.program_id(0)
    nb = (cnt_ref[e] + TME - 1) // TME
    b0 = bst_ref[e]
    gbufs = (gbuf0, gbuf1)
    sbufs = (sbuf0, sbuf1)

    def gather_row(g, buf, r):
        v = slot_ref[(g + 2) * TME + r]
        tok = jnp.minimum(lax.shift_right_logical(v, TOP_K_SHIFT), N - 1)
        pltpu.make_async_copy(xn_hbm.at[_row_tile(tok), :], gbufs[buf].at[_row_tile(r), :],
                              gsem.at[buf]).start()

    def scatter_row(g, buf, r):
        v = slot_ref[(g + 2) * TME + r]
        dst = (v & (TOP_K - 1)) * N + lax.shift_right_logical(v, TOP_K_SHIFT)
        pltpu.make_async_copy(sbufs[buf].at[_row_tile(r), :], y_hbm.at[_row_tile(dst), :],
                              ssem.at[buf]).start()

    def rolled(row_fn, g, buf):
        def body(r, c):
            row_fn(g, buf, r)
            return c
        lax.fori_loop(0, TME, body, 0)

    def wait_gather(buf):
        pltpu.make_async_copy(xn_hbm.at[pl.ds(0, TME * ROW_TILES), :], gbufs[buf], gsem.at[buf]).wait()

    def wait_scatter(buf):
        pltpu.make_async_copy(sbufs[buf], y_hbm.at[pl.ds(0, TME * ROW_TILES), :], ssem.at[buf]).wait()

    @pl.when(e == 0)
    def _():
        sbuf0[...] = jnp.zeros_like(sbuf0)
        sbuf1[...] = jnp.zeros_like(sbuf1)
        rolled(gather_row, 0, 0)
        rolled(scatter_row, -2, 0)

    @pl.when(nb > 0)
    def _():
        upbf[...] = upw_ref[0].astype(BF16)
        dwbf[...] = dww_ref[0].astype(BF16)

    def work(g, cur):
        oth = 1 - cur
        wait_gather(cur)
        xb_ref[...] = jnp.concatenate(
            [gbufs[cur][pl.ds(s, TME, stride=ROW_TILES), :] for s in range(ROW_TILES)], axis=1).astype(BF16)

        def phase(ph, c):
            @pl.when(ph == 0)
            def _():
                for r in range(TME):
                    gather_row(g + 1, oth, r)
                gu = jnp.dot(xb_ref[...], upbf[...], preferred_element_type=F32) + upb_ref[0]
                gg = jnp.minimum(gu[:, :D_FF], SWIGLU_LIMIT)
                lin = jnp.clip(gu[:, D_FF:], -SWIGLU_LIMIT, SWIGLU_LIMIT)
                hdn_ref[...] = (gg * jax.nn.sigmoid(SWIGLU_ALPHA * gg) * (lin + 1.0)).astype(BF16)

            @pl.when(ph == 1)
            def _():
                for r in range(TME):
                    scatter_row(g - 1, oth, r)
                y = jnp.dot(hdn_ref[...], dwbf[...], preferred_element_type=F32) + dwb_ref[0]
                wait_scatter(cur)
                for s in range(ROW_TILES):
                    sbufs[cur][pl.ds(s, TME, stride=ROW_TILES), :] = y[:, s * LANES:(s + 1) * LANES]
            return c

        lax.fori_loop(0, 2, phase, 0)

    def block(j, carry):
        g = b0 + j
        for parity in range(2):
            @pl.when(g % 2 == parity)
            def _():
                work(g, parity)
        return carry

    lax.fori_loop(0, nb, block, 0)

    @pl.when(e == pl.num_programs(0) - 1)
    def _():
        g_end = bst_ref[N_EXPERTS]
        for parity in range(2):
            @pl.when(g_end % 2 == parity)
            def _():
                rolled(scatter_row, g_end - 1, 1 - parity)
                wait_gather(parity)
        wait_scatter(0)
        wait_scatter(1)


def _experts(slot_asg, bst, counts, xn_tiles, up_w, up_b, down_w, down_b):
    grid_spec = pltpu.PrefetchScalarGridSpec(
        num_scalar_prefetch=3,
        grid=(N_EXPERTS,),
        in_specs=[pl.BlockSpec(memory_space=pl.ANY),
                  pl.BlockSpec((1, D_MODEL, 2 * D_FF), lambda e, *_: (e, 0, 0)),
                  pl.BlockSpec((1, 1, 2 * D_FF), lambda e, *_: (e, 0, 0)),
                  pl.BlockSpec((1, D_FF, D_MODEL), lambda e, *_: (e, 0, 0)),
                  pl.BlockSpec((1, 1, D_MODEL), lambda e, *_: (e, 0, 0))],
        out_specs=pl.BlockSpec(memory_space=pl.ANY),
        scratch_shapes=[pltpu.VMEM((TME * ROW_TILES, LANES), F32),
                        pltpu.VMEM((TME * ROW_TILES, LANES), F32),
                        pltpu.VMEM((TME * ROW_TILES, LANES), F32),
                        pltpu.VMEM((TME * ROW_TILES, LANES), F32),
                        pltpu.VMEM((D_MODEL, 2 * D_FF), BF16),
                        pltpu.VMEM((D_FF, D_MODEL), BF16),
                        pltpu.VMEM((TME, D_MODEL), BF16),
                        pltpu.VMEM((TME, D_FF), BF16),
                        pltpu.SemaphoreType.DMA((2,)),
                        pltpu.SemaphoreType.DMA((2,))],
    )
    return pl.pallas_call(
        _expert_kernel,
        grid_spec=grid_spec,
        out_shape=jax.ShapeDtypeStruct(((N_ASG + SLOT_LEAD) * ROW_TILES, LANES), F32),
        compiler_params=_cparams(("arbitrary",)),
        name="experts",
    )(slot_asg, bst, counts, xn_tiles, up_w, up_b.reshape(N_EXPERTS, 1, -1),
      down_w, down_b.reshape(N_EXPERTS, 1, -1))


def _combine_kernel(x_ref, gate_ref, g2_ref, fg_ref, y0_ref, y1_ref, y2_ref, y3_ref, o_ref, *, final):
    gate = gate_ref[...]
    cols = []
    for s in range(ROW_TILES):
        acc = None
        for k, y_ref in enumerate((y0_ref, y1_ref, y2_ref, y3_ref)):
            term = gate[:, k:k + 1] * y_ref[pl.ds(s, TMC, stride=ROW_TILES), :]
            acc = term if acc is None else acc + term
        cols.append(acc)
    x2 = x_ref[...] + g2_ref[0] * jnp.concatenate(cols, axis=1)
    if final:
        ms = jnp.mean(x2 * x2, axis=-1, keepdims=True)
        x2 = x2 * lax.rsqrt(ms + EPS) * fg_ref[...]
    o_ref[...] = x2


def _combine(x, gates, mods, final_g, y_tiles, final):
    nblk = N // TMC

    def y_spec(k):
        return pl.BlockSpec((TMC * ROW_TILES, LANES), lambda i: (k * nblk + i, 0))

    return pl.pallas_call(
        functools.partial(_combine_kernel, final=final),
        grid=(nblk,),
        in_specs=[pl.BlockSpec((TMC, D_MODEL), lambda i: (i, 0)),
                  pl.BlockSpec((TMC, TOP_K), lambda i: (i, 0)),
                  _mod_spec(5, TMC),
                  _const_spec((1, D_MODEL)),
                  y_spec(0), y_spec(1), y_spec(2), y_spec(3)],
        out_specs=pl.BlockSpec((TMC, D_MODEL), lambda i: (i, 0)),
        out_shape=jax.ShapeDtypeStruct((N, D_MODEL), F32),
        compiler_params=_cparams(("arbitrary",)),
        name="combine",
    )(x, gates, mods, final_g.reshape(1, -1), y_tiles, y_tiles, y_tiles, y_tiles)


def _rope_tables():
    pos = jnp.arange(DEC_SEQ)
    r = (pos // GRID_W).astype(F32)
    col = (pos % GRID_W).astype(F32)
    quarter = HEAD_DIM // 4
    inv = ROPE_BASE ** (-jnp.arange(quarter, dtype=F32) / quarter)
    ar = r[:, None] * inv
    ac = col[:, None] * inv
    ang = jnp.concatenate([ar, ar, ac, ac], axis=-1)
    cos = jnp.tile(jnp.cos(ang), (1, LANES // HEAD_DIM))
    sin = jnp.tile(jnp.sin(ang), (1, LANES // HEAD_DIM))
    first = (jnp.arange(LANES) % 32) < 16
    sa = jnp.where(first[None, :], -sin, 0.0)
    sb = jnp.where(first[None, :], 0.0, sin)
    return cos, sa, sb


def _moe(x, xn, eid, gates, counts, mods, final_g, up_w, up_b, down_w, down_b, final):
    slot_asg, bst, counts_i = _route(eid, counts)
    y = _experts(slot_asg, bst, counts_i, xn, up_w, up_b, down_w, down_b)
    return _combine(x, gates, mods, final_g, y, final)


@jax.jit
def kernel(x_prompt, x_sample, cache_l0_k, cache_l0_v, cache_l1_k, cache_l1_v, c, c_ctx, final_g,
           l0_mod_w, l0_mod_b, l0_norm1_g, l0_w_in, l0_conv_w, l0_conv_b, l0_cnorm_g, l0_cnorm_b,
           l0_sink, l0_w_out, l0_norm2_g, l0_router_w, l0_router_b, l0_up_w, l0_up_b, l0_down_w,
           l0_down_b,
           l1_mod_w, l1_mod_b, l1_norm1_g, l1_w_in, l1_lam_q1, l1_lam_k1, l1_lam_q2, l1_lam_k2,
           l1_subln_g, l1_w_out, l1_norm2_g, l1_router_w, l1_router_b, l1_up_w, l1_up_b, l1_down_w,
           l1_down_b):
    x = jnp.concatenate([x_prompt.reshape(NP, D_MODEL), x_sample.reshape(NS, D_MODEL)], axis=0)
    cond16 = jnp.concatenate([c, c_ctx[None, :], jnp.zeros((16 - DEC_BATCH - 1, D_MODEL), F32)], axis=0)
    rope_tabs = _rope_tables()

    mods = _modulation(cond16, l0_mod_w, l0_mod_b)
    ha, hq, st0 = _inproj(x, mods, l0_norm1_g, l0_w_in.astype(BF16), rope_tabs, 0)
    a = _conv(ha, l0_conv_w, l0_conv_b, l0_cnorm_g, l0_cnorm_b)
    o_p = _l0_prompt_attn(hq, l0_sink)
    o_s = _l0_window_attn(hq, l0_sink, cache_l0_k.reshape(DEC_BATCH, PAST_LEN, LANES),
                          cache_l0_v.reshape(DEC_BATCH, PAST_LEN, LANES))
    x, xn, eid, gates, counts = _post(0, (a, o_p, o_s), x, mods, l0_w_out.astype(BF16), l0_norm2_g,
                                      l0_router_w, l0_router_b)
    x = _moe(x, xn, eid, gates, counts, mods, final_g, l0_up_w, l0_up_b, l0_down_w, l0_down_b, False)
    state_l0_k = st0[:, :LANES].reshape(BATCH, SEQ, WIN_KV, HEAD_DIM)
    state_l0_v = st0[:, LANES:].reshape(BATCH, SEQ, WIN_KV, HEAD_DIM)

    lam_init = 0.8 - 0.6 * math.exp(-0.3 * 1)
    mods = _modulation(cond16, l1_mod_w, l1_mod_b)
    h1, st1 = _inproj(x, mods, l1_norm1_g, l1_w_in.astype(BF16), rope_tabs, 1)
    lam_vecs = jnp.stack([l1_lam_q1, l1_lam_k1, l1_lam_q2, l1_lam_k2], axis=0)
    o_p = _diff_attn(h1, lam_vecs, l1_subln_g, lam_init, None)
    o_s = _diff_attn(h1, lam_vecs, l1_subln_g, lam_init,
                     (cache_l1_k.reshape(DEC_BATCH, PAST_LEN, D_MODEL),
                      cache_l1_v.reshape(DEC_BATCH, PAST_LEN, D_MODEL)))
    x, xn, eid, gates, counts = _post(1, (o_p, o_s), x, mods, l1_w_out.astype(BF16), l1_norm2_g,
                                      l1_router_w, l1_router_b)
    x = _moe(x, xn, eid, gates, counts, mods, final_g, l1_up_w, l1_up_b, l1_down_w, l1_down_b, True)
    state_l1_k = st1[:, :D_MODEL].reshape(BATCH, SEQ, DIFF_HEADS, 2, HEAD_DIM)
    state_l1_v = st1[:, D_MODEL:].reshape(BATCH, SEQ, DIFF_HEADS, DIFF_V)

    y_prompt = x[:NP].reshape(BATCH, SEQ, D_MODEL)
    y_sample = x[NP:].reshape(DEC_BATCH, DEC_SEQ, D_MODEL)
    return (y_prompt, y_sample, state_l0_k, state_l0_v, state_l1_k, state_l1_v)
```

```python
import functools
import math

import jax
import jax.numpy as jnp
from jax import lax
from jax.experimental import pallas as pl
from jax.experimental.pallas import tpu as pltpu

F32 = jnp.float32
BF16 = jnp.bfloat16
I32 = jnp.int32

D_MODEL = 1024
BATCH = 16
SEQ = 256
DEC_BATCH = 8
DEC_SEQ = 2048
PAST_LEN = 256
GRID_W = 64
HEAD_DIM = 64
BLOCK = 128
WINDOW = 128
ROPE_BASE = 10000.0
EPS = 1e-6
NEG = -1e30
ATTN_SCALE = HEAD_DIM ** -0.5
LOG2E = math.log2(math.e)
CONV_CH = 512
CONV_W = 31
WIN_HEADS = 8
WIN_KV = 2
WIN_G = 4
IN0_W = 1792
DIFF_HEADS = 8
DIFF_V = 128
IN1_W = 3072
N_EXPERTS = 32
TOP_K = 4
TOP_K_SHIFT = 2
D_FF = 1024
SWIGLU_LIMIT = 7.0
SWIGLU_ALPHA = 1.702

NP = BATCH * SEQ
NS = DEC_BATCH * DEC_SEQ
N = NP + NS
N_ASG = N * TOP_K

LANES = 128
SUBLANES = 8
ROW_TILES = D_MODEL // LANES
VMEM_LIMIT = 56 * 1024 * 1024

TM = 512
TMC = 512
TME = 256
CB = 256
HALO = 16
SHIFT_ROWS = CB + 2 * HALO - SUBLANES
TQ = 256
RT = 512
SCATTER_UNROLL = 64
MAX_EBLOCKS = (N_ASG + N_EXPERTS * (TME - 1)) // TME
SLOT_LEAD = 2 * TME
N_SLOTS = (MAX_EBLOCKS + 3) * TME

_NT = (((1,), (1,)), ((), ()))


def _cparams(sem):
    return pltpu.CompilerParams(dimension_semantics=sem, vmem_limit_bytes=VMEM_LIMIT)


def _mod_row(i, tm):
    npb = NP // tm
    return jnp.where(i < npb, DEC_BATCH, (i - npb) // (DEC_SEQ // tm))


def _mod_spec(j, tm):
    return pl.BlockSpec((1, 1, D_MODEL), lambda i, *_: (_mod_row(i, tm) * 6 + j, 0, 0))


def _const_spec(shape):
    nd = len(shape)
    return pl.BlockSpec(shape, lambda *_: (0,) * nd)


def _ada_norm(x, g, shift, scale):
    ms = jnp.mean(x * x, axis=-1, keepdims=True)
    return (x * lax.rsqrt(ms + EPS) * g) * (1.0 + scale) + shift


def _mod_kernel(c_ref, w_ref, b_ref, o_ref):
    c = c_ref[...]
    s = c * jax.nn.sigmoid(c)
    o_ref[...] = jnp.dot(s.astype(BF16), w_ref[...].astype(BF16), preferred_element_type=F32) + b_ref[...]


def _modulation(cond16, w, b):
    m = pl.pallas_call(
        _mod_kernel,
        grid=(6,),
        in_specs=[_const_spec((16, D_MODEL)),
                  pl.BlockSpec((D_MODEL, D_MODEL), lambda j: (0, j)),
                  pl.BlockSpec((1, D_MODEL), lambda j: (0, j))],
        out_specs=pl.BlockSpec((16, D_MODEL), lambda j: (0, j)),
        out_shape=jax.ShapeDtypeStruct((16, 6 * D_MODEL), F32),
        compiler_params=_cparams(("arbitrary",)),
        name="modulation",
    )(cond16, w, b.reshape(1, -1))
    return m.reshape(16 * 6, 1, D_MODEL)


def _rope(v, cos, sa, sb):
    return v * cos + pltpu.roll(v, LANES - 16, 1) * sa + pltpu.roll(v, 16, 1) * sb


def _inproj_kernel(x_ref, sh_ref, sc_ref, g_ref, w_ref, cos_ref, sa_ref, sb_ref, *outs, layer):
    i = pl.program_id(0)
    h = _ada_norm(x_ref[...], g_ref[...], sh_ref[0], sc_ref[0])
    acc = jnp.dot(h.astype(BF16), w_ref[...], preferred_element_type=F32)
    if layer == 0:
        ha_ref, hq_ref, st_ref = outs
        ha_ref[...] = acc[:, :2 * CONV_CH]
        base, n_q, n_rope, n_all, st_lo = 2 * CONV_CH, 4, 5, 6, 2 * CONV_CH + 512
        q_scale = ATTN_SCALE
    else:
        hq_ref, st_ref = outs
        base, n_q, n_rope, n_all, st_lo = 0, 8, 16, 24, 1024
        q_scale = ATTN_SCALE * LOG2E
    is_prompt = i < NP // TM

    def chunk(c):
        v = acc[:, base + c * LANES: base + (c + 1) * LANES]
        return v * q_scale if c < n_q else v

    @pl.when(is_prompt)
    def _():
        for c in range(n_all):
            hq_ref[:, c * LANES:(c + 1) * LANES] = chunk(c).astype(BF16)
        st_ref[...] = acc[:, st_lo:]

    @pl.when(jnp.logical_not(is_prompt))
    def _():
        cos, sa, sb = cos_ref[...], sa_ref[...], sb_ref[...]
        for c in range(n_all):
            v = chunk(c)
            if c < n_rope:
                v = _rope(v, cos, sa, sb)
            hq_ref[:, c * LANES:(c + 1) * LANES] = v.astype(BF16)


def _inproj(x, mods, g, w_bf, rope_tabs, layer):
    npb = NP // TM
    spb = DEC_SEQ // TM
    nout = w_bf.shape[1]
    rope_spec = pl.BlockSpec((TM, LANES), lambda i: (jnp.where(i < npb, 0, (i - npb) % spb), 0))
    st_w = 256 if layer == 0 else 2048
    st_spec = pl.BlockSpec((TM, st_w), lambda i: (jnp.minimum(i, npb - 1), 0))
    if layer == 0:
        out_shape = (jax.ShapeDtypeStruct((N, 2 * CONV_CH), F32),
                     jax.ShapeDtypeStruct((N, 768), BF16),
                     jax.ShapeDtypeStruct((NP, st_w), F32))
        out_specs = (pl.BlockSpec((TM, 2 * CONV_CH), lambda i: (i, 0)),
                     pl.BlockSpec((TM, 768), lambda i: (i, 0)), st_spec)
    else:
        out_shape = (jax.ShapeDtypeStruct((N, IN1_W), BF16),
                     jax.ShapeDtypeStruct((NP, st_w), F32))
        out_specs = (pl.BlockSpec((TM, IN1_W), lambda i: (i, 0)), st_spec)
    return pl.pallas_call(
        functools.partial(_inproj_kernel, layer=layer),
        grid=(N // TM,),
        in_specs=[pl.BlockSpec((TM, D_MODEL), lambda i: (i, 0)),
                  _mod_spec(0, TM), _mod_spec(1, TM),
                  _const_spec((1, D_MODEL)),
                  _const_spec((D_MODEL, nout)),
                  rope_spec, rope_spec, rope_spec],
        out_specs=out_specs,
        out_shape=out_shape,
        compiler_params=_cparams(("arbitrary",)),
        name="inproj_l%d" % layer,
    )(x, mods, mods, g.reshape(1, -1), w_bf, *rope_tabs)


def _conv_kernel(prev_ref, cur_ref, next_ref, w_ref, b_ref, g_ref, bb_ref, o_ref, upad, ush):
    i = pl.program_id(0)
    npb = NP // CB
    spb = DEC_SEQ // CB
    j = (i - npb) % spb
    first = jnp.logical_or(i < npb, j == 0)
    last = jnp.logical_or(i < npb, j == spb - 1)

    def glu(r):
        return r[:, :CONV_CH] * jax.nn.sigmoid(r[:, CONV_CH:])

    upad[HALO:HALO + CB, :] = glu(cur_ref[...])
    upad[0:HALO, :] = jnp.where(first, 0.0, glu(prev_ref[...]))
    upad[HALO + CB:, :] = jnp.where(last, 0.0, glu(next_ref[...]))
    acc = jnp.zeros((CB, CONV_CH), F32)
    off = HALO - CONV_W // 2
    for b in range(SUBLANES):
        taps = [t for t in range(CONV_W) if (off + t) % SUBLANES == b]
        if b > 0:
            ush[b - 1] = upad[b:b + SHIFT_ROWS, :]
        for t in taps:
            a8 = off + t - b
            win = upad[a8:a8 + CB, :] if b == 0 else ush[b - 1, a8:a8 + CB, :]
            acc = acc + w_ref[t:t + 1, :] * win
    u = acc + b_ref[...]
    mu = jnp.mean(u, axis=-1, keepdims=True)
    var = jnp.mean(jnp.square(u - mu), axis=-1, keepdims=True)
    y = (u - mu) * lax.rsqrt(var + EPS) * g_ref[...] + bb_ref[...]
    o_ref[...] = (y * jax.nn.sigmoid(y)).astype(BF16)


def _conv(ha, conv_w, conv_b, cg, cb):
    hb = CB // HALO
    nh = N // HALO
    w_pad = jnp.concatenate([conv_w, jnp.zeros((1, CONV_CH), F32)], axis=0)
    return pl.pallas_call(
        _conv_kernel,
        grid=(N // CB,),
        in_specs=[pl.BlockSpec((HALO, 2 * CONV_CH), lambda i: (jnp.maximum(i * hb - 1, 0), 0)),
                  pl.BlockSpec((CB, 2 * CONV_CH), lambda i: (i, 0)),
                  pl.BlockSpec((HALO, 2 * CONV_CH), lambda i: (jnp.minimum((i + 1) * hb, nh - 1), 0)),
                  _const_spec((CONV_W + 1, CONV_CH)),
                  _const_spec((1, CONV_CH)), _const_spec((1, CONV_CH)), _const_spec((1, CONV_CH))],
        out_specs=pl.BlockSpec((CB, CONV_CH), lambda i: (i, 0)),
        out_shape=jax.ShapeDtypeStruct((N, CONV_CH), BF16),
        scratch_shapes=[pltpu.VMEM((CB + 2 * HALO, CONV_CH), F32),
                        pltpu.VMEM((SUBLANES - 1, SHIFT_ROWS, CONV_CH), F32)],
        compiler_params=_cparams(("arbitrary",)),
        name="conformer_conv",
    )(ha, ha, ha, w_pad, conv_b.reshape(1, -1), cg.reshape(1, -1), cb.reshape(1, -1))


def _sink_attend(q, k, v, sink, mask):
    s = lax.dot_general(q, k, _NT, preferred_element_type=F32)
    if mask is not None:
        s = jnp.where(mask, s, NEG)
    m = jnp.maximum(jnp.max(s, axis=-1, keepdims=True), sink)
    p = jnp.exp(s - m)
    den = jnp.sum(p, axis=-1, keepdims=True) + jnp.exp(sink - m)
    return jnp.dot(p.astype(BF16), v, preferred_element_type=F32) / den


def _l0_prompt_attn_kernel(sink_ref, q_ref, k_ref, v_ref, o_ref):
    k = k_ref[...]
    v = v_ref[...]
    outs = []
    for h in range(WIN_HEADS):
        j = h // WIN_G
        outs.append(_sink_attend(q_ref[:, h * HEAD_DIM:(h + 1) * HEAD_DIM],
                                 k[:, j * HEAD_DIM:(j + 1) * HEAD_DIM],
                                 v[:, j * HEAD_DIM:(j + 1) * HEAD_DIM], sink_ref[h], None))
    o_ref[...] = jnp.concatenate(outs, axis=1).astype(BF16)


def _l0_prompt_attn(hq, sink):
    return pl.pallas_call(
        _l0_prompt_attn_kernel,
        grid=(BATCH,),
        in_specs=[pl.BlockSpec(memory_space=pltpu.SMEM),
                  pl.BlockSpec((SEQ, 512), lambda b: (b, 0)),
                  pl.BlockSpec((SEQ, LANES), lambda b: (b, 4)),
                  pl.BlockSpec((SEQ, LANES), lambda b: (b, 5))],
        out_specs=pl.BlockSpec((SEQ, 512), lambda b: (b, 0)),
        out_shape=jax.ShapeDtypeStruct((NP, 512), BF16),
        compiler_params=_cparams(("arbitrary",)),
        name="l0_prompt_attn",
    )(sink, hq, hq, hq)


def _l0_window_attn_kernel(sink_ref, q_ref, kp_ref, kc_ref, kn_ref, vp_ref, vc_ref, vn_ref,
                           ck_ref, cv_ref, o_ref):
    n = pl.program_id(1)
    k = jnp.concatenate([kp_ref[...], kc_ref[...], kn_ref[...], ck_ref[...].astype(BF16)], axis=0)
    v = jnp.concatenate([vp_ref[...], vc_ref[...], vn_ref[...], cv_ref[...].astype(BF16)], axis=0)
    qpos = n * BLOCK + lax.broadcasted_iota(I32, (BLOCK, 3 * BLOCK + PAST_LEN), 0)
    col = lax.broadcasted_iota(I32, (BLOCK, 3 * BLOCK + PAST_LEN), 1)
    kpos = (n - 1) * BLOCK + col
    local_ok = (jnp.abs(kpos - qpos) <= WINDOW) & (kpos >= 0) & (kpos < DEC_SEQ)
    ok = jnp.logical_or(col >= 3 * BLOCK, local_ok).astype(F32)
    mask = jnp.concatenate([ok] * WIN_G, axis=0) > 0.5
    grp = lax.broadcasted_iota(I32, (WIN_G * BLOCK, 1), 0) // BLOCK
    outs = []
    for j in range(WIN_KV):
        heads = range(j * WIN_G, (j + 1) * WIN_G)
        q4 = jnp.concatenate([q_ref[:, h * HEAD_DIM:(h + 1) * HEAD_DIM] for h in heads], axis=0)
        sink4 = jnp.zeros((WIN_G * BLOCK, 1), F32)
        for i, h in enumerate(heads):
            sink4 = jnp.where(grp == i, sink_ref[h], sink4)
        o4 = _sink_attend(q4, k[:, j * HEAD_DIM:(j + 1) * HEAD_DIM],
                          v[:, j * HEAD_DIM:(j + 1) * HEAD_DIM], sink4, mask)
        outs += [o4[i * BLOCK:(i + 1) * BLOCK, :] for i in range(WIN_G)]
    o_ref[...] = jnp.concatenate(outs, axis=1).astype(BF16)


def _l0_window_attn(hq, sink, ck, cv):
    nb = DEC_SEQ // BLOCK
    r0 = NP // BLOCK

    def kv_spec(col, d):
        return pl.BlockSpec((BLOCK, LANES),
                            lambda b, n: (r0 + b * nb + jnp.clip(n + d, 0, nb - 1), col))

    ctx_spec = pl.BlockSpec((None, PAST_LEN, LANES), lambda b, n: (b, 0, 0))
    return pl.pallas_call(
        _l0_window_attn_kernel,
        grid=(DEC_BATCH, nb),
        in_specs=[pl.BlockSpec(memory_space=pltpu.SMEM),
                  pl.BlockSpec((BLOCK, 512), lambda b, n: (r0 + b * nb + n, 0)),
                  kv_spec(4, -1), kv_spec(4, 0), kv_spec(4, 1),
                  kv_spec(5, -1), kv_spec(5, 0), kv_spec(5, 1),
                  ctx_spec, ctx_spec],
        out_specs=pl.BlockSpec((BLOCK, 512), lambda b, n: (b * nb + n, 0)),
        out_shape=jax.ShapeDtypeStruct((NS, 512), BF16),
        compiler_params=_cparams(("arbitrary", "arbitrary")),
        name="l0_window_attn",
    )(sink, hq, hq, hq, hq, hq, hq, hq, ck, cv)


def _diff_attn_kernel(*refs, lam_init, has_ctx):
    if has_ctx:
        q_ref, k_ref, v_ref, ck_ref, cv_ref, lam_ref, g_ref, o_ref = refs
    else:
        q_ref, k_ref, v_ref, lam_ref, g_ref, o_ref = refs
    lv = lam_ref[...]
    lam = (jnp.exp(jnp.sum(lv[0:1] * lv[1:2], axis=-1, keepdims=True))
           - jnp.exp(jnp.sum(lv[2:3] * lv[3:4], axis=-1, keepdims=True)) + lam_init)
    g = g_ref[...]
    outs = []
    for h in range(DIFF_HEADS):
        probs = []
        for c in range(2):
            lo = h * DIFF_V + c * HEAD_DIM
            q = q_ref[:, lo:lo + HEAD_DIM]
            s = lax.dot_general(q, k_ref[:, lo:lo + HEAD_DIM], _NT, preferred_element_type=F32)
            m = jnp.max(s, axis=-1, keepdims=True)
            if has_ctx:
                sc = lax.dot_general(q, ck_ref[:, lo:lo + HEAD_DIM].astype(BF16), _NT,
                                     preferred_element_type=F32)
                m = jnp.maximum(m, jnp.max(sc, axis=-1, keepdims=True))
                pc = jnp.exp2(sc - m)
            p = jnp.exp2(s - m)
            den = jnp.sum(p, axis=-1, keepdims=True)
            if has_ctx:
                den = den + jnp.sum(pc, axis=-1, keepdims=True)
                probs.append((p, pc, den))
            else:
                probs.append((p, None, den))
        (p0, pc0, d0), (p1, pc1, d1) = probs
        ratio = lam * d0 / d1
        a = (p0 - p1 * ratio).astype(BF16)
        o = jnp.dot(a, v_ref[:, h * DIFF_V:(h + 1) * DIFF_V], preferred_element_type=F32)
        if has_ctx:
            ac = (pc0 - pc1 * ratio).astype(BF16)
            o = o + jnp.dot(ac, cv_ref[:, h * DIFF_V:(h + 1) * DIFF_V].astype(BF16),
                            preferred_element_type=F32)
        o = o / d0
        ms = jnp.mean(o * o, axis=-1, keepdims=True)
        outs.append(((o * lax.rsqrt(ms + EPS)) * g) * (1.0 - lam_init))
    o_ref[...] = jnp.concatenate(outs, axis=1).astype(BF16)


def _diff_attn(h1, lam_vecs, subln_g, lam_init, ctx):
    g = subln_g.reshape(1, DIFF_V)
    small = [_const_spec((4, HEAD_DIM)), _const_spec((1, DIFF_V))]
    if ctx is None:
        grid = (BATCH, 1)
        in_specs = [pl.BlockSpec((SEQ, D_MODEL), lambda b, i: (b, 0)),
                    pl.BlockSpec((SEQ, D_MODEL), lambda b, i: (b, 1)),
                    pl.BlockSpec((SEQ, D_MODEL), lambda b, i: (b, 2))] + small
        out_specs = pl.BlockSpec((SEQ, D_MODEL), lambda b, i: (b, 0))
        rows = NP
        args = (h1, h1, h1, lam_vecs, g)
        name = "diff_attn_prompt"
    else:
        nq = DEC_SEQ // TQ
        q0 = NP // TQ
        s0 = NP // DEC_SEQ
        ctx_spec = pl.BlockSpec((None, PAST_LEN, D_MODEL), lambda b, i: (b, 0, 0))
        grid = (DEC_BATCH, nq)
        in_specs = [pl.BlockSpec((TQ, D_MODEL), lambda b, i: (q0 + b * nq + i, 0)),
                    pl.BlockSpec((DEC_SEQ, D_MODEL), lambda b, i: (s0 + b, 1)),
                    pl.BlockSpec((DEC_SEQ, D_MODEL), lambda b, i: (s0 + b, 2)),
                    ctx_spec, ctx_spec] + small
        out_specs = pl.BlockSpec((TQ, D_MODEL), lambda b, i: (b * nq + i, 0))
        rows = NS
        args = (h1, h1, h1, ctx[0], ctx[1], lam_vecs, g)
        name = "diff_attn_sample"
    return pl.pallas_call(
        functools.partial(_diff_attn_kernel, lam_init=lam_init, has_ctx=ctx is not None),
        grid=grid, in_specs=in_specs, out_specs=out_specs,
        out_shape=jax.ShapeDtypeStruct((rows, D_MODEL), BF16),
        compiler_params=_cparams(("arbitrary", "arbitrary")),
        name=name,
    )(*args)


def _store_row_tiles(ref, val, rows):
    for s in range(ROW_TILES):
        ref[pl.ds(s, rows, stride=ROW_TILES), :] = val[:, s * LANES:(s + 1) * LANES]


def _post_kernel(*refs, layer):
    if layer == 0:
        (a_ref, op_ref, os_ref, x_ref, g1_ref, sh2_ref, sc2_ref, w_ref, n2g_ref, rw_ref, rb_ref,
         xo_ref, xn_ref, eid_ref, gate_ref, cnt_ref) = refs
    else:
        (op_ref, os_ref, x_ref, g1_ref, sh2_ref, sc2_ref, w_ref, n2g_ref, rw_ref, rb_ref,
         xo_ref, xn_ref, eid_ref, gate_ref, cnt_ref) = refs
    i = pl.program_id(0)
    o = jnp.where(i < NP // TM, op_ref[...], os_ref[...])
    if layer == 0:
        mix = (jnp.dot(a_ref[...], w_ref[:CONV_CH, :], preferred_element_type=F32)
               + jnp.dot(o, w_ref[CONV_CH:, :], preferred_element_type=F32))
    else:
        mix = jnp.dot(o, w_ref[...], preferred_element_type=F32)
    x1 = x_ref[...] + g1_ref[0] * mix
    xo_ref[...] = x1
    xn = _ada_norm(x1, n2g_ref[...], sh2_ref[0], sc2_ref[0])
    _store_row_tiles(xn_ref, xn, TM)
    rw = rw_ref[...]
    xh = xn.astype(BF16)
    xl = (xn - xh.astype(F32)).astype(BF16)
    wh = rw.astype(BF16)
    wl = (rw - wh.astype(F32)).astype(BF16)
    logits = (jnp.dot(xh, wh, preferred_element_type=F32)
              + (jnp.dot(xl, wh, preferred_element_type=F32)
                 + jnp.dot(xh, wl, preferred_element_type=F32))) + rb_ref[...]
    lane = lax.broadcasted_iota(I32, logits.shape, 1)
    vals, idxs = [], []
    picked = jnp.zeros(logits.shape, F32)
    for _ in range(TOP_K):
        m = jnp.max(logits, axis=-1, keepdims=True)
        idx = jnp.min(jnp.where(logits == m, lane, N_EXPERTS), axis=-1, keepdims=True)
        vals.append(m)
        idxs.append(idx)
        hit = lane == idx
        picked = picked + hit.astype(F32)
        logits = jnp.where(hit, -jnp.inf, logits)
    es = [jnp.exp(v - vals[0]) for v in vals]
    den = es[0] + es[1] + es[2] + es[3]
    eid_ref[...] = jnp.concatenate(idxs, axis=1)
    gate_ref[...] = jnp.concatenate([e / den for e in es], axis=1)

    @pl.when(i == 0)
    def _():
        cnt_ref[...] = jnp.zeros_like(cnt_ref)

    cnt_ref[...] += jnp.sum(picked, axis=0, keepdims=True)


def _post(layer, mix_parts, x, mods, w_bf, n2g, rw, rb):
    npb = NP // TM
    if layer == 0:
        a, o_p, o_s = mix_parts
        wo = 512
        mix_specs = [pl.BlockSpec((TM, CONV_CH), lambda i: (i, 0))]
        mix_args = [a, o_p, o_s]
    else:
        o_p, o_s = mix_parts
        wo = D_MODEL
        mix_specs = []
        mix_args = [o_p, o_s]
    mix_specs += [pl.BlockSpec((TM, wo), lambda i: (jnp.minimum(i, npb - 1), 0)),
                  pl.BlockSpec((TM, wo), lambda i: (jnp.maximum(i - npb, 0), 0))]
    return pl.pallas_call(
        functools.partial(_post_kernel, layer=layer),
        grid=(N // TM,),
        in_specs=mix_specs + [pl.BlockSpec((TM, D_MODEL), lambda i: (i, 0)),
                              _mod_spec(2, TM), _mod_spec(3, TM), _mod_spec(4, TM),
                              _const_spec((D_MODEL, D_MODEL)), _const_spec((1, D_MODEL)),
                              _const_spec((D_MODEL, N_EXPERTS)), _const_spec((1, N_EXPERTS))],
        out_specs=(pl.BlockSpec((TM, D_MODEL), lambda i: (i, 0)),
                   pl.BlockSpec((TM * ROW_TILES, LANES), lambda i: (i, 0)),
                   pl.BlockSpec((TM, TOP_K), lambda i: (i, 0)),
                   pl.BlockSpec((TM, TOP_K), lambda i: (i, 0)),
                   _const_spec((1, N_EXPERTS))),
        out_shape=(jax.ShapeDtypeStruct((N, D_MODEL), F32),
                   jax.ShapeDtypeStruct((N * ROW_TILES, LANES), F32),
                   jax.ShapeDtypeStruct((N, TOP_K), I32),
                   jax.ShapeDtypeStruct((N, TOP_K), F32),
                   jax.ShapeDtypeStruct((1, N_EXPERTS), F32)),
        compiler_params=_cparams(("arbitrary",)),
        name="post_l%d" % layer,
    )(*mix_args, x, mods, mods, mods, w_bf, n2g.reshape(1, -1), rw, rb.reshape(1, -1))


def _slots_kernel(cnt_ref, bst_ref, eidt_ref, base_ref, triu_ref, slot_ref, carry, dvm,
                  dsm0, dsm1, dsm2, dsm3, sem):
    j = pl.program_id(0)

    @pl.when(j == 0)
    def _():
        carry[...] = jnp.zeros_like(carry)

    e_iota = lax.broadcasted_iota(I32, (N_EXPERTS, RT), 0)
    ohs = [(eidt_ref[k:k + 1, :] == e_iota).astype(F32) for k in range(TOP_K)]
    ohsum = ohs[0] + ohs[1] + ohs[2] + ohs[3]
    cum = jnp.dot(ohsum.astype(BF16), triu_ref[...], preferred_element_type=F32)
    tot = cum + carry[...] + base_ref[...]
    dvm[...] = jnp.concatenate(
        [jnp.sum(oh * tot, axis=0, keepdims=True) for oh in ohs], axis=0).astype(I32)
    carry[...] += jnp.sum(ohsum, axis=1, keepdims=True)

    dsm = (dsm0, dsm1, dsm2, dsm3)
    copies = [pltpu.make_async_copy(dvm.at[k], dsm[k], sem.at[k]) for k in range(TOP_K)]
    for cp in copies:
        cp.start()
    for cp in copies:
        cp.wait()

    def scatter(c, carry_):
        t0 = c * SCATTER_UNROLL
        a0 = (j * RT + t0) * TOP_K
        for u in range(SCATTER_UNROLL):
            for k in range(TOP_K):
                slot_ref[dsm[k][t0 + u]] = a0 + (u * TOP_K + k)
        return carry_

    lax.fori_loop(0, RT // SCATTER_UNROLL, scatter, 0)

    @pl.when(j == pl.num_programs(0) - 1)
    def _():
        def pad(s, c):
            slot_ref[s] = (N + s % SLOT_LEAD) * TOP_K + (TOP_K - 1)
            return c

        def per_expert(e, c):
            cnt = cnt_ref[e]
            b0 = bst_ref[e]
            lax.fori_loop(SLOT_LEAD + b0 * TME + cnt,
                          SLOT_LEAD + (b0 + (cnt + TME - 1) // TME) * TME, pad, 0)
            return c

        lax.fori_loop(0, SLOT_LEAD, pad, 0)
        lax.fori_loop(0, N_EXPERTS, per_expert, 0)
        lax.fori_loop(SLOT_LEAD + bst_ref[N_EXPERTS] * TME, N_SLOTS, pad, 0)


def _route(eid, counts_f):
    counts = counts_f.reshape(-1).astype(I32)
    nblk = (counts + TME - 1) // TME
    bend = jnp.cumsum(nblk)
    bstart = bend - nblk
    bst = jnp.concatenate([bstart, bend[-1:]]).astype(I32)
    base = (bstart * TME + SLOT_LEAD).astype(F32).reshape(N_EXPERTS, 1)
    ar = jnp.arange(RT, dtype=I32)
    triu = (ar[:, None] < ar[None, :]).astype(BF16)
    grid_spec = pltpu.PrefetchScalarGridSpec(
        num_scalar_prefetch=2,
        grid=(N // RT,),
        in_specs=[pl.BlockSpec((TOP_K, RT), lambda j, c, b: (0, j)),
                  _const_spec((N_EXPERTS, 1)),
                  _const_spec((RT, RT))],
        out_specs=pl.BlockSpec(memory_space=pltpu.SMEM),
        scratch_shapes=[pltpu.VMEM((N_EXPERTS, 1), F32),
                        pltpu.VMEM((TOP_K, RT), I32)]
        + [pltpu.SMEM((RT,), I32)] * TOP_K
        + [pltpu.SemaphoreType.DMA((TOP_K,))],
    )
    slot_asg = pl.pallas_call(
        _slots_kernel,
        grid_spec=grid_spec,
        out_shape=jax.ShapeDtypeStruct((N_SLOTS,), I32),
        compiler_params=_cparams(("arbitrary",)),
        name="slots",
    )(counts, bst, eid.T, base, triu)
    return slot_asg, bst, counts


def _row_tile(idx):
    if isinstance(idx, int):
        return pl.ds(idx * ROW_TILES, ROW_TILES)
    return pl.ds(pl.multiple_of(idx * ROW_TILES, ROW_TILES), ROW_TILES)


def _expert_kernel(slot_ref, bst_ref, cnt_ref, xn_hbm, upw_ref, upb_ref, dww_ref, dwb_ref,
                   y_hbm, flag_ref, gbuf0, gbuf1, sbuf0, sbuf1, upbf, dwbf, xb_ref, hdn_ref, gsem, ssem,
                   *, gather_mode="real", do_scatter=True):
    flag_ref[...] = jnp.zeros_like(flag_ref)
    e = pl.program_id(0)
    nb = (cnt_ref[e] + TME - 1) // TME
    b0 = bst_ref[e]
    gbufs = (gbuf0, gbuf1)
    sbufs = (sbuf0, sbuf1)

    def gather_row(g, buf, r):
        if gather_mode == "none":
            return
        v = slot_ref[(g + 2) * TME + r]
        tok = jnp.minimum(lax.shift_right_logical(v, TOP_K_SHIFT), N - 1)
        if gather_mode == "seq":
            tok = lax.rem((g + 2) * TME + r, N)
        pltpu.make_async_copy(xn_hbm.at[_row_tile(tok), :], gbufs[buf].at[_row_tile(r), :],
                              gsem.at[buf]).start()

    def scatter_row(g, buf, r):
        if not do_scatter:
            return
        v = slot_ref[(g + 2) * TME + r]
        dst = (v & (TOP_K - 1)) * N + lax.shift_right_logical(v, TOP_K_SHIFT)
        pltpu.make_async_copy(sbufs[buf].at[_row_tile(r), :], y_hbm.at[_row_tile(dst), :],
                              ssem.at[buf]).start()

    def rolled(row_fn, g, buf):
        def body(r, c):
            row_fn(g, buf, r)
            return c
        lax.fori_loop(0, TME, body, 0)

    def wait_gather(buf):
        if gather_mode == "none":
            return
        pltpu.make_async_copy(xn_hbm.at[pl.ds(0, TME * ROW_TILES), :], gbufs[buf], gsem.at[buf]).wait()

    def wait_scatter(buf):
        if not do_scatter:
            return
        pltpu.make_async_copy(sbufs[buf], y_hbm.at[pl.ds(0, TME * ROW_TILES), :], ssem.at[buf]).wait()

    @pl.when(e == 0)
    def _():
        sbuf0[...] = jnp.zeros_like(sbuf0)
        sbuf1[...] = jnp.zeros_like(sbuf1)
        if gather_mode == "none":
            gbuf0[...] = jnp.zeros_like(gbuf0)
            gbuf1[...] = jnp.zeros_like(gbuf1)
        rolled(gather_row, 0, 0)
        rolled(scatter_row, -2, 0)

    @pl.when(nb > 0)
    def _():
        upbf[...] = upw_ref[0].astype(BF16)
        dwbf[...] = dww_ref[0].astype(BF16)

    def work(g, cur):
        oth = 1 - cur
        wait_gather(cur)
        xb_ref[...] = jnp.concatenate(
            [gbufs[cur][pl.ds(s, TME, stride=ROW_TILES), :] for s in range(ROW_TILES)], axis=1).astype(BF16)

        def phase(ph, c):
            @pl.when(ph == 0)
            def _():
                for r in range(TME):
                    gather_row(g + 1, oth, r)
                gu = jnp.dot(xb_ref[...], upbf[...], preferred_element_type=F32) + upb_ref[0]
                gg = jnp.minimum(gu[:, :D_FF], SWIGLU_LIMIT)
                lin = jnp.clip(gu[:, D_FF:], -SWIGLU_LIMIT, SWIGLU_LIMIT)
                hdn_ref[...] = (gg * jax.nn.sigmoid(SWIGLU_ALPHA * gg) * (lin + 1.0)).astype(BF16)

            @pl.when(ph == 1)
            def _():
                for r in range(TME):
                    scatter_row(g - 1, oth, r)
                y = jnp.dot(hdn_ref[...], dwbf[...], preferred_element_type=F32) + dwb_ref[0]
                wait_scatter(cur)
                for s in range(ROW_TILES):
                    sbufs[cur][pl.ds(s, TME, stride=ROW_TILES), :] = y[:, s * LANES:(s + 1) * LANES]
            return c

        lax.fori_loop(0, 2, phase, 0)

    def block(j, carry):
        g = b0 + j
        for parity in range(2):
            @pl.when(g % 2 == parity)
            def _():
                work(g, parity)
        return carry

    lax.fori_loop(0, nb, block, 0)

    @pl.when(e == pl.num_programs(0) - 1)
    def _():
        g_end = bst_ref[N_EXPERTS]
        for parity in range(2):
            @pl.when(g_end % 2 == parity)
            def _():
                rolled(scatter_row, g_end - 1, 1 - parity)
                wait_gather(parity)
        wait_scatter(0)
        wait_scatter(1)


def _experts(slot_asg, bst, counts, xn_tiles, up_w, up_b, down_w, down_b, gather_mode="real", do_scatter=True):
    grid_spec = pltpu.PrefetchScalarGridSpec(
        num_scalar_prefetch=3,
        grid=(N_EXPERTS,),
        in_specs=[pl.BlockSpec(memory_space=pl.ANY),
                  pl.BlockSpec((1, D_MODEL, 2 * D_FF), lambda e, *_: (e, 0, 0)),
                  pl.BlockSpec((1, 1, 2 * D_FF), lambda e, *_: (e, 0, 0)),
                  pl.BlockSpec((1, D_FF, D_MODEL), lambda e, *_: (e, 0, 0)),
                  pl.BlockSpec((1, 1, D_MODEL), lambda e, *_: (e, 0, 0))],
        out_specs=(pl.BlockSpec(memory_space=pl.ANY), pl.BlockSpec((8, LANES), lambda e, *_: (0, 0))),
        scratch_shapes=[pltpu.VMEM((TME * ROW_TILES, LANES), F32),
                        pltpu.VMEM((TME * ROW_TILES, LANES), F32),
                        pltpu.VMEM((TME * ROW_TILES, LANES), F32),
                        pltpu.VMEM((TME * ROW_TILES, LANES), F32),
                        pltpu.VMEM((D_MODEL, 2 * D_FF), BF16),
                        pltpu.VMEM((D_FF, D_MODEL), BF16),
                        pltpu.VMEM((TME, D_MODEL), BF16),
                        pltpu.VMEM((TME, D_FF), BF16),
                        pltpu.SemaphoreType.DMA((2,)),
                        pltpu.SemaphoreType.DMA((2,))],
    )
    return pl.pallas_call(
        functools.partial(_expert_kernel, gather_mode=gather_mode, do_scatter=do_scatter),
        grid_spec=grid_spec,
        out_shape=(jax.ShapeDtypeStruct(((N_ASG + SLOT_LEAD) * ROW_TILES, LANES), F32),
                   jax.ShapeDtypeStruct((8, LANES), F32)),
        compiler_params=_cparams(("arbitrary",)),
        name="experts_%s_%s" % (gather_mode, "s" if do_scatter else "nos"),
    )(slot_asg, bst, counts, xn_tiles, up_w, up_b.reshape(N_EXPERTS, 1, -1),
      down_w, down_b.reshape(N_EXPERTS, 1, -1))


def _combine_kernel(x_ref, gate_ref, g2_ref, fg_ref, y0_ref, y1_ref, y2_ref, y3_ref, o_ref, *, final):
    gate = gate_ref[...]
    cols = []
    for s in range(ROW_TILES):
        acc = None
        for k, y_ref in enumerate((y0_ref, y1_ref, y2_ref, y3_ref)):
            term = gate[:, k:k + 1] * y_ref[pl.ds(s, TMC, stride=ROW_TILES), :]
            acc = term if acc is None else acc + term
        cols.append(acc)
    x2 = x_ref[...] + g2_ref[0] * jnp.concatenate(cols, axis=1)
    if final:
        ms = jnp.mean(x2 * x2, axis=-1, keepdims=True)
        x2 = x2 * lax.rsqrt(ms + EPS) * fg_ref[...]
    o_ref[...] = x2


def _combine(x, gates, mods, final_g, y_tiles, final):
    nblk = N // TMC

    def y_spec(k):
        return pl.BlockSpec((TMC * ROW_TILES, LANES), lambda i: (k * nblk + i, 0))

    return pl.pallas_call(
        functools.partial(_combine_kernel, final=final),
        grid=(nblk,),
        in_specs=[pl.BlockSpec((TMC, D_MODEL), lambda i: (i, 0)),
                  pl.BlockSpec((TMC, TOP_K), lambda i: (i, 0)),
                  _mod_spec(5, TMC),
                  _const_spec((1, D_MODEL)),
                  y_spec(0), y_spec(1), y_spec(2), y_spec(3)],
        out_specs=pl.BlockSpec((TMC, D_MODEL), lambda i: (i, 0)),
        out_shape=jax.ShapeDtypeStruct((N, D_MODEL), F32),
        compiler_params=_cparams(("arbitrary",)),
        name="combine",
    )(x, gates, mods, final_g.reshape(1, -1), y_tiles, y_tiles, y_tiles, y_tiles)


def _rope_tables():
    pos = jnp.arange(DEC_SEQ)
    r = (pos // GRID_W).astype(F32)
    col = (pos % GRID_W).astype(F32)
    quarter = HEAD_DIM // 4
    inv = ROPE_BASE ** (-jnp.arange(quarter, dtype=F32) / quarter)
    ar = r[:, None] * inv
    ac = col[:, None] * inv
    ang = jnp.concatenate([ar, ar, ac, ac], axis=-1)
    cos = jnp.tile(jnp.cos(ang), (1, LANES // HEAD_DIM))
    sin = jnp.tile(jnp.sin(ang), (1, LANES // HEAD_DIM))
    first = (jnp.arange(LANES) % 32) < 16
    sa = jnp.where(first[None, :], -sin, 0.0)
    sb = jnp.where(first[None, :], 0.0, sin)
    return cos, sa, sb


def _moe(x, xn, eid, gates, counts, mods, final_g, up_w, up_b, down_w, down_b, final):
    slot_asg, bst, counts_i = _route(eid, counts)
    y, flag = _experts(slot_asg, bst, counts_i, xn, up_w, up_b, down_w, down_b)
    if not final:
        for gm, ds in (("none", True), ("real", False), ("none", False), ("seq", True)):
            _, f2 = _experts(slot_asg, bst, counts_i, xn, up_w, up_b, down_w, down_b, gm, ds)
            flag = flag + f2
    x = x + flag[0:1, 0:1]
    return _combine(x, gates, mods, final_g, y, final)


@jax.jit
def kernel(x_prompt, x_sample, cache_l0_k, cache_l0_v, cache_l1_k, cache_l1_v, c, c_ctx, final_g,
           l0_mod_w, l0_mod_b, l0_norm1_g, l0_w_in, l0_conv_w, l0_conv_b, l0_cnorm_g, l0_cnorm_b,
           l0_sink, l0_w_out, l0_norm2_g, l0_router_w, l0_router_b, l0_up_w, l0_up_b, l0_down_w,
           l0_down_b,
           l1_mod_w, l1_mod_b, l1_norm1_g, l1_w_in, l1_lam_q1, l1_lam_k1, l1_lam_q2, l1_lam_k2,
           l1_subln_g, l1_w_out, l1_norm2_g, l1_router_w, l1_router_b, l1_up_w, l1_up_b, l1_down_w,
           l1_down_b):
    x = jnp.concatenate([x_prompt.reshape(NP, D_MODEL), x_sample.reshape(NS, D_MODEL)], axis=0)
    cond16 = jnp.concatenate([c, c_ctx[None, :], jnp.zeros((16 - DEC_BATCH - 1, D_MODEL), F32)], axis=0)
    rope_tabs = _rope_tables()

    mods = _modulation(cond16, l0_mod_w, l0_mod_b)
    ha, hq, st0 = _inproj(x, mods, l0_norm1_g, l0_w_in.astype(BF16), rope_tabs, 0)
    a = _conv(ha, l0_conv_w, l0_conv_b, l0_cnorm_g, l0_cnorm_b)
    o_p = _l0_prompt_attn(hq, l0_sink)
    o_s = _l0_window_attn(hq, l0_sink, cache_l0_k.reshape(DEC_BATCH, PAST_LEN, LANES),
                          cache_l0_v.reshape(DEC_BATCH, PAST_LEN, LANES))
    x, xn, eid, gates, counts = _post(0, (a, o_p, o_s), x, mods, l0_w_out.astype(BF16), l0_norm2_g,
                                      l0_router_w, l0_router_b)
    x = _moe(x, xn, eid, gates, counts, mods, final_g, l0_up_w, l0_up_b, l0_down_w, l0_down_b, False)
    state_l0_k = st0[:, :LANES].reshape(BATCH, SEQ, WIN_KV, HEAD_DIM)
    state_l0_v = st0[:, LANES:].reshape(BATCH, SEQ, WIN_KV, HEAD_DIM)

    lam_init = 0.8 - 0.6 * math.exp(-0.3 * 1)
    mods = _modulation(cond16, l1_mod_w, l1_mod_b)
    h1, st1 = _inproj(x, mods, l1_norm1_g, l1_w_in.astype(BF16), rope_tabs, 1)
    lam_vecs = jnp.stack([l1_lam_q1, l1_lam_k1, l1_lam_q2, l1_lam_k2], axis=0)
    o_p = _diff_attn(h1, lam_vecs, l1_subln_g, lam_init, None)
    o_s = _diff_attn(h1, lam_vecs, l1_subln_g, lam_init,
                     (cache_l1_k.reshape(DEC_BATCH, PAST_LEN, D_MODEL),
                      cache_l1_v.reshape(DEC_BATCH, PAST_LEN, D_MODEL)))
    x, xn, eid, gates, counts = _post(1, (o_p, o_s), x, mods, l1_w_out.astype(BF16), l1_norm2_g,
                                      l1_router_w, l1_router_b)
    x = _moe(x, xn, eid, gates, counts, mods, final_g, l1_up_w, l1_up_b, l1_down_w, l1_down_b, True)
    state_l1_k = st1[:, :D_MODEL].reshape(BATCH, SEQ, DIFF_HEADS, 2, HEAD_DIM)
    state_l1_v = st1[:, D_MODEL:].reshape(BATCH, SEQ, DIFF_HEADS, DIFF_V)

    y_prompt = x[:NP].reshape(BATCH, SEQ, D_MODEL)
    y_sample = x[NP:].reshape(DEC_BATCH, DEC_SEQ, D_MODEL)
    return (y_prompt, y_sample, state_l0_k, state_l0_v, state_l1_k, state_l1_v)
```

```python
import functools
import math

import jax
import jax.numpy as jnp
from jax import lax
from jax.experimental import pallas as pl
from jax.experimental.pallas import tpu as pltpu

F32 = jnp.float32
BF16 = jnp.bfloat16
I32 = jnp.int32

D_MODEL = 1024
BATCH = 16
SEQ = 256
DEC_BATCH = 8
DEC_SEQ = 2048
PAST_LEN = 256
GRID_W = 64
HEAD_DIM = 64
BLOCK = 128
WINDOW = 128
ROPE_BASE = 10000.0
EPS = 1e-6
NEG = -1e30
ATTN_SCALE = HEAD_DIM ** -0.5
LOG2E = math.log2(math.e)
CONV_CH = 512
CONV_W = 31
WIN_HEADS = 8
WIN_KV = 2
WIN_G = 4
IN0_W = 1792
DIFF_HEADS = 8
DIFF_V = 128
IN1_W = 3072
N_EXPERTS = 32
TOP_K = 4
TOP_K_SHIFT = 2
D_FF = 1024
SWIGLU_LIMIT = 7.0
SWIGLU_ALPHA = 1.702

NP = BATCH * SEQ
NS = DEC_BATCH * DEC_SEQ
N = NP + NS
N_ASG = N * TOP_K

LANES = 128
SUBLANES = 8
ROW_TILES = D_MODEL // LANES
VMEM_LIMIT = 56 * 1024 * 1024

TM = 512
TMC = 512
TME = 256
CB = 256
HALO = 16
SHIFT_ROWS = CB + 2 * HALO - SUBLANES
TQ = 256
RT = 512
SCATTER_UNROLL = 64
MAX_EBLOCKS = (N_ASG + N_EXPERTS * (TME - 1)) // TME
SLOT_LEAD = 2 * TME
N_SLOTS = (MAX_EBLOCKS + 3) * TME

_NT = (((1,), (1,)), ((), ()))


def _cparams(sem):
    return pltpu.CompilerParams(dimension_semantics=sem, vmem_limit_bytes=VMEM_LIMIT)


def _mod_row(i, tm):
    npb = NP // tm
    return jnp.where(i < npb, DEC_BATCH, (i - npb) // (DEC_SEQ // tm))


def _mod_spec(j, tm):
    return pl.BlockSpec((1, 1, D_MODEL), lambda i, *_: (_mod_row(i, tm) * 6 + j, 0, 0))


def _const_spec(shape):
    nd = len(shape)
    return pl.BlockSpec(shape, lambda *_: (0,) * nd)


def _x_specs(unified, tm):
    npb = NP // tm
    off = npb if unified else 0
    return [pl.BlockSpec((tm, D_MODEL), lambda i, *_: (jnp.minimum(i, npb - 1), 0)),
            pl.BlockSpec((tm, D_MODEL), lambda i, *_: (jnp.maximum(i - npb, 0) + off, 0))]


def _pick_x(xa_ref, xb_ref, tm):
    return jnp.where(pl.program_id(0) < NP // tm, xa_ref[...], xb_ref[...])


def _ada_norm(x, g, shift, scale):
    ms = jnp.mean(x * x, axis=-1, keepdims=True)
    return (x * lax.rsqrt(ms + EPS) * g) * (1.0 + scale) + shift


def _mod_kernel(c_ref, w_ref, b_ref, o_ref):
    c = c_ref[...]
    s = c * jax.nn.sigmoid(c)
    o_ref[...] = jnp.dot(s.astype(BF16), w_ref[...].astype(BF16), preferred_element_type=F32) + b_ref[...]


def _modulation(cond16, w, b):
    m = pl.pallas_call(
        _mod_kernel,
        grid=(6,),
        in_specs=[_const_spec((16, D_MODEL)),
                  pl.BlockSpec((D_MODEL, D_MODEL), lambda j: (0, j)),
                  pl.BlockSpec((1, D_MODEL), lambda j: (0, j))],
        out_specs=pl.BlockSpec((16, D_MODEL), lambda j: (0, j)),
        out_shape=jax.ShapeDtypeStruct((16, 6 * D_MODEL), F32),
        compiler_params=_cparams(("arbitrary",)),
        name="modulation",
    )(cond16, w, b.reshape(1, -1))
    return m.reshape(16 * 6, 1, D_MODEL)


def _rope(v, cos, sa, sb):
    return v * cos + pltpu.roll(v, LANES - 16, 1) * sa + pltpu.roll(v, 16, 1) * sb


def _inproj_kernel(xa_ref, xb_ref, sh_ref, sc_ref, g_ref, w_ref, cos_ref, sa_ref, sb_ref, *outs, layer):
    i = pl.program_id(0)
    h = _ada_norm(_pick_x(xa_ref, xb_ref, TM), g_ref[...], sh_ref[0], sc_ref[0])
    acc = jnp.dot(h.astype(BF16), w_ref[...], preferred_element_type=F32)
    if layer == 0:
        ha_ref, hq_ref, st_ref = outs
        ha_ref[...] = acc[:, :2 * CONV_CH]
        base, n_q, n_rope, n_all, st_lo = 2 * CONV_CH, 4, 5, 6, 2 * CONV_CH + 512
        q_scale = ATTN_SCALE
    else:
        hq_ref, st_ref = outs
        base, n_q, n_rope, n_all, st_lo = 0, 8, 16, 24, 1024
        q_scale = ATTN_SCALE * LOG2E
    is_prompt = i < NP // TM

    def chunk(c):
        v = acc[:, base + c * LANES: base + (c + 1) * LANES]
        return v * q_scale if c < n_q else v

    @pl.when(is_prompt)
    def _():
        for c in range(n_all):
            hq_ref[:, c * LANES:(c + 1) * LANES] = chunk(c).astype(BF16)
        st_ref[...] = acc[:, st_lo:]

    @pl.when(jnp.logical_not(is_prompt))
    def _():
        cos, sa, sb = cos_ref[...], sa_ref[...], sb_ref[...]
        for c in range(n_all):
            v = chunk(c)
            if c < n_rope:
                v = _rope(v, cos, sa, sb)
            hq_ref[:, c * LANES:(c + 1) * LANES] = v.astype(BF16)


def _inproj(xa, xb, mods, g, w_bf, rope_tabs, layer):
    npb = NP // TM
    spb = DEC_SEQ // TM
    nout = w_bf.shape[1]
    rope_spec = pl.BlockSpec((TM, LANES), lambda i: (jnp.where(i < npb, 0, (i - npb) % spb), 0))
    st_w = 256 if layer == 0 else 2048
    st_spec = pl.BlockSpec((TM, st_w), lambda i: (jnp.minimum(i, npb - 1), 0))
    if layer == 0:
        out_shape = (jax.ShapeDtypeStruct((N, 2 * CONV_CH), F32),
                     jax.ShapeDtypeStruct((N, 768), BF16),
                     jax.ShapeDtypeStruct((NP, st_w), F32))
        out_specs = (pl.BlockSpec((TM, 2 * CONV_CH), lambda i: (i, 0)),
                     pl.BlockSpec((TM, 768), lambda i: (i, 0)), st_spec)
    else:
        out_shape = (jax.ShapeDtypeStruct((N, IN1_W), BF16),
                     jax.ShapeDtypeStruct((NP, st_w), F32))
        out_specs = (pl.BlockSpec((TM, IN1_W), lambda i: (i, 0)), st_spec)
    return pl.pallas_call(
        functools.partial(_inproj_kernel, layer=layer),
        grid=(N // TM,),
        in_specs=_x_specs(xa is xb, TM) + [
                  _mod_spec(0, TM), _mod_spec(1, TM),
                  _const_spec((1, D_MODEL)),
                  _const_spec((D_MODEL, nout)),
                  rope_spec, rope_spec, rope_spec],
        out_specs=out_specs,
        out_shape=out_shape,
        compiler_params=_cparams(("arbitrary",)),
        name="inproj_l%d" % layer,
    )(xa, xb, mods, mods, g.reshape(1, -1), w_bf, *rope_tabs)


def _conv_kernel(prev_ref, cur_ref, next_ref, w_ref, b_ref, g_ref, bb_ref, o_ref, upad, ush):
    i = pl.program_id(0)
    npb = NP // CB
    spb = DEC_SEQ // CB
    j = (i - npb) % spb
    first = jnp.logical_or(i < npb, j == 0)
    last = jnp.logical_or(i < npb, j == spb - 1)

    def glu(r):
        return r[:, :CONV_CH] * jax.nn.sigmoid(r[:, CONV_CH:])

    upad[HALO:HALO + CB, :] = glu(cur_ref[...])
    upad[0:HALO, :] = jnp.where(first, 0.0, glu(prev_ref[...]))
    upad[HALO + CB:, :] = jnp.where(last, 0.0, glu(next_ref[...]))
    acc = jnp.zeros((CB, CONV_CH), F32)
    off = HALO - CONV_W // 2
    for b in range(SUBLANES):
        taps = [t for t in range(CONV_W) if (off + t) % SUBLANES == b]
        if b > 0:
            ush[b - 1] = upad[b:b + SHIFT_ROWS, :]
        for t in taps:
            a8 = off + t - b
            win = upad[a8:a8 + CB, :] if b == 0 else ush[b - 1, a8:a8 + CB, :]
            acc = acc + w_ref[t:t + 1, :] * win
    u = acc + b_ref[...]
    mu = jnp.mean(u, axis=-1, keepdims=True)
    var = jnp.mean(jnp.square(u - mu), axis=-1, keepdims=True)
    y = (u - mu) * lax.rsqrt(var + EPS) * g_ref[...] + bb_ref[...]
    o_ref[...] = (y * jax.nn.sigmoid(y)).astype(BF16)


def _conv(ha, conv_w, conv_b, cg, cb):
    hb = CB // HALO
    nh = N // HALO
    w_pad = jnp.concatenate([conv_w, jnp.zeros((1, CONV_CH), F32)], axis=0)
    return pl.pallas_call(
        _conv_kernel,
        grid=(N // CB,),
        in_specs=[pl.BlockSpec((HALO, 2 * CONV_CH), lambda i: (jnp.maximum(i * hb - 1, 0), 0)),
                  pl.BlockSpec((CB, 2 * CONV_CH), lambda i: (i, 0)),
                  pl.BlockSpec((HALO, 2 * CONV_CH), lambda i: (jnp.minimum((i + 1) * hb, nh - 1), 0)),
                  _const_spec((CONV_W + 1, CONV_CH)),
                  _const_spec((1, CONV_CH)), _const_spec((1, CONV_CH)), _const_spec((1, CONV_CH))],
        out_specs=pl.BlockSpec((CB, CONV_CH), lambda i: (i, 0)),
        out_shape=jax.ShapeDtypeStruct((N, CONV_CH), BF16),
        scratch_shapes=[pltpu.VMEM((CB + 2 * HALO, CONV_CH), F32),
                        pltpu.VMEM((SUBLANES - 1, SHIFT_ROWS, CONV_CH), F32)],
        compiler_params=_cparams(("arbitrary",)),
        name="conformer_conv",
    )(ha, ha, ha, w_pad, conv_b.reshape(1, -1), cg.reshape(1, -1), cb.reshape(1, -1))


def _sink_attend(q, k, v, sink, mask):
    s = lax.dot_general(q, k, _NT, preferred_element_type=F32)
    if mask is not None:
        s = jnp.where(mask, s, NEG)
    m = jnp.maximum(jnp.max(s, axis=-1, keepdims=True), sink)
    p = jnp.exp(s - m)
    den = jnp.sum(p, axis=-1, keepdims=True) + jnp.exp(sink - m)
    return jnp.dot(p.astype(BF16), v, preferred_element_type=F32) / den


def _l0_prompt_attn_kernel(sink_ref, q_ref, k_ref, v_ref, o_ref):
    k = k_ref[...]
    v = v_ref[...]
    outs = []
    for h in range(WIN_HEADS):
        j = h // WIN_G
        outs.append(_sink_attend(q_ref[:, h * HEAD_DIM:(h + 1) * HEAD_DIM],
                                 k[:, j * HEAD_DIM:(j + 1) * HEAD_DIM],
                                 v[:, j * HEAD_DIM:(j + 1) * HEAD_DIM], sink_ref[h], None))
    o_ref[...] = jnp.concatenate(outs, axis=1).astype(BF16)


def _l0_prompt_attn(hq, sink):
    return pl.pallas_call(
        _l0_prompt_attn_kernel,
        grid=(BATCH,),
        in_specs=[pl.BlockSpec(memory_space=pltpu.SMEM),
                  pl.BlockSpec((SEQ, 512), lambda b: (b, 0)),
                  pl.BlockSpec((SEQ, LANES), lambda b: (b, 4)),
                  pl.BlockSpec((SEQ, LANES), lambda b: (b, 5))],
        out_specs=pl.BlockSpec((SEQ, 512), lambda b: (b, 0)),
        out_shape=jax.ShapeDtypeStruct((NP, 512), BF16),
        compiler_params=_cparams(("arbitrary",)),
        name="l0_prompt_attn",
    )(sink, hq, hq, hq)


def _l0_window_attn_kernel(sink_ref, q_ref, kp_ref, kc_ref, kn_ref, vp_ref, vc_ref, vn_ref,
                           ck_ref, cv_ref, o_ref):
    n = pl.program_id(1)
    k = jnp.concatenate([kp_ref[...], kc_ref[...], kn_ref[...], ck_ref[...].astype(BF16)], axis=0)
    v = jnp.concatenate([vp_ref[...], vc_ref[...], vn_ref[...], cv_ref[...].astype(BF16)], axis=0)
    qpos = n * BLOCK + lax.broadcasted_iota(I32, (BLOCK, 3 * BLOCK + PAST_LEN), 0)
    col = lax.broadcasted_iota(I32, (BLOCK, 3 * BLOCK + PAST_LEN), 1)
    kpos = (n - 1) * BLOCK + col
    local_ok = (jnp.abs(kpos - qpos) <= WINDOW) & (kpos >= 0) & (kpos < DEC_SEQ)
    ok = jnp.logical_or(col >= 3 * BLOCK, local_ok).astype(F32)
    mask = jnp.concatenate([ok] * WIN_G, axis=0) > 0.5
    grp = lax.broadcasted_iota(I32, (WIN_G * BLOCK, 1), 0) // BLOCK
    outs = []
    for j in range(WIN_KV):
        heads = range(j * WIN_G, (j + 1) * WIN_G)
        q4 = jnp.concatenate([q_ref[:, h * HEAD_DIM:(h + 1) * HEAD_DIM] for h in heads], axis=0)
        sink4 = jnp.zeros((WIN_G * BLOCK, 1), F32)
        for i, h in enumerate(heads):
            sink4 = jnp.where(grp == i, sink_ref[h], sink4)
        o4 = _sink_attend(q4, k[:, j * HEAD_DIM:(j + 1) * HEAD_DIM],
                          v[:, j * HEAD_DIM:(j + 1) * HEAD_DIM], sink4, mask)
        outs += [o4[i * BLOCK:(i + 1) * BLOCK, :] for i in range(WIN_G)]
    o_ref[...] = jnp.concatenate(outs, axis=1).astype(BF16)


def _l0_window_attn(hq, sink, ck, cv):
    nb = DEC_SEQ // BLOCK
    r0 = NP // BLOCK

    def kv_spec(col, d):
        return pl.BlockSpec((BLOCK, LANES),
                            lambda b, n: (r0 + b * nb + jnp.clip(n + d, 0, nb - 1), col))

    ctx_spec = pl.BlockSpec((None, PAST_LEN, LANES), lambda b, n: (b, 0, 0))
    return pl.pallas_call(
        _l0_window_attn_kernel,
        grid=(DEC_BATCH, nb),
        in_specs=[pl.BlockSpec(memory_space=pltpu.SMEM),
                  pl.BlockSpec((BLOCK, 512), lambda b, n: (r0 + b * nb + n, 0)),
                  kv_spec(4, -1), kv_spec(4, 0), kv_spec(4, 1),
                  kv_spec(5, -1), kv_spec(5, 0), kv_spec(5, 1),
                  ctx_spec, ctx_spec],
        out_specs=pl.BlockSpec((BLOCK, 512), lambda b, n: (b * nb + n, 0)),
        out_shape=jax.ShapeDtypeStruct((NS, 512), BF16),
        compiler_params=_cparams(("arbitrary", "arbitrary")),
        name="l0_window_attn",
    )(sink, hq, hq, hq, hq, hq, hq, hq, ck, cv)


def _diff_attn_kernel(*refs, lam_init, has_ctx):
    if has_ctx:
        q_ref, k_ref, v_ref, ck_ref, cv_ref, lam_ref, g_ref, o_ref, vaug, cvaug = refs
    else:
        q_ref, k_ref, v_ref, lam_ref, g_ref, o_ref, vaug = refs

    @pl.when(pl.program_id(1) == 0)
    def _():
        ones = jnp.ones((v_ref.shape[0], DIFF_V), BF16)
        for h in range(DIFF_HEADS):
            vaug[h, :, :DIFF_V] = v_ref[:, h * DIFF_V:(h + 1) * DIFF_V]
            vaug[h, :, DIFF_V:] = ones
        if has_ctx:
            cones = jnp.ones((PAST_LEN, DIFF_V), BF16)
            for h in range(DIFF_HEADS):
                cvaug[h, :, :DIFF_V] = cv_ref[:, h * DIFF_V:(h + 1) * DIFF_V].astype(BF16)
                cvaug[h, :, DIFF_V:] = cones

    lv = lam_ref[...]
    lam = (jnp.exp(jnp.sum(lv[0:1] * lv[1:2], axis=-1, keepdims=True))
           - jnp.exp(jnp.sum(lv[2:3] * lv[3:4], axis=-1, keepdims=True)) + lam_init)
    g = g_ref[...]
    outs = []
    for h in range(DIFF_HEADS):
        comps = []
        for c in range(2):
            lo = h * DIFF_V + c * HEAD_DIM
            q = q_ref[:, lo:lo + HEAD_DIM]
            s = lax.dot_general(q, k_ref[:, lo:lo + HEAD_DIM], _NT, preferred_element_type=F32)
            m = jnp.max(s, axis=-1, keepdims=True)
            if has_ctx:
                sc = lax.dot_general(q, ck_ref[:, lo:lo + HEAD_DIM].astype(BF16), _NT,
                                     preferred_element_type=F32)
                m = jnp.maximum(m, jnp.max(sc, axis=-1, keepdims=True))
            oa = jnp.dot(jnp.exp2(s - m).astype(BF16), vaug[h], preferred_element_type=F32)
            if has_ctx:
                oa = oa + jnp.dot(jnp.exp2(sc - m).astype(BF16), cvaug[h], preferred_element_type=F32)
            comps.append((oa[:, :DIFF_V], oa[:, DIFF_V:DIFF_V + 1]))
        (o0, d0), (o1, d1) = comps
        o = o0 / d0 - o1 * (lam / d1)
        ms = jnp.mean(o * o, axis=-1, keepdims=True)
        outs.append(((o * lax.rsqrt(ms + EPS)) * g) * (1.0 - lam_init))
    o_ref[...] = jnp.concatenate(outs, axis=1).astype(BF16)


def _diff_attn(h1, lam_vecs, subln_g, lam_init, ctx):
    g = subln_g.reshape(1, DIFF_V)
    small = [_const_spec((4, HEAD_DIM)), _const_spec((1, DIFF_V))]
    if ctx is None:
        grid = (BATCH, 1)
        in_specs = [pl.BlockSpec((SEQ, D_MODEL), lambda b, i: (b, 0)),
                    pl.BlockSpec((SEQ, D_MODEL), lambda b, i: (b, 1)),
                    pl.BlockSpec((SEQ, D_MODEL), lambda b, i: (b, 2))] + small
        out_specs = pl.BlockSpec((SEQ, D_MODEL), lambda b, i: (b, 0))
        rows = NP
        args = (h1, h1, h1, lam_vecs, g)
        name = "diff_attn_prompt"
    else:
        nq = DEC_SEQ // TQ
        q0 = NP // TQ
        s0 = NP // DEC_SEQ
        ctx_spec = pl.BlockSpec((None, PAST_LEN, D_MODEL), lambda b, i: (b, 0, 0))
        grid = (DEC_BATCH, nq)
        in_specs = [pl.BlockSpec((TQ, D_MODEL), lambda b, i: (q0 + b * nq + i, 0)),
                    pl.BlockSpec((DEC_SEQ, D_MODEL), lambda b, i: (s0 + b, 1)),
                    pl.BlockSpec((DEC_SEQ, D_MODEL), lambda b, i: (s0 + b, 2)),
                    ctx_spec, ctx_spec] + small
        out_specs = pl.BlockSpec((TQ, D_MODEL), lambda b, i: (b * nq + i, 0))
        rows = NS
        args = (h1, h1, h1, ctx[0], ctx[1], lam_vecs, g)
        name = "diff_attn_sample"
    tk = SEQ if ctx is None else DEC_SEQ
    scratch = [pltpu.VMEM((DIFF_HEADS, tk, 2 * DIFF_V), BF16)]
    if ctx is not None:
        scratch.append(pltpu.VMEM((DIFF_HEADS, PAST_LEN, 2 * DIFF_V), BF16))
    return pl.pallas_call(
        functools.partial(_diff_attn_kernel, lam_init=lam_init, has_ctx=ctx is not None),
        grid=grid, in_specs=in_specs, out_specs=out_specs,
        scratch_shapes=scratch,
        out_shape=jax.ShapeDtypeStruct((rows, D_MODEL), BF16),
        compiler_params=_cparams(("arbitrary", "arbitrary")),
        name=name,
    )(*args)


def _store_row_tiles(ref, val, rows):
    for s in range(ROW_TILES):
        ref[pl.ds(s, rows, stride=ROW_TILES), :] = val[:, s * LANES:(s + 1) * LANES]


def _post_kernel(*refs, layer):
    if layer == 0:
        (a_ref, op_ref, os_ref, xa_ref, xb_ref, g1_ref, sh2_ref, sc2_ref, w_ref, n2g_ref, rw_ref, rb_ref,
         xo_ref, xn_ref, eid_ref, gate_ref, cnt_ref) = refs
    else:
        (op_ref, os_ref, xa_ref, xb_ref, g1_ref, sh2_ref, sc2_ref, w_ref, n2g_ref, rw_ref, rb_ref,
         xo_ref, xn_ref, eid_ref, gate_ref, cnt_ref) = refs
    i = pl.program_id(0)
    o = jnp.where(i < NP // TM, op_ref[...], os_ref[...])
    if layer == 0:
        mix = (jnp.dot(a_ref[...], w_ref[:CONV_CH, :], preferred_element_type=F32)
               + jnp.dot(o, w_ref[CONV_CH:, :], preferred_element_type=F32))
    else:
        mix = jnp.dot(o, w_ref[...], preferred_element_type=F32)
    x1 = _pick_x(xa_ref, xb_ref, TM) + g1_ref[0] * mix
    xo_ref[...] = x1
    xn = _ada_norm(x1, n2g_ref[...], sh2_ref[0], sc2_ref[0])
    _store_row_tiles(xn_ref, xn, TM)
    rw = rw_ref[...]
    xh = xn.astype(BF16)
    xl = (xn - xh.astype(F32)).astype(BF16)
    wh = rw.astype(BF16)
    wl = (rw - wh.astype(F32)).astype(BF16)
    logits = (jnp.dot(xh, wh, preferred_element_type=F32)
              + (jnp.dot(xl, wh, preferred_element_type=F32)
                 + jnp.dot(xh, wl, preferred_element_type=F32))) + rb_ref[...]
    lane = lax.broadcasted_iota(I32, logits.shape, 1)
    vals, idxs = [], []
    picked = jnp.zeros(logits.shape, F32)
    for _ in range(TOP_K):
        m = jnp.max(logits, axis=-1, keepdims=True)
        idx = jnp.min(jnp.where(logits == m, lane, N_EXPERTS), axis=-1, keepdims=True)
        vals.append(m)
        idxs.append(idx)
        hit = lane == idx
        picked = picked + hit.astype(F32)
        logits = jnp.where(hit, -jnp.inf, logits)
    es = [jnp.exp(v - vals[0]) for v in vals]
    den = es[0] + es[1] + es[2] + es[3]
    eid_ref[...] = jnp.concatenate(idxs, axis=1)
    gate_ref[...] = jnp.concatenate([e / den for e in es], axis=1)

    @pl.when(i == 0)
    def _():
        cnt_ref[...] = jnp.zeros_like(cnt_ref)

    cnt_ref[...] += jnp.sum(picked, axis=0, keepdims=True)


def _post(layer, mix_parts, xa, xb, mods, w_bf, n2g, rw, rb):
    npb = NP // TM
    if layer == 0:
        a, o_p, o_s = mix_parts
        wo = 512
        mix_specs = [pl.BlockSpec((TM, CONV_CH), lambda i: (i, 0))]
        mix_args = [a, o_p, o_s]
    else:
        o_p, o_s = mix_parts
        wo = D_MODEL
        mix_specs = []
        mix_args = [o_p, o_s]
    mix_specs += [pl.BlockSpec((TM, wo), lambda i: (jnp.minimum(i, npb - 1), 0)),
                  pl.BlockSpec((TM, wo), lambda i: (jnp.maximum(i - npb, 0), 0))]
    return pl.pallas_call(
        functools.partial(_post_kernel, layer=layer),
        grid=(N // TM,),
        in_specs=mix_specs + _x_specs(xa is xb, TM) + [
                              _mod_spec(2, TM), _mod_spec(3, TM), _mod_spec(4, TM),
                              _const_spec((D_MODEL, D_MODEL)), _const_spec((1, D_MODEL)),
                              _const_spec((D_MODEL, N_EXPERTS)), _const_spec((1, N_EXPERTS))],
        out_specs=(pl.BlockSpec((TM, D_MODEL), lambda i: (i, 0)),
                   pl.BlockSpec((TM * ROW_TILES, LANES), lambda i: (i, 0)),
                   pl.BlockSpec((TM, TOP_K), lambda i: (i, 0)),
                   pl.BlockSpec((TM, TOP_K), lambda i: (i, 0)),
                   _const_spec((1, N_EXPERTS))),
        out_shape=(jax.ShapeDtypeStruct((N, D_MODEL), F32),
                   jax.ShapeDtypeStruct((N * ROW_TILES, LANES), F32),
                   jax.ShapeDtypeStruct((N, TOP_K), I32),
                   jax.ShapeDtypeStruct((N, TOP_K), F32),
                   jax.ShapeDtypeStruct((1, N_EXPERTS), F32)),
        compiler_params=_cparams(("arbitrary",)),
        name="post_l%d" % layer,
    )(*mix_args, xa, xb, mods, mods, mods, w_bf, n2g.reshape(1, -1), rw, rb.reshape(1, -1))


def _slots_kernel(cnt_ref, bst_ref, eidt_ref, base_ref, triu_ref, slot_ref, carry, dvm,
                  dsm0, dsm1, dsm2, dsm3, sem):
    j = pl.program_id(0)

    @pl.when(j == 0)
    def _():
        carry[...] = jnp.zeros_like(carry)

    e_iota = lax.broadcasted_iota(I32, (N_EXPERTS, RT), 0)
    ohs = [(eidt_ref[k:k + 1, :] == e_iota).astype(F32) for k in range(TOP_K)]
    ohsum = ohs[0] + ohs[1] + ohs[2] + ohs[3]
    cum = jnp.dot(ohsum.astype(BF16), triu_ref[...], preferred_element_type=F32)
    tot = cum + carry[...] + base_ref[...]
    dvm[...] = jnp.concatenate(
        [jnp.sum(oh * tot, axis=0, keepdims=True) for oh in ohs], axis=0).astype(I32)
    carry[...] += jnp.sum(ohsum, axis=1, keepdims=True)

    dsm = (dsm0, dsm1, dsm2, dsm3)
    copies = [pltpu.make_async_copy(dvm.at[k], dsm[k], sem.at[k]) for k in range(TOP_K)]
    for cp in copies:
        cp.start()
    for cp in copies:
        cp.wait()

    def scatter(c, carry_):
        t0 = c * SCATTER_UNROLL
        a0 = (j * RT + t0) * TOP_K
        for u in range(SCATTER_UNROLL):
            for k in range(TOP_K):
                slot_ref[dsm[k][t0 + u]] = a0 + (u * TOP_K + k)
        return carry_

    lax.fori_loop(0, RT // SCATTER_UNROLL, scatter, 0)

    @pl.when(j == pl.num_programs(0) - 1)
    def _():
        def pad(s, c):
            slot_ref[s] = (N + s % SLOT_LEAD) * TOP_K + (TOP_K - 1)
            return c

        def per_expert(e, c):
            cnt = cnt_ref[e]
            b0 = bst_ref[e]
            lax.fori_loop(SLOT_LEAD + b0 * TME + cnt,
                          SLOT_LEAD + (b0 + (cnt + TME - 1) // TME) * TME, pad, 0)
            return c

        lax.fori_loop(0, SLOT_LEAD, pad, 0)
        lax.fori_loop(0, N_EXPERTS, per_expert, 0)
        lax.fori_loop(SLOT_LEAD + bst_ref[N_EXPERTS] * TME, N_SLOTS, pad, 0)


def _route(eid, counts_f):
    counts = counts_f.reshape(-1).astype(I32)
    nblk = (counts + TME - 1) // TME
    bend = jnp.cumsum(nblk)
    bstart = bend - nblk
    bst = jnp.concatenate([bstart, bend[-1:]]).astype(I32)
    base = (bstart * TME + SLOT_LEAD).astype(F32).reshape(N_EXPERTS, 1)
    ar = jnp.arange(RT, dtype=I32)
    triu = (ar[:, None] < ar[None, :]).astype(BF16)
    grid_spec = pltpu.PrefetchScalarGridSpec(
        num_scalar_prefetch=2,
        grid=(N // RT,),
        in_specs=[pl.BlockSpec((TOP_K, RT), lambda j, c, b: (0, j)),
                  _const_spec((N_EXPERTS, 1)),
                  _const_spec((RT, RT))],
        out_specs=pl.BlockSpec(memory_space=pltpu.SMEM),
        scratch_shapes=[pltpu.VMEM((N_EXPERTS, 1), F32),
                        pltpu.VMEM((TOP_K, RT), I32)]
        + [pltpu.SMEM((RT,), I32)] * TOP_K
        + [pltpu.SemaphoreType.DMA((TOP_K,))],
    )
    slot_asg = pl.pallas_call(
        _slots_kernel,
        grid_spec=grid_spec,
        out_shape=jax.ShapeDtypeStruct((N_SLOTS,), I32),
        compiler_params=_cparams(("arbitrary",)),
        name="slots",
    )(counts, bst, eid.T, base, triu)
    return slot_asg, bst, counts


def _row_tile(idx):
    if isinstance(idx, int):
        return pl.ds(idx * ROW_TILES, ROW_TILES)
    return pl.ds(pl.multiple_of(idx * ROW_TILES, ROW_TILES), ROW_TILES)


def _expert_kernel(slot_ref, bst_ref, cnt_ref, xn_hbm, upw_ref, upb_ref, dww_ref, dwb_ref,
                   y_hbm, gbuf0, gbuf1, sbuf0, sbuf1, upbf, dwbf, xb_ref, hdn_ref, gsem, ssem):
    e = pl.program_id(0)
    nb = (cnt_ref[e] + TME - 1) // TME
    b0 = bst_ref[e]
    gbufs = (gbuf0, gbuf1)
    sbufs = (sbuf0, sbuf1)

    def gather_row(g, buf, r):
        v = slot_ref[(g + 2) * TME + r]
        tok = jnp.minimum(lax.shift_right_logical(v, TOP_K_SHIFT), N - 1)
        pltpu.make_async_copy(xn_hbm.at[_row_tile(tok), :], gbufs[buf].at[_row_tile(r), :],
                              gsem.at[buf]).start()

    def scatter_row(g, buf, r):
        v = slot_ref[(g + 2) * TME + r]
        dst = (v & (TOP_K - 1)) * N + lax.shift_right_logical(v, TOP_K_SHIFT)
        pltpu.make_async_copy(sbufs[buf].at[_row_tile(r), :], y_hbm.at[_row_tile(dst), :],
                              ssem.at[buf]).start()

    def rolled(row_fn, g, buf):
        def body(r, c):
            row_fn(g, buf, r)
            return c
        lax.fori_loop(0, TME, body, 0)

    def wait_gather(buf):
        pltpu.make_async_copy(xn_hbm.at[pl.ds(0, TME * ROW_TILES), :], gbufs[buf], gsem.at[buf]).wait()

    def wait_scatter(buf):
        pltpu.make_async_copy(sbufs[buf], y_hbm.at[pl.ds(0, TME * ROW_TILES), :], ssem.at[buf]).wait()

    @pl.when(e == 0)
    def _():
        sbuf0[...] = jnp.zeros_like(sbuf0)
        sbuf1[...] = jnp.zeros_like(sbuf1)
        rolled(gather_row, 0, 0)
        rolled(scatter_row, -2, 0)

    @pl.when(nb > 0)
    def _():
        upbf[...] = upw_ref[0].astype(BF16)
        dwbf[...] = dww_ref[0].astype(BF16)

    def work(g, cur):
        oth = 1 - cur
        wait_gather(cur)
        xb_ref[...] = jnp.concatenate(
            [gbufs[cur][pl.ds(s, TME, stride=ROW_TILES), :] for s in range(ROW_TILES)], axis=1).astype(BF16)

        def phase(ph, c):
            @pl.when(ph == 0)
            def _():
                for r in range(TME):
                    gather_row(g + 1, oth, r)
                gu = jnp.dot(xb_ref[...], upbf[...], preferred_element_type=F32) + upb_ref[0]
                gg = jnp.minimum(gu[:, :D_FF], SWIGLU_LIMIT)
                lin = jnp.clip(gu[:, D_FF:], -SWIGLU_LIMIT, SWIGLU_LIMIT)
                hdn_ref[...] = (gg * jax.nn.sigmoid(SWIGLU_ALPHA * gg) * (lin + 1.0)).astype(BF16)

            @pl.when(ph == 1)
            def _():
                for r in range(TME):
                    scatter_row(g - 1, oth, r)
                y = jnp.dot(hdn_ref[...], dwbf[...], preferred_element_type=F32) + dwb_ref[0]
                wait_scatter(cur)
                for s in range(ROW_TILES):
                    sbufs[cur][pl.ds(s, TME, stride=ROW_TILES), :] = y[:, s * LANES:(s + 1) * LANES]
            return c

        lax.fori_loop(0, 2, phase, 0)

    def block(j, carry):
        g = b0 + j
        for parity in range(2):
            @pl.when(g % 2 == parity)
            def _():
                work(g, parity)
        return carry

    lax.fori_loop(0, nb, block, 0)

    @pl.when(e == pl.num_programs(0) - 1)
    def _():
        g_end = bst_ref[N_EXPERTS]
        for parity in range(2):
            @pl.when(g_end % 2 == parity)
            def _():
                rolled(scatter_row, g_end - 1, 1 - parity)
                wait_gather(parity)
        wait_scatter(0)
        wait_scatter(1)


def _experts(slot_asg, bst, counts, xn_tiles, up_w, up_b, down_w, down_b):
    grid_spec = pltpu.PrefetchScalarGridSpec(
        num_scalar_prefetch=3,
        grid=(N_EXPERTS,),
        in_specs=[pl.BlockSpec(memory_space=pl.ANY),
                  pl.BlockSpec((1, D_MODEL, 2 * D_FF), lambda e, *_: (e, 0, 0)),
                  pl.BlockSpec((1, 1, 2 * D_FF), lambda e, *_: (e, 0, 0)),
                  pl.BlockSpec((1, D_FF, D_MODEL), lambda e, *_: (e, 0, 0)),
                  pl.BlockSpec((1, 1, D_MODEL), lambda e, *_: (e, 0, 0))],
        out_specs=pl.BlockSpec(memory_space=pl.ANY),
        scratch_shapes=[pltpu.VMEM((TME * ROW_TILES, LANES), F32),
                        pltpu.VMEM((TME * ROW_TILES, LANES), F32),
                        pltpu.VMEM((TME * ROW_TILES, LANES), F32),
                        pltpu.VMEM((TME * ROW_TILES, LANES), F32),
                        pltpu.VMEM((D_MODEL, 2 * D_FF), BF16),
                        pltpu.VMEM((D_FF, D_MODEL), BF16),
                        pltpu.VMEM((TME, D_MODEL), BF16),
                        pltpu.VMEM((TME, D_FF), BF16),
                        pltpu.SemaphoreType.DMA((2,)),
                        pltpu.SemaphoreType.DMA((2,))],
    )
    return pl.pallas_call(
        _expert_kernel,
        grid_spec=grid_spec,
        out_shape=jax.ShapeDtypeStruct(((N_ASG + SLOT_LEAD) * ROW_TILES, LANES), F32),
        compiler_params=_cparams(("arbitrary",)),
        name="experts",
    )(slot_asg, bst, counts, xn_tiles, up_w, up_b.reshape(N_EXPERTS, 1, -1),
      down_w, down_b.reshape(N_EXPERTS, 1, -1))


def _combine_kernel(x_ref, gate_ref, g2_ref, fg_ref, y0_ref, y1_ref, y2_ref, y3_ref, *o_refs, final):
    gate = gate_ref[...]
    cols = []
    for s in range(ROW_TILES):
        acc = None
        for k, y_ref in enumerate((y0_ref, y1_ref, y2_ref, y3_ref)):
            term = gate[:, k:k + 1] * y_ref[pl.ds(s, TMC, stride=ROW_TILES), :]
            acc = term if acc is None else acc + term
        cols.append(acc)
    x2 = x_ref[...] + g2_ref[0] * jnp.concatenate(cols, axis=1)
    if not final:
        o_refs[0][...] = x2
        return
    ms = jnp.mean(x2 * x2, axis=-1, keepdims=True)
    y = x2 * lax.rsqrt(ms + EPS) * fg_ref[...]
    is_prompt = pl.program_id(0) < NP // TMC

    @pl.when(is_prompt)
    def _():
        o_refs[0][...] = y

    @pl.when(jnp.logical_not(is_prompt))
    def _():
        o_refs[1][...] = y


def _combine(x, gates, mods, final_g, y_tiles, final):
    nblk = N // TMC

    npb = NP // TMC

    def y_spec(k):
        return pl.BlockSpec((TMC * ROW_TILES, LANES), lambda i: (k * nblk + i, 0))

    if final:
        out_specs = (pl.BlockSpec((TMC, D_MODEL), lambda i: (jnp.minimum(i, npb - 1), 0)),
                     pl.BlockSpec((TMC, D_MODEL), lambda i: (jnp.maximum(i - npb, 0), 0)))
        out_shape = (jax.ShapeDtypeStruct((NP, D_MODEL), F32), jax.ShapeDtypeStruct((NS, D_MODEL), F32))
    else:
        out_specs = pl.BlockSpec((TMC, D_MODEL), lambda i: (i, 0))
        out_shape = jax.ShapeDtypeStruct((N, D_MODEL), F32)

    return pl.pallas_call(
        functools.partial(_combine_kernel, final=final),
        grid=(nblk,),
        in_specs=[pl.BlockSpec((TMC, D_MODEL), lambda i: (i, 0)),
                  pl.BlockSpec((TMC, TOP_K), lambda i: (i, 0)),
                  _mod_spec(5, TMC),
                  _const_spec((1, D_MODEL)),
                  y_spec(0), y_spec(1), y_spec(2), y_spec(3)],
        out_specs=out_specs,
        out_shape=out_shape,
        compiler_params=_cparams(("arbitrary",)),
        name="combine",
    )(x, gates, mods, final_g.reshape(1, -1), y_tiles, y_tiles, y_tiles, y_tiles)


def _rope_tables():
    pos = jnp.arange(DEC_SEQ)
    r = (pos // GRID_W).astype(F32)
    col = (pos % GRID_W).astype(F32)
    quarter = HEAD_DIM // 4
    inv = ROPE_BASE ** (-jnp.arange(quarter, dtype=F32) / quarter)
    ar = r[:, None] * inv
    ac = col[:, None] * inv
    ang = jnp.concatenate([ar, ar, ac, ac], axis=-1)
    cos = jnp.tile(jnp.cos(ang), (1, LANES // HEAD_DIM))
    sin = jnp.tile(jnp.sin(ang), (1, LANES // HEAD_DIM))
    first = (jnp.arange(LANES) % 32) < 16
    sa = jnp.where(first[None, :], -sin, 0.0)
    sb = jnp.where(first[None, :], 0.0, sin)
    return cos, sa, sb


def _moe(x, xn, eid, gates, counts, mods, final_g, up_w, up_b, down_w, down_b, final):
    slot_asg, bst, counts_i = _route(eid, counts)
    y = _experts(slot_asg, bst, counts_i, xn, up_w, up_b, down_w, down_b)
    return _combine(x, gates, mods, final_g, y, final)


@jax.jit
def kernel(x_prompt, x_sample, cache_l0_k, cache_l0_v, cache_l1_k, cache_l1_v, c, c_ctx, final_g,
           l0_mod_w, l0_mod_b, l0_norm1_g, l0_w_in, l0_conv_w, l0_conv_b, l0_cnorm_g, l0_cnorm_b,
           l0_sink, l0_w_out, l0_norm2_g, l0_router_w, l0_router_b, l0_up_w, l0_up_b, l0_down_w,
           l0_down_b,
           l1_mod_w, l1_mod_b, l1_norm1_g, l1_w_in, l1_lam_q1, l1_lam_k1, l1_lam_q2, l1_lam_k2,
           l1_subln_g, l1_w_out, l1_norm2_g, l1_router_w, l1_router_b, l1_up_w, l1_up_b, l1_down_w,
           l1_down_b):
    xp0 = x_prompt.reshape(NP, D_MODEL)
    xs0 = x_sample.reshape(NS, D_MODEL)
    cond16 = jnp.concatenate([c, c_ctx[None, :], jnp.zeros((16 - DEC_BATCH - 1, D_MODEL), F32)], axis=0)
    rope_tabs = _rope_tables()

    mods = _modulation(cond16, l0_mod_w, l0_mod_b)
    ha, hq, st0 = _inproj(xp0, xs0, mods, l0_norm1_g, l0_w_in.astype(BF16), rope_tabs, 0)
    a = _conv(ha, l0_conv_w, l0_conv_b, l0_cnorm_g, l0_cnorm_b)
    o_p = _l0_prompt_attn(hq, l0_sink)
    o_s = _l0_window_attn(hq, l0_sink, cache_l0_k.reshape(DEC_BATCH, PAST_LEN, LANES),
                          cache_l0_v.reshape(DEC_BATCH, PAST_LEN, LANES))
    x, xn, eid, gates, counts = _post(0, (a, o_p, o_s), xp0, xs0, mods, l0_w_out.astype(BF16), l0_norm2_g,
                                      l0_router_w, l0_router_b)
    x = _moe(x, xn, eid, gates, counts, mods, final_g, l0_up_w, l0_up_b, l0_down_w, l0_down_b, False)
    state_l0_k = st0[:, :LANES].reshape(BATCH, SEQ, WIN_KV, HEAD_DIM)
    state_l0_v = st0[:, LANES:].reshape(BATCH, SEQ, WIN_KV, HEAD_DIM)

    lam_init = 0.8 - 0.6 * math.exp(-0.3 * 1)
    mods = _modulation(cond16, l1_mod_w, l1_mod_b)
    h1, st1 = _inproj(x, x, mods, l1_norm1_g, l1_w_in.astype(BF16), rope_tabs, 1)
    lam_vecs = jnp.stack([l1_lam_q1, l1_lam_k1, l1_lam_q2, l1_lam_k2], axis=0)
    o_p = _diff_attn(h1, lam_vecs, l1_subln_g, lam_init, None)
    o_s = _diff_attn(h1, lam_vecs, l1_subln_g, lam_init,
                     (cache_l1_k.reshape(DEC_BATCH, PAST_LEN, D_MODEL),
                      cache_l1_v.reshape(DEC_BATCH, PAST_LEN, D_MODEL)))
    x, xn, eid, gates, counts = _post(1, (o_p, o_s), x, x, mods, l1_w_out.astype(BF16), l1_norm2_g,
                                      l1_router_w, l1_router_b)
    yp, ys = _moe(x, xn, eid, gates, counts, mods, final_g, l1_up_w, l1_up_b, l1_down_w, l1_down_b, True)
    state_l1_k = st1[:, :D_MODEL].reshape(BATCH, SEQ, DIFF_HEADS, 2, HEAD_DIM)
    state_l1_v = st1[:, D_MODEL:].reshape(BATCH, SEQ, DIFF_HEADS, DIFF_V)

    y_prompt = yp.reshape(BATCH, SEQ, D_MODEL)
    y_sample = ys.reshape(DEC_BATCH, DEC_SEQ, D_MODEL)
    return (y_prompt, y_sample, state_l0_k, state_l0_v, state_l1_k, state_l1_v)
```

```python
import functools
import math

import jax
import jax.numpy as jnp
from jax import lax
from jax.experimental import pallas as pl
from jax.experimental.pallas import tpu as pltpu

F32 = jnp.float32
BF16 = jnp.bfloat16
I32 = jnp.int32

D_MODEL = 1024
BATCH = 16
SEQ = 256
DEC_BATCH = 8
DEC_SEQ = 2048
PAST_LEN = 256
GRID_W = 64
HEAD_DIM = 64
BLOCK = 128
WINDOW = 128
ROPE_BASE = 10000.0
EPS = 1e-6
NEG = -1e30
ATTN_SCALE = HEAD_DIM ** -0.5
LOG2E = math.log2(math.e)
CONV_CH = 512
CONV_W = 31
WIN_HEADS = 8
WIN_KV = 2
WIN_G = 4
IN0_W = 1792
DIFF_HEADS = 8
DIFF_V = 128
IN1_W = 3072
N_EXPERTS = 32
TOP_K = 4
TOP_K_SHIFT = 2
D_FF = 1024
SWIGLU_LIMIT = 7.0
SWIGLU_ALPHA = 1.702

NP = BATCH * SEQ
NS = DEC_BATCH * DEC_SEQ
N = NP + NS
N_ASG = N * TOP_K

LANES = 128
SUBLANES = 8
ROW_TILES = D_MODEL // LANES
VMEM_LIMIT = 56 * 1024 * 1024

TM = 512
TMC = 512
TME = 256
CB = 256
HALO = 16
SHIFT_ROWS = CB + 2 * HALO - SUBLANES
TQ = 256
RT = 512
SCATTER_UNROLL = 64
MAX_EBLOCKS = (N_ASG + N_EXPERTS * (TME - 1)) // TME
SLOT_LEAD = 2 * TME
N_SLOTS = (MAX_EBLOCKS + 3) * TME

_NT = (((1,), (1,)), ((), ()))


def _cparams(sem):
    return pltpu.CompilerParams(dimension_semantics=sem, vmem_limit_bytes=VMEM_LIMIT)


def _mod_row(i, tm):
    npb = NP // tm
    return jnp.where(i < npb, DEC_BATCH, (i - npb) // (DEC_SEQ // tm))


def _mod_spec(j, tm):
    return pl.BlockSpec((1, 1, D_MODEL), lambda i, *_: (_mod_row(i, tm) * 6 + j, 0, 0))


def _const_spec(shape):
    nd = len(shape)
    return pl.BlockSpec(shape, lambda *_: (0,) * nd)


def _x_specs(unified, tm):
    npb = NP // tm
    off = npb if unified else 0
    return [pl.BlockSpec((tm, D_MODEL), lambda i, *_: (jnp.minimum(i, npb - 1), 0)),
            pl.BlockSpec((tm, D_MODEL), lambda i, *_: (jnp.maximum(i - npb, 0) + off, 0))]


def _pick_x(xa_ref, xb_ref, tm):
    return jnp.where(pl.program_id(0) < NP // tm, xa_ref[...], xb_ref[...])


def _ada_norm(x, g, shift, scale):
    ms = jnp.mean(x * x, axis=-1, keepdims=True)
    return (x * lax.rsqrt(ms + EPS) * g) * (1.0 + scale) + shift


def _mod_kernel(c_ref, w_ref, b_ref, o_ref):
    c = c_ref[...]
    s = c * jax.nn.sigmoid(c)
    o_ref[...] = jnp.dot(s.astype(BF16), w_ref[...].astype(BF16), preferred_element_type=F32) + b_ref[...]


def _modulation(cond16, w, b):
    m = pl.pallas_call(
        _mod_kernel,
        grid=(6,),
        in_specs=[_const_spec((16, D_MODEL)),
                  pl.BlockSpec((D_MODEL, D_MODEL), lambda j: (0, j)),
                  pl.BlockSpec((1, D_MODEL), lambda j: (0, j))],
        out_specs=pl.BlockSpec((16, D_MODEL), lambda j: (0, j)),
        out_shape=jax.ShapeDtypeStruct((16, 6 * D_MODEL), F32),
        compiler_params=_cparams(("arbitrary",)),
        name="modulation",
    )(cond16, w, b.reshape(1, -1))
    return m.reshape(16 * 6, 1, D_MODEL)


def _rope(v, cos, sa, sb):
    return v * cos + pltpu.roll(v, LANES - 16, 1) * sa + pltpu.roll(v, 16, 1) * sb


def _inproj_kernel(xa_ref, xb_ref, sh_ref, sc_ref, g_ref, w_ref, cos_ref, sa_ref, sb_ref, *outs, layer):
    i = pl.program_id(0)
    h = _ada_norm(_pick_x(xa_ref, xb_ref, TM), g_ref[...], sh_ref[0], sc_ref[0])
    acc = jnp.dot(h.astype(BF16), w_ref[...], preferred_element_type=F32)
    if layer == 0:
        ha_ref, hq_ref, st_ref = outs
        ha_ref[...] = acc[:, :2 * CONV_CH]
        base, n_q, n_rope, n_all, st_lo = 2 * CONV_CH, 4, 5, 6, 2 * CONV_CH + 512
        q_scale = ATTN_SCALE
    else:
        hq_ref, st_ref = outs
        base, n_q, n_rope, n_all, st_lo = 0, 8, 16, 24, 1024
        q_scale = ATTN_SCALE * LOG2E
    is_prompt = i < NP // TM

    def chunk(c):
        v = acc[:, base + c * LANES: base + (c + 1) * LANES]
        return v * q_scale if c < n_q else v

    @pl.when(is_prompt)
    def _():
        for c in range(n_all):
            hq_ref[:, c * LANES:(c + 1) * LANES] = chunk(c).astype(BF16)
        st_ref[...] = acc[:, st_lo:]

    @pl.when(jnp.logical_not(is_prompt))
    def _():
        cos, sa, sb = cos_ref[...], sa_ref[...], sb_ref[...]
        for c in range(n_all):
            v = chunk(c)
            if c < n_rope:
                v = _rope(v, cos, sa, sb)
            hq_ref[:, c * LANES:(c + 1) * LANES] = v.astype(BF16)


def _inproj(xa, xb, mods, g, w_bf, rope_tabs, layer):
    npb = NP // TM
    spb = DEC_SEQ // TM
    nout = w_bf.shape[1]
    rope_spec = pl.BlockSpec((TM, LANES), lambda i: (jnp.where(i < npb, 0, (i - npb) % spb), 0))
    st_w = 256 if layer == 0 else 2048
    st_spec = pl.BlockSpec((TM, st_w), lambda i: (jnp.minimum(i, npb - 1), 0))
    if layer == 0:
        out_shape = (jax.ShapeDtypeStruct((N, 2 * CONV_CH), F32),
                     jax.ShapeDtypeStruct((N, 768), BF16),
                     jax.ShapeDtypeStruct((NP, st_w), F32))
        out_specs = (pl.BlockSpec((TM, 2 * CONV_CH), lambda i: (i, 0)),
                     pl.BlockSpec((TM, 768), lambda i: (i, 0)), st_spec)
    else:
        out_shape = (jax.ShapeDtypeStruct((N, IN1_W), BF16),
                     jax.ShapeDtypeStruct((NP, st_w), F32))
        out_specs = (pl.BlockSpec((TM, IN1_W), lambda i: (i, 0)), st_spec)
    return pl.pallas_call(
        functools.partial(_inproj_kernel, layer=layer),
        grid=(N // TM,),
        in_specs=_x_specs(xa is xb, TM) + [
                  _mod_spec(0, TM), _mod_spec(1, TM),
                  _const_spec((1, D_MODEL)),
                  _const_spec((D_MODEL, nout)),
                  rope_spec, rope_spec, rope_spec],
        out_specs=out_specs,
        out_shape=out_shape,
        compiler_params=_cparams(("arbitrary",)),
        name="inproj_l%d" % layer,
    )(xa, xb, mods, mods, g.reshape(1, -1), w_bf, *rope_tabs)


def _conv_kernel(prev_ref, cur_ref, next_ref, w_ref, b_ref, g_ref, bb_ref, o_ref, upad, ush):
    i = pl.program_id(0)
    npb = NP // CB
    spb = DEC_SEQ // CB
    j = (i - npb) % spb
    first = jnp.logical_or(i < npb, j == 0)
    last = jnp.logical_or(i < npb, j == spb - 1)

    def glu(r):
        return r[:, :CONV_CH] * jax.nn.sigmoid(r[:, CONV_CH:])

    upad[HALO:HALO + CB, :] = glu(cur_ref[...])
    upad[0:HALO, :] = jnp.where(first, 0.0, glu(prev_ref[...]))
    upad[HALO + CB:, :] = jnp.where(last, 0.0, glu(next_ref[...]))
    acc = jnp.zeros((CB, CONV_CH), F32)
    off = HALO - CONV_W // 2
    for b in range(SUBLANES):
        taps = [t for t in range(CONV_W) if (off + t) % SUBLANES == b]
        if b > 0:
            ush[b - 1] = upad[b:b + SHIFT_ROWS, :]
        for t in taps:
            a8 = off + t - b
            win = upad[a8:a8 + CB, :] if b == 0 else ush[b - 1, a8:a8 + CB, :]
            acc = acc + w_ref[t:t + 1, :] * win
    u = acc + b_ref[...]
    mu = jnp.mean(u, axis=-1, keepdims=True)
    var = jnp.mean(jnp.square(u - mu), axis=-1, keepdims=True)
    y = (u - mu) * lax.rsqrt(var + EPS) * g_ref[...] + bb_ref[...]
    o_ref[...] = (y * jax.nn.sigmoid(y)).astype(BF16)


def _conv(ha, conv_w, conv_b, cg, cb):
    hb = CB // HALO
    nh = N // HALO
    w_pad = jnp.concatenate([conv_w, jnp.zeros((1, CONV_CH), F32)], axis=0)
    return pl.pallas_call(
        _conv_kernel,
        grid=(N // CB,),
        in_specs=[pl.BlockSpec((HALO, 2 * CONV_CH), lambda i: (jnp.maximum(i * hb - 1, 0), 0)),
                  pl.BlockSpec((CB, 2 * CONV_CH), lambda i: (i, 0)),
                  pl.BlockSpec((HALO, 2 * CONV_CH), lambda i: (jnp.minimum((i + 1) * hb, nh - 1), 0)),
                  _const_spec((CONV_W + 1, CONV_CH)),
                  _const_spec((1, CONV_CH)), _const_spec((1, CONV_CH)), _const_spec((1, CONV_CH))],
        out_specs=pl.BlockSpec((CB, CONV_CH), lambda i: (i, 0)),
        out_shape=jax.ShapeDtypeStruct((N, CONV_CH), BF16),
        scratch_shapes=[pltpu.VMEM((CB + 2 * HALO, CONV_CH), F32),
                        pltpu.VMEM((SUBLANES - 1, SHIFT_ROWS, CONV_CH), F32)],
        compiler_params=_cparams(("arbitrary",)),
        name="conformer_conv",
    )(ha, ha, ha, w_pad, conv_b.reshape(1, -1), cg.reshape(1, -1), cb.reshape(1, -1))


def _sink_attend(q, k, v, sink, mask):
    s = lax.dot_general(q, k, _NT, preferred_element_type=F32)
    if mask is not None:
        s = jnp.where(mask, s, NEG)
    m = jnp.maximum(jnp.max(s, axis=-1, keepdims=True), sink)
    p = jnp.exp(s - m)
    den = jnp.sum(p, axis=-1, keepdims=True) + jnp.exp(sink - m)
    return jnp.dot(p.astype(BF16), v, preferred_element_type=F32) / den


def _l0_prompt_attn_kernel(sink_ref, q_ref, k_ref, v_ref, o_ref):
    k = k_ref[...]
    v = v_ref[...]
    outs = []
    for h in range(WIN_HEADS):
        j = h // WIN_G
        outs.append(_sink_attend(q_ref[:, h * HEAD_DIM:(h + 1) * HEAD_DIM],
                                 k[:, j * HEAD_DIM:(j + 1) * HEAD_DIM],
                                 v[:, j * HEAD_DIM:(j + 1) * HEAD_DIM], sink_ref[h], None))
    o_ref[...] = jnp.concatenate(outs, axis=1).astype(BF16)


def _l0_prompt_attn(hq, sink):
    return pl.pallas_call(
        _l0_prompt_attn_kernel,
        grid=(BATCH,),
        in_specs=[pl.BlockSpec(memory_space=pltpu.SMEM),
                  pl.BlockSpec((SEQ, 512), lambda b: (b, 0)),
                  pl.BlockSpec((SEQ, LANES), lambda b: (b, 4)),
                  pl.BlockSpec((SEQ, LANES), lambda b: (b, 5))],
        out_specs=pl.BlockSpec((SEQ, 512), lambda b: (b, 0)),
        out_shape=jax.ShapeDtypeStruct((NP, 512), BF16),
        compiler_params=_cparams(("arbitrary",)),
        name="l0_prompt_attn",
    )(sink, hq, hq, hq)


def _l0_window_attn_kernel(sink_ref, q_ref, kp_ref, kc_ref, kn_ref, vp_ref, vc_ref, vn_ref,
                           ck_ref, cv_ref, o_ref):
    n = pl.program_id(1)
    k = jnp.concatenate([kp_ref[...], kc_ref[...], kn_ref[...], ck_ref[...].astype(BF16)], axis=0)
    v = jnp.concatenate([vp_ref[...], vc_ref[...], vn_ref[...], cv_ref[...].astype(BF16)], axis=0)
    qpos = n * BLOCK + lax.broadcasted_iota(I32, (BLOCK, 3 * BLOCK + PAST_LEN), 0)
    col = lax.broadcasted_iota(I32, (BLOCK, 3 * BLOCK + PAST_LEN), 1)
    kpos = (n - 1) * BLOCK + col
    local_ok = (jnp.abs(kpos - qpos) <= WINDOW) & (kpos >= 0) & (kpos < DEC_SEQ)
    ok = jnp.logical_or(col >= 3 * BLOCK, local_ok).astype(F32)
    mask = jnp.concatenate([ok] * WIN_G, axis=0) > 0.5
    grp = lax.broadcasted_iota(I32, (WIN_G * BLOCK, 1), 0) // BLOCK
    outs = []
    for j in range(WIN_KV):
        heads = range(j * WIN_G, (j + 1) * WIN_G)
        q4 = jnp.concatenate([q_ref[:, h * HEAD_DIM:(h + 1) * HEAD_DIM] for h in heads], axis=0)
        sink4 = jnp.zeros((WIN_G * BLOCK, 1), F32)
        for i, h in enumerate(heads):
            sink4 = jnp.where(grp == i, sink_ref[h], sink4)
        o4 = _sink_attend(q4, k[:, j * HEAD_DIM:(j + 1) * HEAD_DIM],
                          v[:, j * HEAD_DIM:(j + 1) * HEAD_DIM], sink4, mask)
        outs += [o4[i * BLOCK:(i + 1) * BLOCK, :] for i in range(WIN_G)]
    o_ref[...] = jnp.concatenate(outs, axis=1).astype(BF16)


def _l0_window_attn(hq, sink, ck, cv):
    nb = DEC_SEQ // BLOCK
    r0 = NP // BLOCK

    def kv_spec(col, d):
        return pl.BlockSpec((BLOCK, LANES),
                            lambda b, n: (r0 + b * nb + jnp.clip(n + d, 0, nb - 1), col))

    ctx_spec = pl.BlockSpec((None, PAST_LEN, LANES), lambda b, n: (b, 0, 0))
    return pl.pallas_call(
        _l0_window_attn_kernel,
        grid=(DEC_BATCH, nb),
        in_specs=[pl.BlockSpec(memory_space=pltpu.SMEM),
                  pl.BlockSpec((BLOCK, 512), lambda b, n: (r0 + b * nb + n, 0)),
                  kv_spec(4, -1), kv_spec(4, 0), kv_spec(4, 1),
                  kv_spec(5, -1), kv_spec(5, 0), kv_spec(5, 1),
                  ctx_spec, ctx_spec],
        out_specs=pl.BlockSpec((BLOCK, 512), lambda b, n: (b * nb + n, 0)),
        out_shape=jax.ShapeDtypeStruct((NS, 512), BF16),
        compiler_params=_cparams(("arbitrary", "arbitrary")),
        name="l0_window_attn",
    )(sink, hq, hq, hq, hq, hq, hq, hq, ck, cv)


def _diff_attn_kernel(*refs, lam_init, has_ctx):
    if has_ctx:
        q_ref, k_ref, v_ref, ck_ref, cv_ref, lam_ref, g_ref, o_ref, vaug, cvaug = refs
    else:
        q_ref, k_ref, v_ref, lam_ref, g_ref, o_ref, vaug = refs

    @pl.when(pl.program_id(1) == 0)
    def _():
        ones = jnp.ones((v_ref.shape[0], DIFF_V), BF16)
        for h in range(DIFF_HEADS):
            vaug[h, :, :DIFF_V] = v_ref[:, h * DIFF_V:(h + 1) * DIFF_V]
            vaug[h, :, DIFF_V:] = ones
        if has_ctx:
            cones = jnp.ones((PAST_LEN, DIFF_V), BF16)
            for h in range(DIFF_HEADS):
                cvaug[h, :, :DIFF_V] = cv_ref[:, h * DIFF_V:(h + 1) * DIFF_V].astype(BF16)
                cvaug[h, :, DIFF_V:] = cones

    lv = lam_ref[...]
    lam = (jnp.exp(jnp.sum(lv[0:1] * lv[1:2], axis=-1, keepdims=True))
           - jnp.exp(jnp.sum(lv[2:3] * lv[3:4], axis=-1, keepdims=True)) + lam_init)
    g = g_ref[...]
    outs = []
    for h in range(DIFF_HEADS):
        comps = []
        for c in range(2):
            lo = h * DIFF_V + c * HEAD_DIM
            q = q_ref[:, lo:lo + HEAD_DIM]
            s = lax.dot_general(q, k_ref[:, lo:lo + HEAD_DIM], _NT, preferred_element_type=F32)
            m = jnp.max(s, axis=-1, keepdims=True)
            if has_ctx:
                sc = lax.dot_general(q, ck_ref[:, lo:lo + HEAD_DIM].astype(BF16), _NT,
                                     preferred_element_type=F32)
                m = jnp.maximum(m, jnp.max(sc, axis=-1, keepdims=True))
            oa = jnp.dot(jnp.exp2(s - m).astype(BF16), vaug[h], preferred_element_type=F32)
            if has_ctx:
                oa = oa + jnp.dot(jnp.exp2(sc - m).astype(BF16), cvaug[h], preferred_element_type=F32)
            comps.append((oa[:, :DIFF_V], oa[:, DIFF_V:DIFF_V + 1]))
        (o0, d0), (o1, d1) = comps
        o = o0 / d0 - o1 * (lam / d1)
        ms = jnp.mean(o * o, axis=-1, keepdims=True)
        outs.append(((o * lax.rsqrt(ms + EPS)) * g) * (1.0 - lam_init))
    o_ref[...] = jnp.concatenate(outs, axis=1).astype(BF16)


def _diff_attn(h1, lam_vecs, subln_g, lam_init, ctx):
    g = subln_g.reshape(1, DIFF_V)
    small = [_const_spec((4, HEAD_DIM)), _const_spec((1, DIFF_V))]
    if ctx is None:
        grid = (BATCH, 1)
        in_specs = [pl.BlockSpec((SEQ, D_MODEL), lambda b, i: (b, 0)),
                    pl.BlockSpec((SEQ, D_MODEL), lambda b, i: (b, 1)),
                    pl.BlockSpec((SEQ, D_MODEL), lambda b, i: (b, 2))] + small
        out_specs = pl.BlockSpec((SEQ, D_MODEL), lambda b, i: (b, 0))
        rows = NP
        args = (h1, h1, h1, lam_vecs, g)
        name = "diff_attn_prompt"
    else:
        nq = DEC_SEQ // TQ
        q0 = NP // TQ
        s0 = NP // DEC_SEQ
        ctx_spec = pl.BlockSpec((None, PAST_LEN, D_MODEL), lambda b, i: (b, 0, 0))
        grid = (DEC_BATCH, nq)
        in_specs = [pl.BlockSpec((TQ, D_MODEL), lambda b, i: (q0 + b * nq + i, 0)),
                    pl.BlockSpec((DEC_SEQ, D_MODEL), lambda b, i: (s0 + b, 1)),
                    pl.BlockSpec((DEC_SEQ, D_MODEL), lambda b, i: (s0 + b, 2)),
                    ctx_spec, ctx_spec] + small
        out_specs = pl.BlockSpec((TQ, D_MODEL), lambda b, i: (b * nq + i, 0))
        rows = NS
        args = (h1, h1, h1, ctx[0], ctx[1], lam_vecs, g)
        name = "diff_attn_sample"
    tk = SEQ if ctx is None else DEC_SEQ
    scratch = [pltpu.VMEM((DIFF_HEADS, tk, 2 * DIFF_V), BF16)]
    if ctx is not None:
        scratch.append(pltpu.VMEM((DIFF_HEADS, PAST_LEN, 2 * DIFF_V), BF16))
    return pl.pallas_call(
        functools.partial(_diff_attn_kernel, lam_init=lam_init, has_ctx=ctx is not None),
        grid=grid, in_specs=in_specs, out_specs=out_specs,
        scratch_shapes=scratch,
        out_shape=jax.ShapeDtypeStruct((rows, D_MODEL), BF16),
        compiler_params=_cparams(("arbitrary", "arbitrary")),
        name=name,
    )(*args)


def _store_row_tiles(ref, val, rows):
    for s in range(ROW_TILES):
        ref[pl.ds(s, rows, stride=ROW_TILES), :] = val[:, s * LANES:(s + 1) * LANES]


def _post_kernel(*refs, layer):
    if layer == 0:
        (a_ref, op_ref, os_ref, xa_ref, xb_ref, g1_ref, sh2_ref, sc2_ref, w_ref, n2g_ref, rw_ref, rb_ref,
         xo_ref, xn_ref, eid_ref, gate_ref, cnt_ref) = refs
    else:
        (op_ref, os_ref, xa_ref, xb_ref, g1_ref, sh2_ref, sc2_ref, w_ref, n2g_ref, rw_ref, rb_ref,
         xo_ref, xn_ref, eid_ref, gate_ref, cnt_ref) = refs
    i = pl.program_id(0)
    o = jnp.where(i < NP // TM, op_ref[...], os_ref[...])
    if layer == 0:
        mix = (jnp.dot(a_ref[...], w_ref[:CONV_CH, :], preferred_element_type=F32)
               + jnp.dot(o, w_ref[CONV_CH:, :], preferred_element_type=F32))
    else:
        mix = jnp.dot(o, w_ref[...], preferred_element_type=F32)
    x1 = _pick_x(xa_ref, xb_ref, TM) + g1_ref[0] * mix
    xo_ref[...] = x1
    xn = _ada_norm(x1, n2g_ref[...], sh2_ref[0], sc2_ref[0])
    _store_row_tiles(xn_ref, xn, TM)
    rw = rw_ref[...]
    xh = xn.astype(BF16)
    xl = (xn - xh.astype(F32)).astype(BF16)
    wh = rw.astype(BF16)
    wl = (rw - wh.astype(F32)).astype(BF16)
    logits = (jnp.dot(xh, wh, preferred_element_type=F32)
              + (jnp.dot(xl, wh, preferred_element_type=F32)
                 + jnp.dot(xh, wl, preferred_element_type=F32))) + rb_ref[...]
    lane = lax.broadcasted_iota(I32, logits.shape, 1)
    vals, idxs = [], []
    picked = jnp.zeros(logits.shape, F32)
    for _ in range(TOP_K):
        m = jnp.max(logits, axis=-1, keepdims=True)
        idx = jnp.min(jnp.where(logits == m, lane, N_EXPERTS), axis=-1, keepdims=True)
        vals.append(m)
        idxs.append(idx)
        hit = lane == idx
        picked = picked + hit.astype(F32)
        logits = jnp.where(hit, -jnp.inf, logits)
    es = [jnp.exp(v - vals[0]) for v in vals]
    den = es[0] + es[1] + es[2] + es[3]
    eid_ref[...] = jnp.concatenate(idxs, axis=1)
    gate_ref[...] = jnp.concatenate([e / den for e in es], axis=1)

    @pl.when(i == 0)
    def _():
        cnt_ref[...] = jnp.zeros_like(cnt_ref)

    cnt_ref[...] += jnp.sum(picked, axis=0, keepdims=True)


def _post(layer, mix_parts, xa, xb, mods, w_bf, n2g, rw, rb):
    npb = NP // TM
    if layer == 0:
        a, o_p, o_s = mix_parts
        wo = 512
        mix_specs = [pl.BlockSpec((TM, CONV_CH), lambda i: (i, 0))]
        mix_args = [a, o_p, o_s]
    else:
        o_p, o_s = mix_parts
        wo = D_MODEL
        mix_specs = []
        mix_args = [o_p, o_s]
    mix_specs += [pl.BlockSpec((TM, wo), lambda i: (jnp.minimum(i, npb - 1), 0)),
                  pl.BlockSpec((TM, wo), lambda i: (jnp.maximum(i - npb, 0), 0))]
    return pl.pallas_call(
        functools.partial(_post_kernel, layer=layer),
        grid=(N // TM,),
        in_specs=mix_specs + _x_specs(xa is xb, TM) + [
                              _mod_spec(2, TM), _mod_spec(3, TM), _mod_spec(4, TM),
                              _const_spec((D_MODEL, D_MODEL)), _const_spec((1, D_MODEL)),
                              _const_spec((D_MODEL, N_EXPERTS)), _const_spec((1, N_EXPERTS))],
        out_specs=(pl.BlockSpec((TM, D_MODEL), lambda i: (i, 0)),
                   pl.BlockSpec((TM * ROW_TILES, LANES), lambda i: (i, 0)),
                   pl.BlockSpec((TM, TOP_K), lambda i: (i, 0)),
                   pl.BlockSpec((TM, TOP_K), lambda i: (i, 0)),
                   _const_spec((1, N_EXPERTS))),
        out_shape=(jax.ShapeDtypeStruct((N, D_MODEL), F32),
                   jax.ShapeDtypeStruct((N * ROW_TILES, LANES), F32),
                   jax.ShapeDtypeStruct((N, TOP_K), I32),
                   jax.ShapeDtypeStruct((N, TOP_K), F32),
                   jax.ShapeDtypeStruct((1, N_EXPERTS), F32)),
        compiler_params=_cparams(("arbitrary",)),
        name="post_l%d" % layer,
    )(*mix_args, xa, xb, mods, mods, mods, w_bf, n2g.reshape(1, -1), rw, rb.reshape(1, -1))


def _slots_kernel(cnt_ref, bst_ref, eidt_ref, base_ref, triu_ref, slot_ref, carry, dvm, *scratch):
    dsm = (scratch[:TOP_K], scratch[TOP_K:2 * TOP_K])
    sem = scratch[2 * TOP_K]
    j = pl.program_id(0)
    n_tiles = pl.num_programs(0) - 1

    def copies(par):
        return [pltpu.make_async_copy(dvm.at[par, k], dsm[par][k], sem.at[par, k]) for k in range(TOP_K)]

    @pl.when(j == 0)
    def _():
        carry[...] = jnp.zeros_like(carry)

    for par in range(2):
        @pl.when(jnp.logical_and(j < n_tiles, j % 2 == par))
        def _():
            e_iota = lax.broadcasted_iota(I32, (N_EXPERTS, RT), 0)
            ohs = [(eidt_ref[k:k + 1, :] == e_iota).astype(F32) for k in range(TOP_K)]
            ohsum = ohs[0] + ohs[1] + ohs[2] + ohs[3]
            cum = jnp.dot(ohsum.astype(BF16), triu_ref[...], preferred_element_type=F32)
            tot = cum + carry[...] + base_ref[...]
            dvm[par] = jnp.concatenate(
                [jnp.sum(oh * tot, axis=0, keepdims=True) for oh in ohs], axis=0).astype(I32)
            carry[...] += jnp.sum(ohsum, axis=1, keepdims=True)
            for cp in copies(par):
                cp.start()

    for par in range(2):
        @pl.when(jnp.logical_and(j > 0, (j - 1) % 2 == par))
        def _():
            for cp in copies(par):
                cp.wait()

            def scatter(c, carry_):
                t0 = c * SCATTER_UNROLL
                a0 = ((j - 1) * RT + t0) * TOP_K
                for u in range(SCATTER_UNROLL):
                    for k in range(TOP_K):
                        slot_ref[dsm[par][k][t0 + u]] = a0 + (u * TOP_K + k)
                return carry_

            lax.fori_loop(0, RT // SCATTER_UNROLL, scatter, 0)

    @pl.when(j == pl.num_programs(0) - 1)
    def _():
        def pad(s, c):
            slot_ref[s] = (N + s % SLOT_LEAD) * TOP_K + (TOP_K - 1)
            return c

        def per_expert(e, c):
            cnt = cnt_ref[e]
            b0 = bst_ref[e]
            lax.fori_loop(SLOT_LEAD + b0 * TME + cnt,
                          SLOT_LEAD + (b0 + (cnt + TME - 1) // TME) * TME, pad, 0)
            return c

        lax.fori_loop(0, SLOT_LEAD, pad, 0)
        lax.fori_loop(0, N_EXPERTS, per_expert, 0)
        lax.fori_loop(SLOT_LEAD + bst_ref[N_EXPERTS] * TME, N_SLOTS, pad, 0)


def _route(eid, counts_f):
    counts = counts_f.reshape(-1).astype(I32)
    nblk = (counts + TME - 1) // TME
    bend = jnp.cumsum(nblk)
    bstart = bend - nblk
    bst = jnp.concatenate([bstart, bend[-1:]]).astype(I32)
    base = (bstart * TME + SLOT_LEAD).astype(F32).reshape(N_EXPERTS, 1)
    ar = jnp.arange(RT, dtype=I32)
    triu = (ar[:, None] < ar[None, :]).astype(BF16)
    grid_spec = pltpu.PrefetchScalarGridSpec(
        num_scalar_prefetch=2,
        grid=(N // RT + 1,),
        in_specs=[pl.BlockSpec((TOP_K, RT), lambda j, c, b: (0, jnp.minimum(j, N // RT - 1))),
                  _const_spec((N_EXPERTS, 1)),
                  _const_spec((RT, RT))],
        out_specs=pl.BlockSpec(memory_space=pltpu.SMEM),
        scratch_shapes=[pltpu.VMEM((N_EXPERTS, 1), F32),
                        pltpu.VMEM((2, TOP_K, RT), I32)]
        + [pltpu.SMEM((RT,), I32)] * (2 * TOP_K)
        + [pltpu.SemaphoreType.DMA((2, TOP_K))],
    )
    slot_asg = pl.pallas_call(
        _slots_kernel,
        grid_spec=grid_spec,
        out_shape=jax.ShapeDtypeStruct((N_SLOTS,), I32),
        compiler_params=_cparams(("arbitrary",)),
        name="slots",
    )(counts, bst, eid.T, base, triu)
    return slot_asg, bst, counts


def _row_tile(idx):
    if isinstance(idx, int):
        return pl.ds(idx * ROW_TILES, ROW_TILES)
    return pl.ds(pl.multiple_of(idx * ROW_TILES, ROW_TILES), ROW_TILES)


def _expert_kernel(slot_ref, bst_ref, cnt_ref, xn_hbm, upw_ref, upb_ref, dww_ref, dwb_ref,
                   y_hbm, gbuf0, gbuf1, sbuf0, sbuf1, upbf, dwbf, xb_ref, hdn_ref, gsem, ssem):
    e = pl.program_id(0)
    nb = (cnt_ref[e] + TME - 1) // TME
    b0 = bst_ref[e]
    gbufs = (gbuf0, gbuf1)
    sbufs = (sbuf0, sbuf1)

    def gather_row(g, buf, r):
        v = slot_ref[(g + 2) * TME + r]
        tok = jnp.minimum(lax.shift_right_logical(v, TOP_K_SHIFT), N - 1)
        pltpu.make_async_copy(xn_hbm.at[_row_tile(tok), :], gbufs[buf].at[_row_tile(r), :],
                              gsem.at[buf]).start()

    def scatter_row(g, buf, r):
        v = slot_ref[(g + 2) * TME + r]
        dst = (v & (TOP_K - 1)) * N + lax.shift_right_logical(v, TOP_K_SHIFT)
        pltpu.make_async_copy(sbufs[buf].at[_row_tile(r), :], y_hbm.at[_row_tile(dst), :],
                              ssem.at[buf]).start()

    def rolled(row_fn, g, buf):
        def body(r, c):
            row_fn(g, buf, r)
            return c
        lax.fori_loop(0, TME, body, 0)

    def wait_gather(buf):
        pltpu.make_async_copy(xn_hbm.at[pl.ds(0, TME * ROW_TILES), :], gbufs[buf], gsem.at[buf]).wait()

    def wait_scatter(buf):
        pltpu.make_async_copy(sbufs[buf], y_hbm.at[pl.ds(0, TME * ROW_TILES), :], ssem.at[buf]).wait()

    @pl.when(e == 0)
    def _():
        sbuf0[...] = jnp.zeros_like(sbuf0)
        sbuf1[...] = jnp.zeros_like(sbuf1)
        rolled(gather_row, 0, 0)
        rolled(scatter_row, -2, 0)

    @pl.when(nb > 0)
    def _():
        upbf[...] = upw_ref[0].astype(BF16)
        dwbf[...] = dww_ref[0].astype(BF16)

    def work(g, cur):
        oth = 1 - cur
        wait_gather(cur)
        xb_ref[...] = jnp.concatenate(
            [gbufs[cur][pl.ds(s, TME, stride=ROW_TILES), :] for s in range(ROW_TILES)], axis=1).astype(BF16)

        def phase(ph, c):
            @pl.when(ph == 0)
            def _():
                for r in range(TME):
                    gather_row(g + 1, oth, r)
                gu = jnp.dot(xb_ref[...], upbf[...], preferred_element_type=F32) + upb_ref[0]
                gg = jnp.minimum(gu[:, :D_FF], SWIGLU_LIMIT)
                lin = jnp.clip(gu[:, D_FF:], -SWIGLU_LIMIT, SWIGLU_LIMIT)
                hdn_ref[...] = (gg * jax.nn.sigmoid(SWIGLU_ALPHA * gg) * (lin + 1.0)).astype(BF16)

            @pl.when(ph == 1)
            def _():
                for r in range(TME):
                    scatter_row(g - 1, oth, r)
                y = jnp.dot(hdn_ref[...], dwbf[...], preferred_element_type=F32) + dwb_ref[0]
                wait_scatter(cur)
                for s in range(ROW_TILES):
                    sbufs[cur][pl.ds(s, TME, stride=ROW_TILES), :] = y[:, s * LANES:(s + 1) * LANES]
            return c

        lax.fori_loop(0, 2, phase, 0)

    def block(j, carry):
        g = b0 + j
        for parity in range(2):
            @pl.when(g % 2 == parity)
            def _():
                work(g, parity)
        return carry

    lax.fori_loop(0, nb, block, 0)

    @pl.when(e == pl.num_programs(0) - 1)
    def _():
        g_end = bst_ref[N_EXPERTS]
        for parity in range(2):
            @pl.when(g_end % 2 == parity)
            def _():
                rolled(scatter_row, g_end - 1, 1 - parity)
                wait_gather(parity)
        wait_scatter(0)
        wait_scatter(1)


def _experts(slot_asg, bst, counts, xn_tiles, up_w, up_b, down_w, down_b):
    grid_spec = pltpu.PrefetchScalarGridSpec(
        num_scalar_prefetch=3,
        grid=(N_EXPERTS,),
        in_specs=[pl.BlockSpec(memory_space=pl.ANY),
                  pl.BlockSpec((1, D_MODEL, 2 * D_FF), lambda e, *_: (e, 0, 0)),
                  pl.BlockSpec((1, 1, 2 * D_FF), lambda e, *_: (e, 0, 0)),
                  pl.BlockSpec((1, D_FF, D_MODEL), lambda e, *_: (e, 0, 0)),
                  pl.BlockSpec((1, 1, D_MODEL), lambda e, *_: (e, 0, 0))],
        out_specs=pl.BlockSpec(memory_space=pl.ANY),
        scratch_shapes=[pltpu.VMEM((TME * ROW_TILES, LANES), F32),
                        pltpu.VMEM((TME * ROW_TILES, LANES), F32),
                        pltpu.VMEM((TME * ROW_TILES, LANES), F32),
                        pltpu.VMEM((TME * ROW_TILES, LANES), F32),
                        pltpu.VMEM((D_MODEL, 2 * D_FF), BF16),
                        pltpu.VMEM((D_FF, D_MODEL), BF16),
                        pltpu.VMEM((TME, D_MODEL), BF16),
                        pltpu.VMEM((TME, D_FF), BF16),
                        pltpu.SemaphoreType.DMA((2,)),
                        pltpu.SemaphoreType.DMA((2,))],
    )
    return pl.pallas_call(
        _expert_kernel,
        grid_spec=grid_spec,
        out_shape=jax.ShapeDtypeStruct(((N_ASG + SLOT_LEAD) * ROW_TILES, LANES), F32),
        compiler_params=_cparams(("arbitrary",)),
        name="experts",
    )(slot_asg, bst, counts, xn_tiles, up_w, up_b.reshape(N_EXPERTS, 1, -1),
      down_w, down_b.reshape(N_EXPERTS, 1, -1))


def _combine_kernel(x_ref, gate_ref, g2_ref, fg_ref, y0_ref, y1_ref, y2_ref, y3_ref, *o_refs, final):
    gate = gate_ref[...]
    cols = []
    for s in range(ROW_TILES):
        acc = None
        for k, y_ref in enumerate((y0_ref, y1_ref, y2_ref, y3_ref)):
            term = gate[:, k:k + 1] * y_ref[pl.ds(s, TMC, stride=ROW_TILES), :]
            acc = term if acc is None else acc + term
        cols.append(acc)
    x2 = x_ref[...] + g2_ref[0] * jnp.concatenate(cols, axis=1)
    if not final:
        o_refs[0][...] = x2
        return
    ms = jnp.mean(x2 * x2, axis=-1, keepdims=True)
    y = x2 * lax.rsqrt(ms + EPS) * fg_ref[...]
    is_prompt = pl.program_id(0) < NP // TMC

    @pl.when(is_prompt)
    def _():
        o_refs[0][...] = y

    @pl.when(jnp.logical_not(is_prompt))
    def _():
        o_refs[1][...] = y


def _combine(x, gates, mods, final_g, y_tiles, final):
    nblk = N // TMC

    npb = NP // TMC

    def y_spec(k):
        return pl.BlockSpec((TMC * ROW_TILES, LANES), lambda i: (k * nblk + i, 0))

    if final:
        out_specs = (pl.BlockSpec((TMC, D_MODEL), lambda i: (jnp.minimum(i, npb - 1), 0)),
                     pl.BlockSpec((TMC, D_MODEL), lambda i: (jnp.maximum(i - npb, 0), 0)))
        out_shape = (jax.ShapeDtypeStruct((NP, D_MODEL), F32), jax.ShapeDtypeStruct((NS, D_MODEL), F32))
    else:
        out_specs = pl.BlockSpec((TMC, D_MODEL), lambda i: (i, 0))
        out_shape = jax.ShapeDtypeStruct((N, D_MODEL), F32)

    return pl.pallas_call(
        functools.partial(_combine_kernel, final=final),
        grid=(nblk,),
        in_specs=[pl.BlockSpec((TMC, D_MODEL), lambda i: (i, 0)),
                  pl.BlockSpec((TMC, TOP_K), lambda i: (i, 0)),
                  _mod_spec(5, TMC),
                  _const_spec((1, D_MODEL)),
                  y_spec(0), y_spec(1), y_spec(2), y_spec(3)],
        out_specs=out_specs,
        out_shape=out_shape,
        compiler_params=_cparams(("arbitrary",)),
        name="combine",
    )(x, gates, mods, final_g.reshape(1, -1), y_tiles, y_tiles, y_tiles, y_tiles)


def _rope_tables():
    pos = jnp.arange(DEC_SEQ)
    r = (pos // GRID_W).astype(F32)
    col = (pos % GRID_W).astype(F32)
    quarter = HEAD_DIM // 4
    inv = ROPE_BASE ** (-jnp.arange(quarter, dtype=F32) / quarter)
    ar = r[:, None] * inv
    ac = col[:, None] * inv
    ang = jnp.concatenate([ar, ar, ac, ac], axis=-1)
    cos = jnp.tile(jnp.cos(ang), (1, LANES // HEAD_DIM))
    sin = jnp.tile(jnp.sin(ang), (1, LANES // HEAD_DIM))
    first = (jnp.arange(LANES) % 32) < 16
    sa = jnp.where(first[None, :], -sin, 0.0)
    sb = jnp.where(first[None, :], 0.0, sin)
    return cos, sa, sb


def _moe(x, xn, eid, gates, counts, mods, final_g, up_w, up_b, down_w, down_b, final):
    slot_asg, bst, counts_i = _route(eid, counts)
    y = _experts(slot_asg, bst, counts_i, xn, up_w, up_b, down_w, down_b)
    return _combine(x, gates, mods, final_g, y, final)


@jax.jit
def kernel(x_prompt, x_sample, cache_l0_k, cache_l0_v, cache_l1_k, cache_l1_v, c, c_ctx, final_g,
           l0_mod_w, l0_mod_b, l0_norm1_g, l0_w_in, l0_conv_w, l0_conv_b, l0_cnorm_g, l0_cnorm_b,
           l0_sink, l0_w_out, l0_norm2_g, l0_router_w, l0_router_b, l0_up_w, l0_up_b, l0_down_w,
           l0_down_b,
           l1_mod_w, l1_mod_b, l1_norm1_g, l1_w_in, l1_lam_q1, l1_lam_k1, l1_lam_q2, l1_lam_k2,
           l1_subln_g, l1_w_out, l1_norm2_g, l1_router_w, l1_router_b, l1_up_w, l1_up_b, l1_down_w,
           l1_down_b):
    xp0 = x_prompt.reshape(NP, D_MODEL)
    xs0 = x_sample.reshape(NS, D_MODEL)
    cond16 = jnp.concatenate([c, c_ctx[None, :], jnp.zeros((16 - DEC_BATCH - 1, D_MODEL), F32)], axis=0)
    rope_tabs = _rope_tables()

    mods = _modulation(cond16, l0_mod_w, l0_mod_b)
    ha, hq, st0 = _inproj(xp0, xs0, mods, l0_norm1_g, l0_w_in.astype(BF16), rope_tabs, 0)
    a = _conv(ha, l0_conv_w, l0_conv_b, l0_cnorm_g, l0_cnorm_b)
    o_p = _l0_prompt_attn(hq, l0_sink)
    o_s = _l0_window_attn(hq, l0_sink, cache_l0_k.reshape(DEC_BATCH, PAST_LEN, LANES),
                          cache_l0_v.reshape(DEC_BATCH, PAST_LEN, LANES))
    x, xn, eid, gates, counts = _post(0, (a, o_p, o_s), xp0, xs0, mods, l0_w_out.astype(BF16), l0_norm2_g,
                                      l0_router_w, l0_router_b)
    x = _moe(x, xn, eid, gates, counts, mods, final_g, l0_up_w, l0_up_b, l0_down_w, l0_down_b, False)
    state_l0_k = st0[:, :LANES].reshape(BATCH, SEQ, WIN_KV, HEAD_DIM)
    state_l0_v = st0[:, LANES:].reshape(BATCH, SEQ, WIN_KV, HEAD_DIM)

    lam_init = 0.8 - 0.6 * math.exp(-0.3 * 1)
    mods = _modulation(cond16, l1_mod_w, l1_mod_b)
    h1, st1 = _inproj(x, x, mods, l1_norm1_g, l1_w_in.astype(BF16), rope_tabs, 1)
    lam_vecs = jnp.stack([l1_lam_q1, l1_lam_k1, l1_lam_q2, l1_lam_k2], axis=0)
    o_p = _diff_attn(h1, lam_vecs, l1_subln_g, lam_init, None)
    o_s = _diff_attn(h1, lam_vecs, l1_subln_g, lam_init,
                     (cache_l1_k.reshape(DEC_BATCH, PAST_LEN, D_MODEL),
                      cache_l1_v.reshape(DEC_BATCH, PAST_LEN, D_MODEL)))
    x, xn, eid, gates, counts = _post(1, (o_p, o_s), x, x, mods, l1_w_out.astype(BF16), l1_norm2_g,
                                      l1_router_w, l1_router_b)
    yp, ys = _moe(x, xn, eid, gates, counts, mods, final_g, l1_up_w, l1_up_b, l1_down_w, l1_down_b, True)
    state_l1_k = st1[:, :D_MODEL].reshape(BATCH, SEQ, DIFF_HEADS, 2, HEAD_DIM)
    state_l1_v = st1[:, D_MODEL:].reshape(BATCH, SEQ, DIFF_HEADS, DIFF_V)

    y_prompt = yp.reshape(BATCH, SEQ, D_MODEL)
    y_sample = ys.reshape(DEC_BATCH, DEC_SEQ, D_MODEL)
    return (y_prompt, y_sample, state_l0_k, state_l0_v, state_l1_k, state_l1_v)
```

```python
import functools
import math

import jax
import jax.numpy as jnp
from jax import lax
from jax.experimental import pallas as pl
from jax.experimental.pallas import tpu as pltpu

F32 = jnp.float32
BF16 = jnp.bfloat16
I32 = jnp.int32

D_MODEL = 1024
BATCH = 16
SEQ = 256
DEC_BATCH = 8
DEC_SEQ = 2048
PAST_LEN = 256
GRID_W = 64
HEAD_DIM = 64
BLOCK = 128
WINDOW = 128
ROPE_BASE = 10000.0
EPS = 1e-6
NEG = -1e30
ATTN_SCALE = HEAD_DIM ** -0.5
LOG2E = math.log2(math.e)
CONV_CH = 512
CONV_W = 31
WIN_HEADS = 8
WIN_KV = 2
WIN_G = 4
IN0_W = 1792
DIFF_HEADS = 8
DIFF_V = 128
IN1_W = 3072
N_EXPERTS = 32
TOP_K = 4
TOP_K_SHIFT = 2
D_FF = 1024
SWIGLU_LIMIT = 7.0
SWIGLU_ALPHA = 1.702

NP = BATCH * SEQ
NS = DEC_BATCH * DEC_SEQ
N = NP + NS
N_ASG = N * TOP_K

LANES = 128
SUBLANES = 8
ROW_TILES = D_MODEL // LANES
VMEM_LIMIT = 56 * 1024 * 1024

TM = 512
TMC = 512
TME = 256
CB = 256
HALO = 16
SHIFT_ROWS = CB + 2 * HALO - SUBLANES
TQ = 256
RT = 512
SCATTER_UNROLL = 64
MAX_EBLOCKS = (N_ASG + N_EXPERTS * (TME - 1)) // TME
SLOT_LEAD = 2 * TME
N_SLOTS = (MAX_EBLOCKS + 3) * TME

_NT = (((1,), (1,)), ((), ()))


def _cparams(sem):
    return pltpu.CompilerParams(dimension_semantics=sem, vmem_limit_bytes=VMEM_LIMIT)


def _mod_row(i, tm):
    npb = NP // tm
    return jnp.where(i < npb, DEC_BATCH, (i - npb) // (DEC_SEQ // tm))


def _mod_spec(j, tm):
    return pl.BlockSpec((1, 1, D_MODEL), lambda i, *_: (_mod_row(i, tm) * 6 + j, 0, 0))


def _const_spec(shape):
    nd = len(shape)
    return pl.BlockSpec(shape, lambda *_: (0,) * nd)


def _x_specs(unified, tm):
    npb = NP // tm
    off = npb if unified else 0
    return [pl.BlockSpec((tm, D_MODEL), lambda i, *_: (jnp.minimum(i, npb - 1), 0)),
            pl.BlockSpec((tm, D_MODEL), lambda i, *_: (jnp.maximum(i - npb, 0) + off, 0))]


def _pick_x(xa_ref, xb_ref, tm):
    return jnp.where(pl.program_id(0) < NP // tm, xa_ref[...], xb_ref[...])


def _ada_norm(x, g, shift, scale):
    ms = jnp.mean(x * x, axis=-1, keepdims=True)
    return (x * lax.rsqrt(ms + EPS) * g) * (1.0 + scale) + shift


def _mod_kernel(c_ref, w_ref, b_ref, o_ref):
    c = c_ref[...]
    s = c * jax.nn.sigmoid(c)
    o_ref[...] = jnp.dot(s.astype(BF16), w_ref[...].astype(BF16), preferred_element_type=F32) + b_ref[...]


def _modulation(cond16, w, b):
    m = pl.pallas_call(
        _mod_kernel,
        grid=(6,),
        in_specs=[_const_spec((16, D_MODEL)),
                  pl.BlockSpec((D_MODEL, D_MODEL), lambda j: (0, j)),
                  pl.BlockSpec((1, D_MODEL), lambda j: (0, j))],
        out_specs=pl.BlockSpec((16, D_MODEL), lambda j: (0, j)),
        out_shape=jax.ShapeDtypeStruct((16, 6 * D_MODEL), F32),
        compiler_params=_cparams(("arbitrary",)),
        name="modulation",
    )(cond16, w, b.reshape(1, -1))
    return m.reshape(16 * 6, 1, D_MODEL)


def _rope(v, cos, sa, sb):
    return v * cos + pltpu.roll(v, LANES - 16, 1) * sa + pltpu.roll(v, 16, 1) * sb


def _inproj_kernel(xa_ref, xb_ref, sh_ref, sc_ref, g_ref, w_ref, cos_ref, sa_ref, sb_ref, *outs, layer):
    i = pl.program_id(0)
    h = _ada_norm(_pick_x(xa_ref, xb_ref, TM), g_ref[...], sh_ref[0], sc_ref[0])
    acc = jnp.dot(h.astype(BF16), w_ref[...], preferred_element_type=F32)
    if layer == 0:
        ha_ref, hq_ref, st_ref = outs
        ha_ref[...] = acc[:, :2 * CONV_CH]
        base, n_q, n_rope, n_all, st_lo = 2 * CONV_CH, 4, 5, 6, 2 * CONV_CH + 512
        q_scale = ATTN_SCALE
    else:
        hq_ref, st_ref = outs
        base, n_q, n_rope, n_all, st_lo = 0, 8, 16, 24, 1024
        q_scale = ATTN_SCALE * LOG2E
    is_prompt = i < NP // TM

    def chunk(c):
        v = acc[:, base + c * LANES: base + (c + 1) * LANES]
        return v * q_scale if c < n_q else v

    @pl.when(is_prompt)
    def _():
        for c in range(n_all):
            hq_ref[:, c * LANES:(c + 1) * LANES] = chunk(c).astype(BF16)
        st_ref[...] = acc[:, st_lo:]

    @pl.when(jnp.logical_not(is_prompt))
    def _():
        cos, sa, sb = cos_ref[...], sa_ref[...], sb_ref[...]
        for c in range(n_all):
            v = chunk(c)
            if c < n_rope:
                v = _rope(v, cos, sa, sb)
            hq_ref[:, c * LANES:(c + 1) * LANES] = v.astype(BF16)


def _inproj(xa, xb, mods, g, w_bf, rope_tabs, layer):
    npb = NP // TM
    spb = DEC_SEQ // TM
    nout = w_bf.shape[1]
    rope_spec = pl.BlockSpec((TM, LANES), lambda i: (jnp.where(i < npb, 0, (i - npb) % spb), 0))
    st_w = 256 if layer == 0 else 2048
    st_spec = pl.BlockSpec((TM, st_w), lambda i: (jnp.minimum(i, npb - 1), 0))
    if layer == 0:
        out_shape = (jax.ShapeDtypeStruct((N, 2 * CONV_CH), F32),
                     jax.ShapeDtypeStruct((N, 768), BF16),
                     jax.ShapeDtypeStruct((NP, st_w), F32))
        out_specs = (pl.BlockSpec((TM, 2 * CONV_CH), lambda i: (i, 0)),
                     pl.BlockSpec((TM, 768), lambda i: (i, 0)), st_spec)
    else:
        out_shape = (jax.ShapeDtypeStruct((N, IN1_W), BF16),
                     jax.ShapeDtypeStruct((NP, st_w), F32))
        out_specs = (pl.BlockSpec((TM, IN1_W), lambda i: (i, 0)), st_spec)
    return pl.pallas_call(
        functools.partial(_inproj_kernel, layer=layer),
        grid=(N // TM,),
        in_specs=_x_specs(xa is xb, TM) + [
                  _mod_spec(0, TM), _mod_spec(1, TM),
                  _const_spec((1, D_MODEL)),
                  _const_spec((D_MODEL, nout)),
                  rope_spec, rope_spec, rope_spec],
        out_specs=out_specs,
        out_shape=out_shape,
        compiler_params=_cparams(("arbitrary",)),
        name="inproj_l%d" % layer,
    )(xa, xb, mods, mods, g.reshape(1, -1), w_bf, *rope_tabs)


def _conv_kernel(prev_ref, cur_ref, next_ref, w_ref, b_ref, g_ref, bb_ref, o_ref, upad, ush):
    i = pl.program_id(0)
    npb = NP // CB
    spb = DEC_SEQ // CB
    j = (i - npb) % spb
    first = jnp.logical_or(i < npb, j == 0)
    last = jnp.logical_or(i < npb, j == spb - 1)

    def glu(r):
        return r[:, :CONV_CH] * jax.nn.sigmoid(r[:, CONV_CH:])

    upad[HALO:HALO + CB, :] = glu(cur_ref[...])
    upad[0:HALO, :] = jnp.where(first, 0.0, glu(prev_ref[...]))
    upad[HALO + CB:, :] = jnp.where(last, 0.0, glu(next_ref[...]))
    acc = jnp.zeros((CB, CONV_CH), F32)
    off = HALO - CONV_W // 2
    for b in range(SUBLANES):
        taps = [t for t in range(CONV_W) if (off + t) % SUBLANES == b]
        if b > 0:
            ush[b - 1] = upad[b:b + SHIFT_ROWS, :]
        for t in taps:
            a8 = off + t - b
            win = upad[a8:a8 + CB, :] if b == 0 else ush[b - 1, a8:a8 + CB, :]
            acc = acc + w_ref[t:t + 1, :] * win
    u = acc + b_ref[...]
    mu = jnp.mean(u, axis=-1, keepdims=True)
    var = jnp.mean(jnp.square(u - mu), axis=-1, keepdims=True)
    y = (u - mu) * lax.rsqrt(var + EPS) * g_ref[...] + bb_ref[...]
    o_ref[...] = (y * jax.nn.sigmoid(y)).astype(BF16)


def _conv(ha, conv_w, conv_b, cg, cb):
    hb = CB // HALO
    nh = N // HALO
    w_pad = jnp.concatenate([conv_w, jnp.zeros((1, CONV_CH), F32)], axis=0)
    return pl.pallas_call(
        _conv_kernel,
        grid=(N // CB,),
        in_specs=[pl.BlockSpec((HALO, 2 * CONV_CH), lambda i: (jnp.maximum(i * hb - 1, 0), 0)),
                  pl.BlockSpec((CB, 2 * CONV_CH), lambda i: (i, 0)),
                  pl.BlockSpec((HALO, 2 * CONV_CH), lambda i: (jnp.minimum((i + 1) * hb, nh - 1), 0)),
                  _const_spec((CONV_W + 1, CONV_CH)),
                  _const_spec((1, CONV_CH)), _const_spec((1, CONV_CH)), _const_spec((1, CONV_CH))],
        out_specs=pl.BlockSpec((CB, CONV_CH), lambda i: (i, 0)),
        out_shape=jax.ShapeDtypeStruct((N, CONV_CH), BF16),
        scratch_shapes=[pltpu.VMEM((CB + 2 * HALO, CONV_CH), F32),
                        pltpu.VMEM((SUBLANES - 1, SHIFT_ROWS, CONV_CH), F32)],
        compiler_params=_cparams(("arbitrary",)),
        name="conformer_conv",
    )(ha, ha, ha, w_pad, conv_b.reshape(1, -1), cg.reshape(1, -1), cb.reshape(1, -1))


def _sink_attend(q, k, v, sink, mask):
    s = lax.dot_general(q, k, _NT, preferred_element_type=F32)
    if mask is not None:
        s = jnp.where(mask, s, NEG)
    m = jnp.maximum(jnp.max(s, axis=-1, keepdims=True), sink)
    p = jnp.exp(s - m)
    den = jnp.sum(p, axis=-1, keepdims=True) + jnp.exp(sink - m)
    return jnp.dot(p.astype(BF16), v, preferred_element_type=F32) / den


def _l0_prompt_attn_kernel(sink_ref, q_ref, k_ref, v_ref, o_ref):
    k = k_ref[...]
    v = v_ref[...]
    outs = []
    for h in range(WIN_HEADS):
        j = h // WIN_G
        outs.append(_sink_attend(q_ref[:, h * HEAD_DIM:(h + 1) * HEAD_DIM],
                                 k[:, j * HEAD_DIM:(j + 1) * HEAD_DIM],
                                 v[:, j * HEAD_DIM:(j + 1) * HEAD_DIM], sink_ref[h], None))
    o_ref[...] = jnp.concatenate(outs, axis=1).astype(BF16)


def _l0_prompt_attn(hq, sink):
    return pl.pallas_call(
        _l0_prompt_attn_kernel,
        grid=(BATCH,),
        in_specs=[pl.BlockSpec(memory_space=pltpu.SMEM),
                  pl.BlockSpec((SEQ, 512), lambda b: (b, 0)),
                  pl.BlockSpec((SEQ, LANES), lambda b: (b, 4)),
                  pl.BlockSpec((SEQ, LANES), lambda b: (b, 5))],
        out_specs=pl.BlockSpec((SEQ, 512), lambda b: (b, 0)),
        out_shape=jax.ShapeDtypeStruct((NP, 512), BF16),
        compiler_params=_cparams(("arbitrary",)),
        name="l0_prompt_attn",
    )(sink, hq, hq, hq)


def _l0_window_attn_kernel(sink_ref, q_ref, kp_ref, kc_ref, kn_ref, vp_ref, vc_ref, vn_ref,
                           ck_ref, cv_ref, o_ref):
    n = pl.program_id(1)
    k = jnp.concatenate([kp_ref[...], kc_ref[...], kn_ref[...], ck_ref[...].astype(BF16)], axis=0)
    v = jnp.concatenate([vp_ref[...], vc_ref[...], vn_ref[...], cv_ref[...].astype(BF16)], axis=0)
    qpos = n * BLOCK + lax.broadcasted_iota(I32, (BLOCK, 3 * BLOCK + PAST_LEN), 0)
    col = lax.broadcasted_iota(I32, (BLOCK, 3 * BLOCK + PAST_LEN), 1)
    kpos = (n - 1) * BLOCK + col
    local_ok = (jnp.abs(kpos - qpos) <= WINDOW) & (kpos >= 0) & (kpos < DEC_SEQ)
    ok = jnp.logical_or(col >= 3 * BLOCK, local_ok).astype(F32)
    mask = jnp.concatenate([ok] * WIN_G, axis=0) > 0.5
    grp = lax.broadcasted_iota(I32, (WIN_G * BLOCK, 1), 0) // BLOCK
    outs = []
    for j in range(WIN_KV):
        heads = range(j * WIN_G, (j + 1) * WIN_G)
        q4 = jnp.concatenate([q_ref[:, h * HEAD_DIM:(h + 1) * HEAD_DIM] for h in heads], axis=0)
        sink4 = jnp.zeros((WIN_G * BLOCK, 1), F32)
        for i, h in enumerate(heads):
            sink4 = jnp.where(grp == i, sink_ref[h], sink4)
        o4 = _sink_attend(q4, k[:, j * HEAD_DIM:(j + 1) * HEAD_DIM],
                          v[:, j * HEAD_DIM:(j + 1) * HEAD_DIM], sink4, mask)
        outs += [o4[i * BLOCK:(i + 1) * BLOCK, :] for i in range(WIN_G)]
    o_ref[...] = jnp.concatenate(outs, axis=1).astype(BF16)


def _l0_window_attn(hq, sink, ck, cv):
    nb = DEC_SEQ // BLOCK
    r0 = NP // BLOCK

    def kv_spec(col, d):
        return pl.BlockSpec((BLOCK, LANES),
                            lambda b, n: (r0 + b * nb + jnp.clip(n + d, 0, nb - 1), col))

    ctx_spec = pl.BlockSpec((None, PAST_LEN, LANES), lambda b, n: (b, 0, 0))
    return pl.pallas_call(
        _l0_window_attn_kernel,
        grid=(DEC_BATCH, nb),
        in_specs=[pl.BlockSpec(memory_space=pltpu.SMEM),
                  pl.BlockSpec((BLOCK, 512), lambda b, n: (r0 + b * nb + n, 0)),
                  kv_spec(4, -1), kv_spec(4, 0), kv_spec(4, 1),
                  kv_spec(5, -1), kv_spec(5, 0), kv_spec(5, 1),
                  ctx_spec, ctx_spec],
        out_specs=pl.BlockSpec((BLOCK, 512), lambda b, n: (b * nb + n, 0)),
        out_shape=jax.ShapeDtypeStruct((NS, 512), BF16),
        compiler_params=_cparams(("arbitrary", "arbitrary")),
        name="l0_window_attn",
    )(sink, hq, hq, hq, hq, hq, hq, hq, ck, cv)


def _diff_attn_kernel(*refs, lam_init, has_ctx):
    if has_ctx:
        q_ref, k_ref, v_ref, ck_ref, cv_ref, lam_ref, g_ref, o_ref, vaug, cvaug = refs
    else:
        q_ref, k_ref, v_ref, lam_ref, g_ref, o_ref, vaug = refs

    @pl.when(pl.program_id(1) == 0)
    def _():
        ones = jnp.ones((v_ref.shape[0], DIFF_V), BF16)
        for h in range(DIFF_HEADS):
            vaug[h, :, :DIFF_V] = v_ref[:, h * DIFF_V:(h + 1) * DIFF_V]
            vaug[h, :, DIFF_V:] = ones
        if has_ctx:
            cones = jnp.ones((PAST_LEN, DIFF_V), BF16)
            for h in range(DIFF_HEADS):
                cvaug[h, :, :DIFF_V] = cv_ref[:, h * DIFF_V:(h + 1) * DIFF_V].astype(BF16)
                cvaug[h, :, DIFF_V:] = cones

    lv = lam_ref[...]
    lam = (jnp.exp(jnp.sum(lv[0:1] * lv[1:2], axis=-1, keepdims=True))
           - jnp.exp(jnp.sum(lv[2:3] * lv[3:4], axis=-1, keepdims=True)) + lam_init)
    g = g_ref[...]
    outs = []
    for h in range(DIFF_HEADS):
        comps = []
        for c in range(2):
            lo = h * DIFF_V + c * HEAD_DIM
            q = q_ref[:, lo:lo + HEAD_DIM]
            s = lax.dot_general(q, k_ref[:, lo:lo + HEAD_DIM], _NT, preferred_element_type=F32)
            m = jnp.max(s, axis=-1, keepdims=True)
            if has_ctx:
                sc = lax.dot_general(q, ck_ref[:, lo:lo + HEAD_DIM].astype(BF16), _NT,
                                     preferred_element_type=F32)
                m = jnp.maximum(m, jnp.max(sc, axis=-1, keepdims=True))
            oa = jnp.dot(jnp.exp2(s - m).astype(BF16), vaug[h], preferred_element_type=F32)
            if has_ctx:
                oa = oa + jnp.dot(jnp.exp2(sc - m).astype(BF16), cvaug[h], preferred_element_type=F32)
            comps.append((oa[:, :DIFF_V], oa[:, DIFF_V:DIFF_V + 1]))
        (o0, d0), (o1, d1) = comps
        o = o0 / d0 - o1 * (lam / d1)
        ms = jnp.mean(o * o, axis=-1, keepdims=True)
        outs.append(((o * lax.rsqrt(ms + EPS)) * g) * (1.0 - lam_init))
    o_ref[...] = jnp.concatenate(outs, axis=1).astype(BF16)


def _diff_attn(h1, lam_vecs, subln_g, lam_init, ctx):
    g = subln_g.reshape(1, DIFF_V)
    small = [_const_spec((4, HEAD_DIM)), _const_spec((1, DIFF_V))]
    if ctx is None:
        grid = (BATCH, 1)
        in_specs = [pl.BlockSpec((SEQ, D_MODEL), lambda b, i: (b, 0)),
                    pl.BlockSpec((SEQ, D_MODEL), lambda b, i: (b, 1)),
                    pl.BlockSpec((SEQ, D_MODEL), lambda b, i: (b, 2))] + small
        out_specs = pl.BlockSpec((SEQ, D_MODEL), lambda b, i: (b, 0))
        rows = NP
        args = (h1, h1, h1, lam_vecs, g)
        name = "diff_attn_prompt"
    else:
        nq = DEC_SEQ // TQ
        q0 = NP // TQ
        s0 = NP // DEC_SEQ
        ctx_spec = pl.BlockSpec((None, PAST_LEN, D_MODEL), lambda b, i: (b, 0, 0))
        grid = (DEC_BATCH, nq)
        in_specs = [pl.BlockSpec((TQ, D_MODEL), lambda b, i: (q0 + b * nq + i, 0)),
                    pl.BlockSpec((DEC_SEQ, D_MODEL), lambda b, i: (s0 + b, 1)),
                    pl.BlockSpec((DEC_SEQ, D_MODEL), lambda b, i: (s0 + b, 2)),
                    ctx_spec, ctx_spec] + small
        out_specs = pl.BlockSpec((TQ, D_MODEL), lambda b, i: (b * nq + i, 0))
        rows = NS
        args = (h1, h1, h1, ctx[0], ctx[1], lam_vecs, g)
        name = "diff_attn_sample"
    tk = SEQ if ctx is None else DEC_SEQ
    scratch = [pltpu.VMEM((DIFF_HEADS, tk, 2 * DIFF_V), BF16)]
    if ctx is not None:
        scratch.append(pltpu.VMEM((DIFF_HEADS, PAST_LEN, 2 * DIFF_V), BF16))
    return pl.pallas_call(
        functools.partial(_diff_attn_kernel, lam_init=lam_init, has_ctx=ctx is not None),
        grid=grid, in_specs=in_specs, out_specs=out_specs,
        scratch_shapes=scratch,
        out_shape=jax.ShapeDtypeStruct((rows, D_MODEL), BF16),
        compiler_params=_cparams(("arbitrary", "arbitrary")),
        name=name,
    )(*args)


def _store_row_tiles(ref, val, rows):
    for s in range(ROW_TILES):
        ref[pl.ds(s, rows, stride=ROW_TILES), :] = val[:, s * LANES:(s + 1) * LANES]


def _post_kernel(*refs, layer):
    if layer == 0:
        (a_ref, op_ref, os_ref, xa_ref, xb_ref, g1_ref, sh2_ref, sc2_ref, w_ref, n2g_ref, rw_ref, rb_ref,
         xo_ref, xn_ref, eid_ref, gate_ref, cnt_ref) = refs
    else:
        (op_ref, os_ref, xa_ref, xb_ref, g1_ref, sh2_ref, sc2_ref, w_ref, n2g_ref, rw_ref, rb_ref,
         xo_ref, xn_ref, eid_ref, gate_ref, cnt_ref) = refs
    i = pl.program_id(0)
    o = jnp.where(i < NP // TM, op_ref[...], os_ref[...])
    if layer == 0:
        mix = (jnp.dot(a_ref[...], w_ref[:CONV_CH, :], preferred_element_type=F32)
               + jnp.dot(o, w_ref[CONV_CH:, :], preferred_element_type=F32))
    else:
        mix = jnp.dot(o, w_ref[...], preferred_element_type=F32)
    x1 = _pick_x(xa_ref, xb_ref, TM) + g1_ref[0] * mix
    xo_ref[...] = x1
    xn = _ada_norm(x1, n2g_ref[...], sh2_ref[0], sc2_ref[0])
    _store_row_tiles(xn_ref, xn, TM)
    rw = rw_ref[...]
    xh = xn.astype(BF16)
    xl = (xn - xh.astype(F32)).astype(BF16)
    wh = rw.astype(BF16)
    wl = (rw - wh.astype(F32)).astype(BF16)
    logits = (jnp.dot(xh, wh, preferred_element_type=F32)
              + (jnp.dot(xl, wh, preferred_element_type=F32)
                 + jnp.dot(xh, wl, preferred_element_type=F32))) + rb_ref[...]
    lane = lax.broadcasted_iota(I32, logits.shape, 1)
    vals, idxs = [], []
    picked = jnp.zeros(logits.shape, F32)
    for _ in range(TOP_K):
        m = jnp.max(logits, axis=-1, keepdims=True)
        idx = jnp.min(jnp.where(logits == m, lane, N_EXPERTS), axis=-1, keepdims=True)
        vals.append(m)
        idxs.append(idx)
        hit = lane == idx
        picked = picked + hit.astype(F32)
        logits = jnp.where(hit, -jnp.inf, logits)
    es = [jnp.exp(v - vals[0]) for v in vals]
    den = es[0] + es[1] + es[2] + es[3]
    eid_ref[...] = jnp.concatenate(idxs, axis=1)
    gate_ref[...] = jnp.concatenate([e / den for e in es], axis=1)

    @pl.when(i == 0)
    def _():
        cnt_ref[...] = jnp.zeros_like(cnt_ref)

    cnt_ref[...] += jnp.sum(picked, axis=0, keepdims=True)


def _post(layer, mix_parts, xa, xb, mods, w_bf, n2g, rw, rb):
    npb = NP // TM
    if layer == 0:
        a, o_p, o_s = mix_parts
        wo = 512
        mix_specs = [pl.BlockSpec((TM, CONV_CH), lambda i: (i, 0))]
        mix_args = [a, o_p, o_s]
    else:
        o_p, o_s = mix_parts
        wo = D_MODEL
        mix_specs = []
        mix_args = [o_p, o_s]
    mix_specs += [pl.BlockSpec((TM, wo), lambda i: (jnp.minimum(i, npb - 1), 0)),
                  pl.BlockSpec((TM, wo), lambda i: (jnp.maximum(i - npb, 0), 0))]
    return pl.pallas_call(
        functools.partial(_post_kernel, layer=layer),
        grid=(N // TM,),
        in_specs=mix_specs + _x_specs(xa is xb, TM) + [
                              _mod_spec(2, TM), _mod_spec(3, TM), _mod_spec(4, TM),
                              _const_spec((D_MODEL, D_MODEL)), _const_spec((1, D_MODEL)),
                              _const_spec((D_MODEL, N_EXPERTS)), _const_spec((1, N_EXPERTS))],
        out_specs=(pl.BlockSpec((TM, D_MODEL), lambda i: (i, 0)),
                   pl.BlockSpec((TM * ROW_TILES, LANES), lambda i: (i, 0)),
                   pl.BlockSpec((TM, TOP_K), lambda i: (i, 0)),
                   pl.BlockSpec((TM, TOP_K), lambda i: (i, 0)),
                   _const_spec((1, N_EXPERTS))),
        out_shape=(jax.ShapeDtypeStruct((N, D_MODEL), F32),
                   jax.ShapeDtypeStruct((N * ROW_TILES, LANES), F32),
                   jax.ShapeDtypeStruct((N, TOP_K), I32),
                   jax.ShapeDtypeStruct((N, TOP_K), F32),
                   jax.ShapeDtypeStruct((1, N_EXPERTS), F32)),
        compiler_params=_cparams(("arbitrary",)),
        name="post_l%d" % layer,
    )(*mix_args, xa, xb, mods, mods, mods, w_bf, n2g.reshape(1, -1), rw, rb.reshape(1, -1))


def _slots_kernel(cnt_ref, bst_ref, eidt_ref, base_ref, triu_ref, slot_ref, carry, dvm, *scratch):
    dsm = (scratch[:TOP_K], scratch[TOP_K:2 * TOP_K])
    sem = scratch[2 * TOP_K]
    j = pl.program_id(0)
    n_tiles = pl.num_programs(0) - 1

    def copies(par):
        return [pltpu.make_async_copy(dvm.at[par, k], dsm[par][k], sem.at[par, k]) for k in range(TOP_K)]

    @pl.when(j == 0)
    def _():
        carry[...] = jnp.zeros_like(carry)

    for par in range(2):
        @pl.when(jnp.logical_and(j < n_tiles, j % 2 == par))
        def _():
            e_iota = lax.broadcasted_iota(I32, (N_EXPERTS, RT), 0)
            ohs = [(eidt_ref[k:k + 1, :] == e_iota).astype(F32) for k in range(TOP_K)]
            ohsum = ohs[0] + ohs[1] + ohs[2] + ohs[3]
            cum = jnp.dot(ohsum.astype(BF16), triu_ref[...], preferred_element_type=F32)
            tot = cum + carry[...] + base_ref[...]
            dvm[par] = jnp.concatenate(
                [jnp.sum(oh * tot, axis=0, keepdims=True) for oh in ohs], axis=0).astype(I32)
            carry[...] += jnp.sum(ohsum, axis=1, keepdims=True)
            for cp in copies(par):
                cp.start()

    for par in range(2):
        @pl.when(jnp.logical_and(j > 0, (j - 1) % 2 == par))
        def _():
            for cp in copies(par):
                cp.wait()

            def scatter(c, carry_):
                t0 = c * SCATTER_UNROLL
                a0 = ((j - 1) * RT + t0) * TOP_K
                for u in range(SCATTER_UNROLL):
                    for k in range(TOP_K):
                        slot_ref[dsm[par][k][t0 + u]] = a0 + (u * TOP_K + k)
                return carry_

            lax.fori_loop(0, RT // SCATTER_UNROLL, scatter, 0)

    @pl.when(j == pl.num_programs(0) - 1)
    def _():
        def pad(s, c):
            slot_ref[s] = (N + s % SLOT_LEAD) * TOP_K + (TOP_K - 1)
            return c

        def per_expert(e, c):
            cnt = cnt_ref[e]
            b0 = bst_ref[e]
            lax.fori_loop(SLOT_LEAD + b0 * TME + cnt,
                          SLOT_LEAD + (b0 + (cnt + TME - 1) // TME) * TME, pad, 0)
            return c

        lax.fori_loop(0, SLOT_LEAD, pad, 0)
        lax.fori_loop(0, N_EXPERTS, per_expert, 0)
        lax.fori_loop(SLOT_LEAD + bst_ref[N_EXPERTS] * TME, N_SLOTS, pad, 0)


def _route(eid, counts_f):
    counts = counts_f.reshape(-1).astype(I32)
    nblk = (counts + TME - 1) // TME
    bend = jnp.cumsum(nblk)
    bstart = bend - nblk
    bst = jnp.concatenate([bstart, bend[-1:]]).astype(I32)
    base = (bstart * TME + SLOT_LEAD).astype(F32).reshape(N_EXPERTS, 1)
    ar = jnp.arange(RT, dtype=I32)
    triu = (ar[:, None] < ar[None, :]).astype(BF16)
    grid_spec = pltpu.PrefetchScalarGridSpec(
        num_scalar_prefetch=2,
        grid=(N // RT + 1,),
        in_specs=[pl.BlockSpec((TOP_K, RT), lambda j, c, b: (0, jnp.minimum(j, N // RT - 1))),
                  _const_spec((N_EXPERTS, 1)),
                  _const_spec((RT, RT))],
        out_specs=pl.BlockSpec(memory_space=pltpu.SMEM),
        scratch_shapes=[pltpu.VMEM((N_EXPERTS, 1), F32),
                        pltpu.VMEM((2, TOP_K, RT), I32)]
        + [pltpu.SMEM((RT,), I32)] * (2 * TOP_K)
        + [pltpu.SemaphoreType.DMA((2, TOP_K))],
    )
    slot_asg = pl.pallas_call(
        _slots_kernel,
        grid_spec=grid_spec,
        out_shape=jax.ShapeDtypeStruct((N_SLOTS,), I32),
        compiler_params=_cparams(("arbitrary",)),
        name="slots",
    )(counts, bst, eid.T, base, triu)
    return slot_asg, bst, counts


def _row_tile(idx):
    if isinstance(idx, int):
        return pl.ds(idx * ROW_TILES, ROW_TILES)
    return pl.ds(pl.multiple_of(idx * ROW_TILES, ROW_TILES), ROW_TILES)


def _expert_kernel(slot_ref, bst_ref, cnt_ref, xn_hbm, upw_ref, upb_ref, dww_ref, dwb_ref,
                   y_hbm, gbuf0, gbuf1, sbuf0, sbuf1, upbf, dwbf, xb_ref, hdn_ref, gsem, ssem):
    e = pl.program_id(0)
    nb = (cnt_ref[e] + TME - 1) // TME
    b0 = bst_ref[e]
    gbufs = (gbuf0, gbuf1)
    sbufs = (sbuf0, sbuf1)

    def gather_row(g, buf, r):
        v = slot_ref[(g + 2) * TME + r]
        tok = jnp.minimum(lax.shift_right_logical(v, TOP_K_SHIFT), N - 1)
        pltpu.make_async_copy(xn_hbm.at[_row_tile(tok), :], gbufs[buf].at[_row_tile(r), :],
                              gsem.at[buf]).start()

    def scatter_row(g, buf, r):
        v = slot_ref[(g + 2) * TME + r]
        dst = (v & (TOP_K - 1)) * N + lax.shift_right_logical(v, TOP_K_SHIFT)
        pltpu.make_async_copy(sbufs[buf].at[_row_tile(r), :], y_hbm.at[_row_tile(dst), :],
                              ssem.at[buf]).start(priority=1)

    def rolled(row_fn, g, buf):
        def body(r, c):
            row_fn(g, buf, r)
            return c
        lax.fori_loop(0, TME, body, 0)

    def wait_gather(buf):
        pltpu.make_async_copy(xn_hbm.at[pl.ds(0, TME * ROW_TILES), :], gbufs[buf], gsem.at[buf]).wait()

    def wait_scatter(buf):
        pltpu.make_async_copy(sbufs[buf], y_hbm.at[pl.ds(0, TME * ROW_TILES), :], ssem.at[buf]).wait()

    @pl.when(e == 0)
    def _():
        sbuf0[...] = jnp.zeros_like(sbuf0)
        sbuf1[...] = jnp.zeros_like(sbuf1)
        rolled(gather_row, 0, 0)
        rolled(scatter_row, -2, 0)

    @pl.when(nb > 0)
    def _():
        upbf[...] = upw_ref[0].astype(BF16)
        dwbf[...] = dww_ref[0].astype(BF16)

    def work(g, cur):
        oth = 1 - cur
        wait_gather(cur)
        xb_ref[...] = jnp.concatenate(
            [gbufs[cur][pl.ds(s, TME, stride=ROW_TILES), :] for s in range(ROW_TILES)], axis=1).astype(BF16)

        def phase(ph, c):
            @pl.when(ph == 0)
            def _():
                for r in range(TME):
                    gather_row(g + 1, oth, r)
                gu = jnp.dot(xb_ref[...], upbf[...], preferred_element_type=F32) + upb_ref[0]
                gg = jnp.minimum(gu[:, :D_FF], SWIGLU_LIMIT)
                lin = jnp.clip(gu[:, D_FF:], -SWIGLU_LIMIT, SWIGLU_LIMIT)
                hdn_ref[...] = (gg * jax.nn.sigmoid(SWIGLU_ALPHA * gg) * (lin + 1.0)).astype(BF16)

            @pl.when(ph == 1)
            def _():
                for r in range(TME):
                    scatter_row(g - 1, oth, r)
                y = jnp.dot(hdn_ref[...], dwbf[...], preferred_element_type=F32) + dwb_ref[0]
                wait_scatter(cur)
                for s in range(ROW_TILES):
                    sbufs[cur][pl.ds(s, TME, stride=ROW_TILES), :] = y[:, s * LANES:(s + 1) * LANES]
            return c

        lax.fori_loop(0, 2, phase, 0)

    def block(j, carry):
        g = b0 + j
        for parity in range(2):
            @pl.when(g % 2 == parity)
            def _():
                work(g, parity)
        return carry

    lax.fori_loop(0, nb, block, 0)

    @pl.when(e == pl.num_programs(0) - 1)
    def _():
        g_end = bst_ref[N_EXPERTS]
        for parity in range(2):
            @pl.when(g_end % 2 == parity)
            def _():
                rolled(scatter_row, g_end - 1, 1 - parity)
                wait_gather(parity)
        wait_scatter(0)
        wait_scatter(1)


def _experts(slot_asg, bst, counts, xn_tiles, up_w, up_b, down_w, down_b):
    grid_spec = pltpu.PrefetchScalarGridSpec(
        num_scalar_prefetch=3,
        grid=(N_EXPERTS,),
        in_specs=[pl.BlockSpec(memory_space=pl.ANY),
                  pl.BlockSpec((1, D_MODEL, 2 * D_FF), lambda e, *_: (e, 0, 0)),
                  pl.BlockSpec((1, 1, 2 * D_FF), lambda e, *_: (e, 0, 0)),
                  pl.BlockSpec((1, D_FF, D_MODEL), lambda e, *_: (e, 0, 0)),
                  pl.BlockSpec((1, 1, D_MODEL), lambda e, *_: (e, 0, 0))],
        out_specs=pl.BlockSpec(memory_space=pl.ANY),
        scratch_shapes=[pltpu.VMEM((TME * ROW_TILES, LANES), F32),
                        pltpu.VMEM((TME * ROW_TILES, LANES), F32),
                        pltpu.VMEM((TME * ROW_TILES, LANES), F32),
                        pltpu.VMEM((TME * ROW_TILES, LANES), F32),
                        pltpu.VMEM((D_MODEL, 2 * D_FF), BF16),
                        pltpu.VMEM((D_FF, D_MODEL), BF16),
                        pltpu.VMEM((TME, D_MODEL), BF16),
                        pltpu.VMEM((TME, D_FF), BF16),
                        pltpu.SemaphoreType.DMA((2,)),
                        pltpu.SemaphoreType.DMA((2,))],
    )
    return pl.pallas_call(
        _expert_kernel,
        grid_spec=grid_spec,
        out_shape=jax.ShapeDtypeStruct(((N_ASG + SLOT_LEAD) * ROW_TILES, LANES), F32),
        compiler_params=_cparams(("arbitrary",)),
        name="experts",
    )(slot_asg, bst, counts, xn_tiles, up_w, up_b.reshape(N_EXPERTS, 1, -1),
      down_w, down_b.reshape(N_EXPERTS, 1, -1))


def _combine_kernel(x_ref, gate_ref, g2_ref, fg_ref, y0_ref, y1_ref, y2_ref, y3_ref, *o_refs, final):
    gate = gate_ref[...]
    cols = []
    for s in range(ROW_TILES):
        acc = None
        for k, y_ref in enumerate((y0_ref, y1_ref, y2_ref, y3_ref)):
            term = gate[:, k:k + 1] * y_ref[pl.ds(s, TMC, stride=ROW_TILES), :]
            acc = term if acc is None else acc + term
        cols.append(acc)
    x2 = x_ref[...] + g2_ref[0] * jnp.concatenate(cols, axis=1)
    if not final:
        o_refs[0][...] = x2
        return
    ms = jnp.mean(x2 * x2, axis=-1, keepdims=True)
    y = x2 * lax.rsqrt(ms + EPS) * fg_ref[...]
    is_prompt = pl.program_id(0) < NP // TMC

    @pl.when(is_prompt)
    def _():
        o_refs[0][...] = y

    @pl.when(jnp.logical_not(is_prompt))
    def _():
        o_refs[1][...] = y


def _combine(x, gates, mods, final_g, y_tiles, final):
    nblk = N // TMC

    npb = NP // TMC

    def y_spec(k):
        return pl.BlockSpec((TMC * ROW_TILES, LANES), lambda i: (k * nblk + i, 0))

    if final:
        out_specs = (pl.BlockSpec((TMC, D_MODEL), lambda i: (jnp.minimum(i, npb - 1), 0)),
                     pl.BlockSpec((TMC, D_MODEL), lambda i: (jnp.maximum(i - npb, 0), 0)))
        out_shape = (jax.ShapeDtypeStruct((NP, D_MODEL), F32), jax.ShapeDtypeStruct((NS, D_MODEL), F32))
    else:
        out_specs = pl.BlockSpec((TMC, D_MODEL), lambda i: (i, 0))
        out_shape = jax.ShapeDtypeStruct((N, D_MODEL), F32)

    return pl.pallas_call(
        functools.partial(_combine_kernel, final=final),
        grid=(nblk,),
        in_specs=[pl.BlockSpec((TMC, D_MODEL), lambda i: (i, 0)),
                  pl.BlockSpec((TMC, TOP_K), lambda i: (i, 0)),
                  _mod_spec(5, TMC),
                  _const_spec((1, D_MODEL)),
                  y_spec(0), y_spec(1), y_spec(2), y_spec(3)],
        out_specs=out_specs,
        out_shape=out_shape,
        compiler_params=_cparams(("arbitrary",)),
        name="combine",
    )(x, gates, mods, final_g.reshape(1, -1), y_tiles, y_tiles, y_tiles, y_tiles)


def _rope_tables():
    pos = jnp.arange(DEC_SEQ)
    r = (pos // GRID_W).astype(F32)
    col = (pos % GRID_W).astype(F32)
    quarter = HEAD_DIM // 4
    inv = ROPE_BASE ** (-jnp.arange(quarter, dtype=F32) / quarter)
    ar = r[:, None] * inv
    ac = col[:, None] * inv
    ang = jnp.concatenate([ar, ar, ac, ac], axis=-1)
    cos = jnp.tile(jnp.cos(ang), (1, LANES // HEAD_DIM))
    sin = jnp.tile(jnp.sin(ang), (1, LANES // HEAD_DIM))
    first = (jnp.arange(LANES) % 32) < 16
    sa = jnp.where(first[None, :], -sin, 0.0)
    sb = jnp.where(first[None, :], 0.0, sin)
    return cos, sa, sb


def _moe(x, xn, eid, gates, counts, mods, final_g, up_w, up_b, down_w, down_b, final):
    slot_asg, bst, counts_i = _route(eid, counts)
    y = _experts(slot_asg, bst, counts_i, xn, up_w, up_b, down_w, down_b)
    return _combine(x, gates, mods, final_g, y, final)


@jax.jit
def kernel(x_prompt, x_sample, cache_l0_k, cache_l0_v, cache_l1_k, cache_l1_v, c, c_ctx, final_g,
           l0_mod_w, l0_mod_b, l0_norm1_g, l0_w_in, l0_conv_w, l0_conv_b, l0_cnorm_g, l0_cnorm_b,
           l0_sink, l0_w_out, l0_norm2_g, l0_router_w, l0_router_b, l0_up_w, l0_up_b, l0_down_w,
           l0_down_b,
           l1_mod_w, l1_mod_b, l1_norm1_g, l1_w_in, l1_lam_q1, l1_lam_k1, l1_lam_q2, l1_lam_k2,
           l1_subln_g, l1_w_out, l1_norm2_g, l1_router_w, l1_router_b, l1_up_w, l1_up_b, l1_down_w,
           l1_down_b):
    xp0 = x_prompt.reshape(NP, D_MODEL)
    xs0 = x_sample.reshape(NS, D_MODEL)
    cond16 = jnp.concatenate([c, c_ctx[None, :], jnp.zeros((16 - DEC_BATCH - 1, D_MODEL), F32)], axis=0)
    rope_tabs = _rope_tables()

    mods = _modulation(cond16, l0_mod_w, l0_mod_b)
    ha, hq, st0 = _inproj(xp0, xs0, mods, l0_norm1_g, l0_w_in.astype(BF16), rope_tabs, 0)
    a = _conv(ha, l0_conv_w, l0_conv_b, l0_cnorm_g, l0_cnorm_b)
    o_p = _l0_prompt_attn(hq, l0_sink)
    o_s = _l0_window_attn(hq, l0_sink, cache_l0_k.reshape(DEC_BATCH, PAST_LEN, LANES),
                          cache_l0_v.reshape(DEC_BATCH, PAST_LEN, LANES))
    x, xn, eid, gates, counts = _post(0, (a, o_p, o_s), xp0, xs0, mods, l0_w_out.astype(BF16), l0_norm2_g,
                                      l0_router_w, l0_router_b)
    x = _moe(x, xn, eid, gates, counts, mods, final_g, l0_up_w, l0_up_b, l0_down_w, l0_down_b, False)
    state_l0_k = st0[:, :LANES].reshape(BATCH, SEQ, WIN_KV, HEAD_DIM)
    state_l0_v = st0[:, LANES:].reshape(BATCH, SEQ, WIN_KV, HEAD_DIM)

    lam_init = 0.8 - 0.6 * math.exp(-0.3 * 1)
    mods = _modulation(cond16, l1_mod_w, l1_mod_b)
    h1, st1 = _inproj(x, x, mods, l1_norm1_g, l1_w_in.astype(BF16), rope_tabs, 1)
    lam_vecs = jnp.stack([l1_lam_q1, l1_lam_k1, l1_lam_q2, l1_lam_k2], axis=0)
    o_p = _diff_attn(h1, lam_vecs, l1_subln_g, lam_init, None)
    o_s = _diff_attn(h1, lam_vecs, l1_subln_g, lam_init,
                     (cache_l1_k.reshape(DEC_BATCH, PAST_LEN, D_MODEL),
                      cache_l1_v.reshape(DEC_BATCH, PAST_LEN, D_MODEL)))
    x, xn, eid, gates, counts = _post(1, (o_p, o_s), x, x, mods, l1_w_out.astype(BF16), l1_norm2_g,
                                      l1_router_w, l1_router_b)
    yp, ys = _moe(x, xn, eid, gates, counts, mods, final_g, l1_up_w, l1_up_b, l1_down_w, l1_down_b, True)
    state_l1_k = st1[:, :D_MODEL].reshape(BATCH, SEQ, DIFF_HEADS, 2, HEAD_DIM)
    state_l1_v = st1[:, D_MODEL:].reshape(BATCH, SEQ, DIFF_HEADS, DIFF_V)

    y_prompt = yp.reshape(BATCH, SEQ, D_MODEL)
    y_sample = ys.reshape(DEC_BATCH, DEC_SEQ, D_MODEL)
    return (y_prompt, y_sample, state_l0_k, state_l0_v, state_l1_k, state_l1_v)
```

```python
import functools
import math

import jax
import jax.numpy as jnp
from jax import lax
from jax.experimental import pallas as pl
from jax.experimental.pallas import tpu as pltpu

F32 = jnp.float32
BF16 = jnp.bfloat16
I32 = jnp.int32

D_MODEL = 1024
BATCH = 16
SEQ = 256
DEC_BATCH = 8
DEC_SEQ = 2048
PAST_LEN = 256
GRID_W = 64
HEAD_DIM = 64
BLOCK = 128
WINDOW = 128
ROPE_BASE = 10000.0
EPS = 1e-6
NEG = -1e30
ATTN_SCALE = HEAD_DIM ** -0.5
LOG2E = math.log2(math.e)
CONV_CH = 512
CONV_W = 31
WIN_HEADS = 8
WIN_KV = 2
WIN_G = 4
IN0_W = 1792
DIFF_HEADS = 8
DIFF_V = 128
IN1_W = 3072
N_EXPERTS = 32
TOP_K = 4
TOP_K_SHIFT = 2
D_FF = 1024
SWIGLU_LIMIT = 7.0
SWIGLU_ALPHA = 1.702

NP = BATCH * SEQ
NS = DEC_BATCH * DEC_SEQ
N = NP + NS
N_ASG = N * TOP_K

LANES = 128
SUBLANES = 8
ROW_TILES = D_MODEL // LANES
VMEM_LIMIT = 56 * 1024 * 1024

TM = 512
TMC = 512
TME = 256
CB = 256
HALO = 16
SHIFT_ROWS = CB + 2 * HALO - SUBLANES
TQ = 256
RT = 512
SCATTER_UNROLL = 64
MAX_EBLOCKS = (N_ASG + N_EXPERTS * (TME - 1)) // TME
SLOT_LEAD = 2 * TME
N_SLOTS = (MAX_EBLOCKS + 3) * TME

_NT = (((1,), (1,)), ((), ()))


def _cparams(sem):
    return pltpu.CompilerParams(dimension_semantics=sem, vmem_limit_bytes=VMEM_LIMIT)


def _mod_row(i, tm):
    npb = NP // tm
    return jnp.where(i < npb, DEC_BATCH, (i - npb) // (DEC_SEQ // tm))


def _mod_spec(j, tm):
    return pl.BlockSpec((1, 1, D_MODEL), lambda i, *_: (_mod_row(i, tm) * 6 + j, 0, 0))


def _const_spec(shape):
    nd = len(shape)
    return pl.BlockSpec(shape, lambda *_: (0,) * nd)


def _x_specs(unified, tm):
    npb = NP // tm
    off = npb if unified else 0
    return [pl.BlockSpec((tm, D_MODEL), lambda i, *_: (jnp.minimum(i, npb - 1), 0)),
            pl.BlockSpec((tm, D_MODEL), lambda i, *_: (jnp.maximum(i - npb, 0) + off, 0))]


def _pick_x(xa_ref, xb_ref, tm):
    return jnp.where(pl.program_id(0) < NP // tm, xa_ref[...], xb_ref[...])


def _ada_norm(x, g, shift, scale):
    ms = jnp.mean(x * x, axis=-1, keepdims=True)
    return (x * lax.rsqrt(ms + EPS) * g) * (1.0 + scale) + shift


def _mod_kernel(c_ref, w_ref, b_ref, o_ref):
    c = c_ref[...]
    s = c * jax.nn.sigmoid(c)
    o_ref[...] = jnp.dot(s.astype(BF16), w_ref[...].astype(BF16), preferred_element_type=F32) + b_ref[...]


def _modulation(cond16, w, b):
    m = pl.pallas_call(
        _mod_kernel,
        grid=(6,),
        in_specs=[_const_spec((16, D_MODEL)),
                  pl.BlockSpec((D_MODEL, D_MODEL), lambda j: (0, j)),
                  pl.BlockSpec((1, D_MODEL), lambda j: (0, j))],
        out_specs=pl.BlockSpec((16, D_MODEL), lambda j: (0, j)),
        out_shape=jax.ShapeDtypeStruct((16, 6 * D_MODEL), F32),
        compiler_params=_cparams(("arbitrary",)),
        name="modulation",
    )(cond16, w, b.reshape(1, -1))
    return m.reshape(16 * 6, 1, D_MODEL)


def _rope(v, cos, sa, sb):
    return v * cos + pltpu.roll(v, LANES - 16, 1) * sa + pltpu.roll(v, 16, 1) * sb


def _inproj_kernel(xa_ref, xb_ref, sh_ref, sc_ref, g_ref, w_ref, cos_ref, sa_ref, sb_ref, *outs, layer):
    i = pl.program_id(0)
    h = _ada_norm(_pick_x(xa_ref, xb_ref, TM), g_ref[...], sh_ref[0], sc_ref[0])
    acc = jnp.dot(h.astype(BF16), w_ref[...], preferred_element_type=F32)
    if layer == 0:
        ha_ref, hq_ref, st_ref = outs
        ha_ref[...] = acc[:, :2 * CONV_CH]
        base, n_q, n_rope, n_all, st_lo = 2 * CONV_CH, 4, 5, 6, 2 * CONV_CH + 512
        q_scale = ATTN_SCALE
    else:
        hq_ref, st_ref = outs
        base, n_q, n_rope, n_all, st_lo = 0, 8, 16, 24, 1024
        q_scale = ATTN_SCALE * LOG2E
    is_prompt = i < NP // TM

    def chunk(c):
        v = acc[:, base + c * LANES: base + (c + 1) * LANES]
        return v * q_scale if c < n_q else v

    @pl.when(is_prompt)
    def _():
        for c in range(n_all):
            hq_ref[:, c * LANES:(c + 1) * LANES] = chunk(c).astype(BF16)
        st_ref[...] = acc[:, st_lo:]

    @pl.when(jnp.logical_not(is_prompt))
    def _():
        cos, sa, sb = cos_ref[...], sa_ref[...], sb_ref[...]
        for c in range(n_all):
            v = chunk(c)
            if c < n_rope:
                v = _rope(v, cos, sa, sb)
            hq_ref[:, c * LANES:(c + 1) * LANES] = v.astype(BF16)


def _inproj(xa, xb, mods, g, w_bf, rope_tabs, layer):
    npb = NP // TM
    spb = DEC_SEQ // TM
    nout = w_bf.shape[1]
    rope_spec = pl.BlockSpec((TM, LANES), lambda i: (jnp.where(i < npb, 0, (i - npb) % spb), 0))
    st_w = 256 if layer == 0 else 2048
    st_spec = pl.BlockSpec((TM, st_w), lambda i: (jnp.minimum(i, npb - 1), 0))
    if layer == 0:
        out_shape = (jax.ShapeDtypeStruct((N, 2 * CONV_CH), F32),
                     jax.ShapeDtypeStruct((N, 768), BF16),
                     jax.ShapeDtypeStruct((NP, st_w), F32))
        out_specs = (pl.BlockSpec((TM, 2 * CONV_CH), lambda i: (i, 0)),
                     pl.BlockSpec((TM, 768), lambda i: (i, 0)), st_spec)
    else:
        out_shape = (jax.ShapeDtypeStruct((N, IN1_W), BF16),
                     jax.ShapeDtypeStruct((NP, st_w), F32))
        out_specs = (pl.BlockSpec((TM, IN1_W), lambda i: (i, 0)), st_spec)
    return pl.pallas_call(
        functools.partial(_inproj_kernel, layer=layer),
        grid=(N // TM,),
        in_specs=_x_specs(xa is xb, TM) + [
                  _mod_spec(0, TM), _mod_spec(1, TM),
                  _const_spec((1, D_MODEL)),
                  _const_spec((D_MODEL, nout)),
                  rope_spec, rope_spec, rope_spec],
        out_specs=out_specs,
        out_shape=out_shape,
        compiler_params=_cparams(("arbitrary",)),
        name="inproj_l%d" % layer,
    )(xa, xb, mods, mods, g.reshape(1, -1), w_bf, *rope_tabs)


def _conv_kernel(prev_ref, cur_ref, next_ref, w_ref, b_ref, g_ref, bb_ref, o_ref, upad, ush):
    i = pl.program_id(0)
    npb = NP // CB
    spb = DEC_SEQ // CB
    j = (i - npb) % spb
    first = jnp.logical_or(i < npb, j == 0)
    last = jnp.logical_or(i < npb, j == spb - 1)

    def glu(r):
        return r[:, :CONV_CH] * jax.nn.sigmoid(r[:, CONV_CH:])

    upad[HALO:HALO + CB, :] = glu(cur_ref[...])
    upad[0:HALO, :] = jnp.where(first, 0.0, glu(prev_ref[...]))
    upad[HALO + CB:, :] = jnp.where(last, 0.0, glu(next_ref[...]))
    acc = jnp.zeros((CB, CONV_CH), F32)
    off = HALO - CONV_W // 2
    for b in range(SUBLANES):
        taps = [t for t in range(CONV_W) if (off + t) % SUBLANES == b]
        if b > 0:
            ush[b - 1] = upad[b:b + SHIFT_ROWS, :]
        for t in taps:
            a8 = off + t - b
            win = upad[a8:a8 + CB, :] if b == 0 else ush[b - 1, a8:a8 + CB, :]
            acc = acc + w_ref[t:t + 1, :] * win
    u = acc + b_ref[...]
    mu = jnp.mean(u, axis=-1, keepdims=True)
    var = jnp.mean(jnp.square(u - mu), axis=-1, keepdims=True)
    y = (u - mu) * lax.rsqrt(var + EPS) * g_ref[...] + bb_ref[...]
    o_ref[...] = (y * jax.nn.sigmoid(y)).astype(BF16)


def _conv(ha, conv_w, conv_b, cg, cb):
    hb = CB // HALO
    nh = N // HALO
    w_pad = jnp.concatenate([conv_w, jnp.zeros((1, CONV_CH), F32)], axis=0)
    return pl.pallas_call(
        _conv_kernel,
        grid=(N // CB,),
        in_specs=[pl.BlockSpec((HALO, 2 * CONV_CH), lambda i: (jnp.maximum(i * hb - 1, 0), 0)),
                  pl.BlockSpec((CB, 2 * CONV_CH), lambda i: (i, 0)),
                  pl.BlockSpec((HALO, 2 * CONV_CH), lambda i: (jnp.minimum((i + 1) * hb, nh - 1), 0)),
                  _const_spec((CONV_W + 1, CONV_CH)),
                  _const_spec((1, CONV_CH)), _const_spec((1, CONV_CH)), _const_spec((1, CONV_CH))],
        out_specs=pl.BlockSpec((CB, CONV_CH), lambda i: (i, 0)),
        out_shape=jax.ShapeDtypeStruct((N, CONV_CH), BF16),
        scratch_shapes=[pltpu.VMEM((CB + 2 * HALO, CONV_CH), F32),
                        pltpu.VMEM((SUBLANES - 1, SHIFT_ROWS, CONV_CH), F32)],
        compiler_params=_cparams(("arbitrary",)),
        name="conformer_conv",
    )(ha, ha, ha, w_pad, conv_b.reshape(1, -1), cg.reshape(1, -1), cb.reshape(1, -1))


def _sink_attend(q, k, v, sink, mask):
    s = lax.dot_general(q, k, _NT, preferred_element_type=F32)
    if mask is not None:
        s = jnp.where(mask, s, NEG)
    m = jnp.maximum(jnp.max(s, axis=-1, keepdims=True), sink)
    p = jnp.exp(s - m)
    den = jnp.sum(p, axis=-1, keepdims=True) + jnp.exp(sink - m)
    return jnp.dot(p.astype(BF16), v, preferred_element_type=F32) / den


def _l0_prompt_attn_kernel(sink_ref, q_ref, k_ref, v_ref, o_ref):
    k = k_ref[...]
    v = v_ref[...]
    outs = []
    for h in range(WIN_HEADS):
        j = h // WIN_G
        outs.append(_sink_attend(q_ref[:, h * HEAD_DIM:(h + 1) * HEAD_DIM],
                                 k[:, j * HEAD_DIM:(j + 1) * HEAD_DIM],
                                 v[:, j * HEAD_DIM:(j + 1) * HEAD_DIM], sink_ref[h], None))
    o_ref[...] = jnp.concatenate(outs, axis=1).astype(BF16)


def _l0_prompt_attn(hq, sink):
    return pl.pallas_call(
        _l0_prompt_attn_kernel,
        grid=(BATCH,),
        in_specs=[pl.BlockSpec(memory_space=pltpu.SMEM),
                  pl.BlockSpec((SEQ, 512), lambda b: (b, 0)),
                  pl.BlockSpec((SEQ, LANES), lambda b: (b, 4)),
                  pl.BlockSpec((SEQ, LANES), lambda b: (b, 5))],
        out_specs=pl.BlockSpec((SEQ, 512), lambda b: (b, 0)),
        out_shape=jax.ShapeDtypeStruct((NP, 512), BF16),
        compiler_params=_cparams(("arbitrary",)),
        name="l0_prompt_attn",
    )(sink, hq, hq, hq)


def _l0_window_attn_kernel(sink_ref, q_ref, kp_ref, kc_ref, kn_ref, vp_ref, vc_ref, vn_ref,
                           ck_ref, cv_ref, o_ref):
    n = pl.program_id(1)
    k = jnp.concatenate([kp_ref[...], kc_ref[...], kn_ref[...], ck_ref[...].astype(BF16)], axis=0)
    v = jnp.concatenate([vp_ref[...], vc_ref[...], vn_ref[...], cv_ref[...].astype(BF16)], axis=0)
    qpos = n * BLOCK + lax.broadcasted_iota(I32, (BLOCK, 3 * BLOCK + PAST_LEN), 0)
    col = lax.broadcasted_iota(I32, (BLOCK, 3 * BLOCK + PAST_LEN), 1)
    kpos = (n - 1) * BLOCK + col
    local_ok = (jnp.abs(kpos - qpos) <= WINDOW) & (kpos >= 0) & (kpos < DEC_SEQ)
    ok = jnp.logical_or(col >= 3 * BLOCK, local_ok).astype(F32)
    mask = jnp.concatenate([ok] * WIN_G, axis=0) > 0.5
    grp = lax.broadcasted_iota(I32, (WIN_G * BLOCK, 1), 0) // BLOCK
    outs = []
    for j in range(WIN_KV):
        heads = range(j * WIN_G, (j + 1) * WIN_G)
        q4 = jnp.concatenate([q_ref[:, h * HEAD_DIM:(h + 1) * HEAD_DIM] for h in heads], axis=0)
        sink4 = jnp.zeros((WIN_G * BLOCK, 1), F32)
        for i, h in enumerate(heads):
            sink4 = jnp.where(grp == i, sink_ref[h], sink4)
        o4 = _sink_attend(q4, k[:, j * HEAD_DIM:(j + 1) * HEAD_DIM],
                          v[:, j * HEAD_DIM:(j + 1) * HEAD_DIM], sink4, mask)
        outs += [o4[i * BLOCK:(i + 1) * BLOCK, :] for i in range(WIN_G)]
    o_ref[...] = jnp.concatenate(outs, axis=1).astype(BF16)


def _l0_window_attn(hq, sink, ck, cv):
    nb = DEC_SEQ // BLOCK
    r0 = NP // BLOCK

    def kv_spec(col, d):
        return pl.BlockSpec((BLOCK, LANES),
                            lambda b, n: (r0 + b * nb + jnp.clip(n + d, 0, nb - 1), col))

    ctx_spec = pl.BlockSpec((None, PAST_LEN, LANES), lambda b, n: (b, 0, 0))
    return pl.pallas_call(
        _l0_window_attn_kernel,
        grid=(DEC_BATCH, nb),
        in_specs=[pl.BlockSpec(memory_space=pltpu.SMEM),
                  pl.BlockSpec((BLOCK, 512), lambda b, n: (r0 + b * nb + n, 0)),
                  kv_spec(4, -1), kv_spec(4, 0), kv_spec(4, 1),
                  kv_spec(5, -1), kv_spec(5, 0), kv_spec(5, 1),
                  ctx_spec, ctx_spec],
        out_specs=pl.BlockSpec((BLOCK, 512), lambda b, n: (b * nb + n, 0)),
        out_shape=jax.ShapeDtypeStruct((NS, 512), BF16),
        compiler_params=_cparams(("arbitrary", "arbitrary")),
        name="l0_window_attn",
    )(sink, hq, hq, hq, hq, hq, hq, hq, ck, cv)


def _diff_attn_kernel(*refs, lam_init, has_ctx):
    if has_ctx:
        q_ref, k_ref, v_ref, ck_ref, cv_ref, lam_ref, g_ref, o_ref, vaug, cvaug = refs
    else:
        q_ref, k_ref, v_ref, lam_ref, g_ref, o_ref, vaug = refs

    @pl.when(pl.program_id(1) == 0)
    def _():
        ones = jnp.ones((v_ref.shape[0], DIFF_V), BF16)
        for h in range(DIFF_HEADS):
            vaug[h, :, :DIFF_V] = v_ref[:, h * DIFF_V:(h + 1) * DIFF_V]
            vaug[h, :, DIFF_V:] = ones
        if has_ctx:
            cones = jnp.ones((PAST_LEN, DIFF_V), BF16)
            for h in range(DIFF_HEADS):
                cvaug[h, :, :DIFF_V] = cv_ref[:, h * DIFF_V:(h + 1) * DIFF_V].astype(BF16)
                cvaug[h, :, DIFF_V:] = cones

    lv = lam_ref[...]
    lam = (jnp.exp(jnp.sum(lv[0:1] * lv[1:2], axis=-1, keepdims=True))
           - jnp.exp(jnp.sum(lv[2:3] * lv[3:4], axis=-1, keepdims=True)) + lam_init)
    g = g_ref[...]
    outs = []
    for h in range(DIFF_HEADS):
        comps = []
        for c in range(2):
            lo = h * DIFF_V + c * HEAD_DIM
            q = q_ref[:, lo:lo + HEAD_DIM]
            s = lax.dot_general(q, k_ref[:, lo:lo + HEAD_DIM], _NT, preferred_element_type=F32)
            m = jnp.max(s, axis=-1, keepdims=True)
            if has_ctx:
                sc = lax.dot_general(q, ck_ref[:, lo:lo + HEAD_DIM].astype(BF16), _NT,
                                     preferred_element_type=F32)
                m = jnp.maximum(m, jnp.max(sc, axis=-1, keepdims=True))
            oa = jnp.dot(jnp.exp2(s - m).astype(BF16), vaug[h], preferred_element_type=F32)
            if has_ctx:
                oa = oa + jnp.dot(jnp.exp2(sc - m).astype(BF16), cvaug[h], preferred_element_type=F32)
            comps.append((oa[:, :DIFF_V], oa[:, DIFF_V:DIFF_V + 1]))
        (o0, d0), (o1, d1) = comps
        o = o0 / d0 - o1 * (lam / d1)
        ms = jnp.mean(o * o, axis=-1, keepdims=True)
        outs.append(((o * lax.rsqrt(ms + EPS)) * g) * (1.0 - lam_init))
    o_ref[...] = jnp.concatenate(outs, axis=1).astype(BF16)


def _diff_attn(h1, lam_vecs, subln_g, lam_init, ctx):
    g = subln_g.reshape(1, DIFF_V)
    small = [_const_spec((4, HEAD_DIM)), _const_spec((1, DIFF_V))]
    if ctx is None:
        grid = (BATCH, 1)
        in_specs = [pl.BlockSpec((SEQ, D_MODEL), lambda b, i: (b, 0)),
                    pl.BlockSpec((SEQ, D_MODEL), lambda b, i: (b, 1)),
                    pl.BlockSpec((SEQ, D_MODEL), lambda b, i: (b, 2))] + small
        out_specs = pl.BlockSpec((SEQ, D_MODEL), lambda b, i: (b, 0))
        rows = NP
        args = (h1, h1, h1, lam_vecs, g)
        name = "diff_attn_prompt"
    else:
        nq = DEC_SEQ // TQ
        q0 = NP // TQ
        s0 = NP // DEC_SEQ
        ctx_spec = pl.BlockSpec((None, PAST_LEN, D_MODEL), lambda b, i: (b, 0, 0))
        grid = (DEC_BATCH, nq)
        in_specs = [pl.BlockSpec((TQ, D_MODEL), lambda b, i: (q0 + b * nq + i, 0)),
                    pl.BlockSpec((DEC_SEQ, D_MODEL), lambda b, i: (s0 + b, 1)),
                    pl.BlockSpec((DEC_SEQ, D_MODEL), lambda b, i: (s0 + b, 2)),
                    ctx_spec, ctx_spec] + small
        out_specs = pl.BlockSpec((TQ, D_MODEL), lambda b, i: (b * nq + i, 0))
        rows = NS
        args = (h1, h1, h1, ctx[0], ctx[1], lam_vecs, g)
        name = "diff_attn_sample"
    tk = SEQ if ctx is None else DEC_SEQ
    scratch = [pltpu.VMEM((DIFF_HEADS, tk, 2 * DIFF_V), BF16)]
    if ctx is not None:
        scratch.append(pltpu.VMEM((DIFF_HEADS, PAST_LEN, 2 * DIFF_V), BF16))
    return pl.pallas_call(
        functools.partial(_diff_attn_kernel, lam_init=lam_init, has_ctx=ctx is not None),
        grid=grid, in_specs=in_specs, out_specs=out_specs,
        scratch_shapes=scratch,
        out_shape=jax.ShapeDtypeStruct((rows, D_MODEL), BF16),
        compiler_params=_cparams(("arbitrary", "arbitrary")),
        name=name,
    )(*args)


def _store_row_tiles(ref, val, rows):
    for s in range(ROW_TILES):
        ref[pl.ds(s, rows, stride=ROW_TILES), :] = val[:, s * LANES:(s + 1) * LANES]


def _post_kernel(*refs, layer):
    if layer == 0:
        (a_ref, op_ref, os_ref, xa_ref, xb_ref, g1_ref, sh2_ref, sc2_ref, w_ref, n2g_ref, rw_ref, rb_ref,
         xo_ref, xn_ref, eid_ref, gate_ref, cnt_ref) = refs
    else:
        (op_ref, os_ref, xa_ref, xb_ref, g1_ref, sh2_ref, sc2_ref, w_ref, n2g_ref, rw_ref, rb_ref,
         xo_ref, xn_ref, eid_ref, gate_ref, cnt_ref) = refs
    i = pl.program_id(0)
    o = jnp.where(i < NP // TM, op_ref[...], os_ref[...])
    if layer == 0:
        mix = (jnp.dot(a_ref[...], w_ref[:CONV_CH, :], preferred_element_type=F32)
               + jnp.dot(o, w_ref[CONV_CH:, :], preferred_element_type=F32))
    else:
        mix = jnp.dot(o, w_ref[...], preferred_element_type=F32)
    x1 = _pick_x(xa_ref, xb_ref, TM) + g1_ref[0] * mix
    xo_ref[...] = x1
    xn = _ada_norm(x1, n2g_ref[...], sh2_ref[0], sc2_ref[0])
    _store_row_tiles(xn_ref, xn, TM)
    rw = rw_ref[...]
    xh = xn.astype(BF16)
    xl = (xn - xh.astype(F32)).astype(BF16)
    wh = rw.astype(BF16)
    wl = (rw - wh.astype(F32)).astype(BF16)
    logits = (jnp.dot(xh, wh, preferred_element_type=F32)
              + (jnp.dot(xl, wh, preferred_element_type=F32)
                 + jnp.dot(xh, wl, preferred_element_type=F32))) + rb_ref[...]
    lane = lax.broadcasted_iota(I32, logits.shape, 1)
    vals, idxs = [], []
    picked = jnp.zeros(logits.shape, F32)
    for _ in range(TOP_K):
        m = jnp.max(logits, axis=-1, keepdims=True)
        idx = jnp.min(jnp.where(logits == m, lane, N_EXPERTS), axis=-1, keepdims=True)
        vals.append(m)
        idxs.append(idx)
        hit = lane == idx
        picked = picked + hit.astype(F32)
        logits = jnp.where(hit, -jnp.inf, logits)
    es = [jnp.exp(v - vals[0]) for v in vals]
    den = es[0] + es[1] + es[2] + es[3]
    eid_ref[...] = jnp.concatenate(idxs, axis=1)
    gate_ref[...] = jnp.concatenate([e / den for e in es], axis=1)

    @pl.when(i == 0)
    def _():
        cnt_ref[...] = jnp.zeros_like(cnt_ref)

    cnt_ref[...] += jnp.sum(picked, axis=0, keepdims=True)


def _post(layer, mix_parts, xa, xb, mods, w_bf, n2g, rw, rb):
    npb = NP // TM
    if layer == 0:
        a, o_p, o_s = mix_parts
        wo = 512
        mix_specs = [pl.BlockSpec((TM, CONV_CH), lambda i: (i, 0))]
        mix_args = [a, o_p, o_s]
    else:
        o_p, o_s = mix_parts
        wo = D_MODEL
        mix_specs = []
        mix_args = [o_p, o_s]
    mix_specs += [pl.BlockSpec((TM, wo), lambda i: (jnp.minimum(i, npb - 1), 0)),
                  pl.BlockSpec((TM, wo), lambda i: (jnp.maximum(i - npb, 0), 0))]
    return pl.pallas_call(
        functools.partial(_post_kernel, layer=layer),
        grid=(N // TM,),
        in_specs=mix_specs + _x_specs(xa is xb, TM) + [
                              _mod_spec(2, TM), _mod_spec(3, TM), _mod_spec(4, TM),
                              _const_spec((D_MODEL, D_MODEL)), _const_spec((1, D_MODEL)),
                              _const_spec((D_MODEL, N_EXPERTS)), _const_spec((1, N_EXPERTS))],
        out_specs=(pl.BlockSpec((TM, D_MODEL), lambda i: (i, 0)),
                   pl.BlockSpec((TM * ROW_TILES, LANES), lambda i: (i, 0)),
                   pl.BlockSpec((TM, TOP_K), lambda i: (i, 0)),
                   pl.BlockSpec((TM, TOP_K), lambda i: (i, 0)),
                   _const_spec((1, N_EXPERTS))),
        out_shape=(jax.ShapeDtypeStruct((N, D_MODEL), F32),
                   jax.ShapeDtypeStruct((N * ROW_TILES, LANES), F32),
                   jax.ShapeDtypeStruct((N, TOP_K), I32),
                   jax.ShapeDtypeStruct((N, TOP_K), F32),
                   jax.ShapeDtypeStruct((1, N_EXPERTS), F32)),
        compiler_params=_cparams(("arbitrary",)),
        name="post_l%d" % layer,
    )(*mix_args, xa, xb, mods, mods, mods, w_bf, n2g.reshape(1, -1), rw, rb.reshape(1, -1))


def _slots_kernel(cnt_ref, bst_ref, eidt_ref, base_ref, triu_ref, slot_ref, carry, dvm, *scratch):
    dsm = (scratch[:TOP_K], scratch[TOP_K:2 * TOP_K])
    sem = scratch[2 * TOP_K]
    j = pl.program_id(0)
    n_tiles = pl.num_programs(0) - 1

    def copies(par):
        return [pltpu.make_async_copy(dvm.at[par, k], dsm[par][k], sem.at[par, k]) for k in range(TOP_K)]

    @pl.when(j == 0)
    def _():
        carry[...] = jnp.zeros_like(carry)

    for par in range(2):
        @pl.when(jnp.logical_and(j < n_tiles, j % 2 == par))
        def _():
            e_iota = lax.broadcasted_iota(I32, (N_EXPERTS, RT), 0)
            ohs = [(eidt_ref[k:k + 1, :] == e_iota).astype(F32) for k in range(TOP_K)]
            ohsum = ohs[0] + ohs[1] + ohs[2] + ohs[3]
            cum = jnp.dot(ohsum.astype(BF16), triu_ref[...], preferred_element_type=F32)
            tot = cum + carry[...] + base_ref[...]
            dvm[par] = jnp.concatenate(
                [jnp.sum(oh * tot, axis=0, keepdims=True) for oh in ohs], axis=0).astype(I32)
            carry[...] += jnp.sum(ohsum, axis=1, keepdims=True)
            for cp in copies(par):
                cp.start()

    for par in range(2):
        @pl.when(jnp.logical_and(j > 0, (j - 1) % 2 == par))
        def _():
            for cp in copies(par):
                cp.wait()

            def scatter(c, carry_):
                t0 = c * SCATTER_UNROLL
                a0 = ((j - 1) * RT + t0) * TOP_K
                for u in range(SCATTER_UNROLL):
                    for k in range(TOP_K):
                        slot_ref[dsm[par][k][t0 + u]] = a0 + (u * TOP_K + k)
                return carry_

            lax.fori_loop(0, RT // SCATTER_UNROLL, scatter, 0)

    @pl.when(j == pl.num_programs(0) - 1)
    def _():
        def pad(s, c):
            slot_ref[s] = (N + s % SLOT_LEAD) * TOP_K + (TOP_K - 1)
            return c

        def per_expert(e, c):
            cnt = cnt_ref[e]
            b0 = bst_ref[e]
            lax.fori_loop(SLOT_LEAD + b0 * TME + cnt,
                          SLOT_LEAD + (b0 + (cnt + TME - 1) // TME) * TME, pad, 0)
            return c

        lax.fori_loop(0, SLOT_LEAD, pad, 0)
        lax.fori_loop(0, N_EXPERTS, per_expert, 0)
        lax.fori_loop(SLOT_LEAD + bst_ref[N_EXPERTS] * TME, N_SLOTS, pad, 0)


def _route(eid, counts_f):
    counts = counts_f.reshape(-1).astype(I32)
    nblk = (counts + TME - 1) // TME
    bend = jnp.cumsum(nblk)
    bstart = bend - nblk
    bst = jnp.concatenate([bstart, bend[-1:]]).astype(I32)
    base = (bstart * TME + SLOT_LEAD).astype(F32).reshape(N_EXPERTS, 1)
    ar = jnp.arange(RT, dtype=I32)
    triu = (ar[:, None] < ar[None, :]).astype(BF16)
    grid_spec = pltpu.PrefetchScalarGridSpec(
        num_scalar_prefetch=2,
        grid=(N // RT + 1,),
        in_specs=[pl.BlockSpec((TOP_K, RT), lambda j, c, b: (0, jnp.minimum(j, N // RT - 1))),
                  _const_spec((N_EXPERTS, 1)),
                  _const_spec((RT, RT))],
        out_specs=pl.BlockSpec(memory_space=pltpu.SMEM),
        scratch_shapes=[pltpu.VMEM((N_EXPERTS, 1), F32),
                        pltpu.VMEM((2, TOP_K, RT), I32)]
        + [pltpu.SMEM((RT,), I32)] * (2 * TOP_K)
        + [pltpu.SemaphoreType.DMA((2, TOP_K))],
    )
    slot_asg = pl.pallas_call(
        _slots_kernel,
        grid_spec=grid_spec,
        out_shape=jax.ShapeDtypeStruct((N_SLOTS,), I32),
        compiler_params=_cparams(("arbitrary",)),
        name="slots",
    )(counts, bst, eid.T, base, triu)
    return slot_asg, bst, counts


def _row_queue(r):
    return r % 2 if isinstance(r, int) else 0


def _row_tile(idx):
    if isinstance(idx, int):
        return pl.ds(idx * ROW_TILES, ROW_TILES)
    return pl.ds(pl.multiple_of(idx * ROW_TILES, ROW_TILES), ROW_TILES)


def _expert_kernel(slot_ref, bst_ref, cnt_ref, xn_hbm, upw_ref, upb_ref, dww_ref, dwb_ref,
                   y_hbm, gbuf0, gbuf1, sbuf0, sbuf1, upbf, dwbf, xb_ref, hdn_ref, gsem, ssem):
    e = pl.program_id(0)
    nb = (cnt_ref[e] + TME - 1) // TME
    b0 = bst_ref[e]
    gbufs = (gbuf0, gbuf1)
    sbufs = (sbuf0, sbuf1)

    def gather_row(g, buf, r):
        v = slot_ref[(g + 2) * TME + r]
        tok = jnp.minimum(lax.shift_right_logical(v, TOP_K_SHIFT), N - 1)
        pltpu.make_async_copy(xn_hbm.at[_row_tile(tok), :], gbufs[buf].at[_row_tile(r), :],
                              gsem.at[buf]).start(priority=_row_queue(r))

    def scatter_row(g, buf, r):
        v = slot_ref[(g + 2) * TME + r]
        dst = (v & (TOP_K - 1)) * N + lax.shift_right_logical(v, TOP_K_SHIFT)
        pltpu.make_async_copy(sbufs[buf].at[_row_tile(r), :], y_hbm.at[_row_tile(dst), :],
                              ssem.at[buf]).start(priority=_row_queue(r))

    def rolled(row_fn, g, buf):
        def body(r, c):
            row_fn(g, buf, r)
            return c
        lax.fori_loop(0, TME, body, 0)

    def wait_gather(buf):
        pltpu.make_async_copy(xn_hbm.at[pl.ds(0, TME * ROW_TILES), :], gbufs[buf], gsem.at[buf]).wait()

    def wait_scatter(buf):
        pltpu.make_async_copy(sbufs[buf], y_hbm.at[pl.ds(0, TME * ROW_TILES), :], ssem.at[buf]).wait()

    @pl.when(e == 0)
    def _():
        sbuf0[...] = jnp.zeros_like(sbuf0)
        sbuf1[...] = jnp.zeros_like(sbuf1)
        rolled(gather_row, 0, 0)
        rolled(scatter_row, -2, 0)

    @pl.when(nb > 0)
    def _():
        upbf[...] = upw_ref[0].astype(BF16)
        dwbf[...] = dww_ref[0].astype(BF16)

    def work(g, cur):
        oth = 1 - cur
        wait_gather(cur)
        xb_ref[...] = jnp.concatenate(
            [gbufs[cur][pl.ds(s, TME, stride=ROW_TILES), :] for s in range(ROW_TILES)], axis=1).astype(BF16)

        def phase(ph, c):
            @pl.when(ph == 0)
            def _():
                for r in range(TME):
                    gather_row(g + 1, oth, r)
                gu = jnp.dot(xb_ref[...], upbf[...], preferred_element_type=F32) + upb_ref[0]
                gg = jnp.minimum(gu[:, :D_FF], SWIGLU_LIMIT)
                lin = jnp.clip(gu[:, D_FF:], -SWIGLU_LIMIT, SWIGLU_LIMIT)
                hdn_ref[...] = (gg * jax.nn.sigmoid(SWIGLU_ALPHA * gg) * (lin + 1.0)).astype(BF16)

            @pl.when(ph == 1)
            def _():
                for r in range(TME):
                    scatter_row(g - 1, oth, r)
                y = jnp.dot(hdn_ref[...], dwbf[...], preferred_element_type=F32) + dwb_ref[0]
                wait_scatter(cur)
                for s in range(ROW_TILES):
                    sbufs[cur][pl.ds(s, TME, stride=ROW_TILES), :] = y[:, s * LANES:(s + 1) * LANES]
            return c

        lax.fori_loop(0, 2, phase, 0)

    def block(j, carry):
        g = b0 + j
        for parity in range(2):
            @pl.when(g % 2 == parity)
            def _():
                work(g, parity)
        return carry

    lax.fori_loop(0, nb, block, 0)

    @pl.when(e == pl.num_programs(0) - 1)
    def _():
        g_end = bst_ref[N_EXPERTS]
        for parity in range(2):
            @pl.when(g_end % 2 == parity)
            def _():
                rolled(scatter_row, g_end - 1, 1 - parity)
                wait_gather(parity)
        wait_scatter(0)
        wait_scatter(1)


def _experts(slot_asg, bst, counts, xn_tiles, up_w, up_b, down_w, down_b):
    grid_spec = pltpu.PrefetchScalarGridSpec(
        num_scalar_prefetch=3,
        grid=(N_EXPERTS,),
        in_specs=[pl.BlockSpec(memory_space=pl.ANY),
                  pl.BlockSpec((1, D_MODEL, 2 * D_FF), lambda e, *_: (e, 0, 0)),
                  pl.BlockSpec((1, 1, 2 * D_FF), lambda e, *_: (e, 0, 0)),
                  pl.BlockSpec((1, D_FF, D_MODEL), lambda e, *_: (e, 0, 0)),
                  pl.BlockSpec((1, 1, D_MODEL), lambda e, *_: (e, 0, 0))],
        out_specs=pl.BlockSpec(memory_space=pl.ANY),
        scratch_shapes=[pltpu.VMEM((TME * ROW_TILES, LANES), F32),
                        pltpu.VMEM((TME * ROW_TILES, LANES), F32),
                        pltpu.VMEM((TME * ROW_TILES, LANES), F32),
                        pltpu.VMEM((TME * ROW_TILES, LANES), F32),
                        pltpu.VMEM((D_MODEL, 2 * D_FF), BF16),
                        pltpu.VMEM((D_FF, D_MODEL), BF16),
                        pltpu.VMEM((TME, D_MODEL), BF16),
                        pltpu.VMEM((TME, D_FF), BF16),
                        pltpu.SemaphoreType.DMA((2,)),
                        pltpu.SemaphoreType.DMA((2,))],
    )
    return pl.pallas_call(
        _expert_kernel,
        grid_spec=grid_spec,
        out_shape=jax.ShapeDtypeStruct(((N_ASG + SLOT_LEAD) * ROW_TILES, LANES), F32),
        compiler_params=_cparams(("arbitrary",)),
        name="experts",
    )(slot_asg, bst, counts, xn_tiles, up_w, up_b.reshape(N_EXPERTS, 1, -1),
      down_w, down_b.reshape(N_EXPERTS, 1, -1))


def _combine_kernel(x_ref, gate_ref, g2_ref, fg_ref, y0_ref, y1_ref, y2_ref, y3_ref, *o_refs, final):
    gate = gate_ref[...]
    cols = []
    for s in range(ROW_TILES):
        acc = None
        for k, y_ref in enumerate((y0_ref, y1_ref, y2_ref, y3_ref)):
            term = gate[:, k:k + 1] * y_ref[pl.ds(s, TMC, stride=ROW_TILES), :]
            acc = term if acc is None else acc + term
        cols.append(acc)
    x2 = x_ref[...] + g2_ref[0] * jnp.concatenate(cols, axis=1)
    if not final:
        o_refs[0][...] = x2
        return
    ms = jnp.mean(x2 * x2, axis=-1, keepdims=True)
    y = x2 * lax.rsqrt(ms + EPS) * fg_ref[...]
    is_prompt = pl.program_id(0) < NP // TMC

    @pl.when(is_prompt)
    def _():
        o_refs[0][...] = y

    @pl.when(jnp.logical_not(is_prompt))
    def _():
        o_refs[1][...] = y


def _combine(x, gates, mods, final_g, y_tiles, final):
    nblk = N // TMC

    npb = NP // TMC

    def y_spec(k):
        return pl.BlockSpec((TMC * ROW_TILES, LANES), lambda i: (k * nblk + i, 0))

    if final:
        out_specs = (pl.BlockSpec((TMC, D_MODEL), lambda i: (jnp.minimum(i, npb - 1), 0)),
                     pl.BlockSpec((TMC, D_MODEL), lambda i: (jnp.maximum(i - npb, 0), 0)))
        out_shape = (jax.ShapeDtypeStruct((NP, D_MODEL), F32), jax.ShapeDtypeStruct((NS, D_MODEL), F32))
    else:
        out_specs = pl.BlockSpec((TMC, D_MODEL), lambda i: (i, 0))
        out_shape = jax.ShapeDtypeStruct((N, D_MODEL), F32)

    return pl.pallas_call(
        functools.partial(_combine_kernel, final=final),
        grid=(nblk,),
        in_specs=[pl.BlockSpec((TMC, D_MODEL), lambda i: (i, 0)),
                  pl.BlockSpec((TMC, TOP_K), lambda i: (i, 0)),
                  _mod_spec(5, TMC),
                  _const_spec((1, D_MODEL)),
                  y_spec(0), y_spec(1), y_spec(2), y_spec(3)],
        out_specs=out_specs,
        out_shape=out_shape,
        compiler_params=_cparams(("arbitrary",)),
        name="combine",
    )(x, gates, mods, final_g.reshape(1, -1), y_tiles, y_tiles, y_tiles, y_tiles)


def _rope_tables():
    pos = jnp.arange(DEC_SEQ)
    r = (pos // GRID_W).astype(F32)
    col = (pos % GRID_W).astype(F32)
    quarter = HEAD_DIM // 4
    inv = ROPE_BASE ** (-jnp.arange(quarter, dtype=F32) / quarter)
    ar = r[:, None] * inv
    ac = col[:, None] * inv
    ang = jnp.concatenate([ar, ar, ac, ac], axis=-1)
    cos = jnp.tile(jnp.cos(ang), (1, LANES // HEAD_DIM))
    sin = jnp.tile(jnp.sin(ang), (1, LANES // HEAD_DIM))
    first = (jnp.arange(LANES) % 32) < 16
    sa = jnp.where(first[None, :], -sin, 0.0)
    sb = jnp.where(first[None, :], 0.0, sin)
    return cos, sa, sb


def _moe(x, xn, eid, gates, counts, mods, final_g, up_w, up_b, down_w, down_b, final):
    slot_asg, bst, counts_i = _route(eid, counts)
    y = _experts(slot_asg, bst, counts_i, xn, up_w, up_b, down_w, down_b)
    return _combine(x, gates, mods, final_g, y, final)


@jax.jit
def kernel(x_prompt, x_sample, cache_l0_k, cache_l0_v, cache_l1_k, cache_l1_v, c, c_ctx, final_g,
           l0_mod_w, l0_mod_b, l0_norm1_g, l0_w_in, l0_conv_w, l0_conv_b, l0_cnorm_g, l0_cnorm_b,
           l0_sink, l0_w_out, l0_norm2_g, l0_router_w, l0_router_b, l0_up_w, l0_up_b, l0_down_w,
           l0_down_b,
           l1_mod_w, l1_mod_b, l1_norm1_g, l1_w_in, l1_lam_q1, l1_lam_k1, l1_lam_q2, l1_lam_k2,
           l1_subln_g, l1_w_out, l1_norm2_g, l1_router_w, l1_router_b, l1_up_w, l1_up_b, l1_down_w,
           l1_down_b):
    xp0 = x_prompt.reshape(NP, D_MODEL)
    xs0 = x_sample.reshape(NS, D_MODEL)
    cond16 = jnp.concatenate([c, c_ctx[None, :], jnp.zeros((16 - DEC_BATCH - 1, D_MODEL), F32)], axis=0)
    rope_tabs = _rope_tables()

    mods = _modulation(cond16, l0_mod_w, l0_mod_b)
    ha, hq, st0 = _inproj(xp0, xs0, mods, l0_norm1_g, l0_w_in.astype(BF16), rope_tabs, 0)
    a = _conv(ha, l0_conv_w, l0_conv_b, l0_cnorm_g, l0_cnorm_b)
    o_p = _l0_prompt_attn(hq, l0_sink)
    o_s = _l0_window_attn(hq, l0_sink, cache_l0_k.reshape(DEC_BATCH, PAST_LEN, LANES),
                          cache_l0_v.reshape(DEC_BATCH, PAST_LEN, LANES))
    x, xn, eid, gates, counts = _post(0, (a, o_p, o_s), xp0, xs0, mods, l0_w_out.astype(BF16), l0_norm2_g,
                                      l0_router_w, l0_router_b)
    x = _moe(x, xn, eid, gates, counts, mods, final_g, l0_up_w, l0_up_b, l0_down_w, l0_down_b, False)
    state_l0_k = st0[:, :LANES].reshape(BATCH, SEQ, WIN_KV, HEAD_DIM)
    state_l0_v = st0[:, LANES:].reshape(BATCH, SEQ, WIN_KV, HEAD_DIM)

    lam_init = 0.8 - 0.6 * math.exp(-0.3 * 1)
    mods = _modulation(cond16, l1_mod_w, l1_mod_b)
    h1, st1 = _inproj(x, x, mods, l1_norm1_g, l1_w_in.astype(BF16), rope_tabs, 1)
    lam_vecs = jnp.stack([l1_lam_q1, l1_lam_k1, l1_lam_q2, l1_lam_k2], axis=0)
    o_p = _diff_attn(h1, lam_vecs, l1_subln_g, lam_init, None)
    o_s = _diff_attn(h1, lam_vecs, l1_subln_g, lam_init,
                     (cache_l1_k.reshape(DEC_BATCH, PAST_LEN, D_MODEL),
                      cache_l1_v.reshape(DEC_BATCH, PAST_LEN, D_MODEL)))
    x, xn, eid, gates, counts = _post(1, (o_p, o_s), x, x, mods, l1_w_out.astype(BF16), l1_norm2_g,
                                      l1_router_w, l1_router_b)
    yp, ys = _moe(x, xn, eid, gates, counts, mods, final_g, l1_up_w, l1_up_b, l1_down_w, l1_down_b, True)
    state_l1_k = st1[:, :D_MODEL].reshape(BATCH, SEQ, DIFF_HEADS, 2, HEAD_DIM)
    state_l1_v = st1[:, D_MODEL:].reshape(BATCH, SEQ, DIFF_HEADS, DIFF_V)

    y_prompt = yp.reshape(BATCH, SEQ, D_MODEL)
    y_sample = ys.reshape(DEC_BATCH, DEC_SEQ, D_MODEL)
    return (y_prompt, y_sample, state_l0_k, state_l0_v, state_l1_k, state_l1_v)
```
